```python
import math
import jax, jax.numpy as jnp
from jax import lax
import numpy as np

D_MODEL = 1024
BATCH = 32
SEQ = 2048
DEPTH = 1

CTX_LEN = 256
GRID_W = 64
POS_BASE = 10000.0
EPS = 1e-6
N_MOD = 6
ML_HEADS = 4
ML_DV = (D_MODEL // 2) // ML_HEADS
ML_DK = ML_DV // 2
ML_QK = ML_HEADS * ML_DK
ML_V = ML_HEADS * ML_DV
ML_CONV = 3
ML_CHUNK = 64
ML_F_BIAS_LO = 3.0
ML_F_BIAS_HI = 6.0
GLA_HEADS = 4
GLA_DV = (D_MODEL - D_MODEL // 2) // GLA_HEADS
GLA_DK = GLA_DV // 2
GLA_QK = GLA_HEADS * GLA_DK
GLA_V = GLA_HEADS * GLA_DV
GLA_RANK = 16
GLA_NORMALIZER = 16.0
GLA_CHUNK = 64
MIX_WIDTH = ML_V + GLA_V
IN_SIZES = (ML_QK, ML_QK, ML_V, ML_V, 4 * ML_HEADS, GLA_QK, GLA_QK, GLA_V, GLA_V, 2 * GLA_RANK)
IN_WIDTH = 2 * ML_QK + 2 * ML_V + 4 * ML_HEADS + 2 * GLA_QK + 2 * GLA_V + 2 * GLA_RANK
N_EXPERTS = 32
TOP_K = 4
D_EXPERT = D_MODEL
SWIGLU_LIMIT = 7.0
SWIGLU_ALPHA = 1.702

kernel_name = 'hybrid_mlstm_gla_moe_dit_block'


def _rms(x, g):
    xf = x.astype(jnp.float32)
    y = xf * lax.rsqrt(jnp.mean(xf * xf, axis=-1, keepdims=True) + EPS)
    return (y * g.astype(jnp.float32)).astype(x.dtype)


def _modulate(h, shift, scale):
    return h * (1 + scale) + shift


def _adaln(cond, w, b):
    return (jax.nn.silu(cond) @ w + b).reshape(cond.shape[0], N_MOD, 1, w.shape[0])


def _grid_sincos(rows, d, dtype):
    f32 = jnp.float32
    nf = d // 4
    omega = 1.0 / (POS_BASE ** (jnp.arange(nf, dtype=f32) / nf))
    r = jnp.broadcast_to(jnp.arange(rows, dtype=f32)[:, None, None] * omega, (rows, GRID_W, nf))
    cl = jnp.broadcast_to(jnp.arange(GRID_W, dtype=f32)[None, :, None] * omega, (rows, GRID_W, nf))
    pe = jnp.concatenate([jnp.sin(r), jnp.cos(r), jnp.sin(cl), jnp.cos(cl)], axis=-1)
    return pe.reshape(rows * GRID_W, d).astype(dtype)


def _split_cols(z):
    offs = [int(o) for o in np.cumsum(IN_SIZES)[:-1]]
    return jnp.split(z, offs, axis=-1)


def _heads(a, n_heads):
    b, t, w = a.shape
    return a.reshape(b, t, n_heads, w // n_heads).transpose(0, 2, 1, 3)


def _merge_heads_norm(h, g, dtype):
    b, nh, t, d = h.shape
    hf = h.astype(jnp.float32)
    hf = hf * lax.rsqrt(jnp.mean(hf * hf, axis=-1, keepdims=True) + EPS)
    hf = hf.transpose(0, 2, 1, 3).reshape(b, t, nh * d)
    return (hf * g.astype(jnp.float32)).astype(dtype)


def _centred_conv(a, w, bias):
    k = w.shape[0]
    p = k // 2
    t = a.shape[1]
    ap = jnp.pad(a, ((0, 0), (p, p), (0, 0)))
    out = bias
    for i in range(k):
        out = out + ap[:, i:i + t] * w[i]
    return out


def _to_chunks(a, L):
    t = a.shape[2]
    return jnp.moveaxis(a.reshape(a.shape[:2] + (t // L, L) + a.shape[3:]), 2, 0)


def _from_chunks(hs):
    nc, b, h, L, d = hs.shape
    return jnp.moveaxis(hs, 0, 2).reshape(b, h, nc * L, d)


def _mlstm_scan(q, k, v, ig, lf, state):
    L = ML_CHUNK
    mask = jnp.tril(jnp.ones((L, L), dtype=bool))
    f32 = jnp.float32

    def step(carry, inp):
        C, n, m = carry
        qc, kc, vc, ic, fc = inp
        b = jnp.cumsum(fc, axis=-1)
        inter = b + m[..., None]
        dmat = b[..., :, None] - b[..., None, :] + ic[..., None, :]
        dmat = jnp.where(mask, dmat, -jnp.inf)
        m_t = jnp.maximum(inter, jnp.max(dmat, axis=-1))
        w_intra = jnp.exp(dmat - m_t[..., None])
        w_inter = jnp.exp(inter - m_t)
        s = jnp.einsum('bhtd,bhsd->bhts', qc, kc) * w_intra
        num = jnp.einsum('bhts,bhsv->bhtv', s, vc) + w_inter[..., None] * jnp.einsum('bhtd,bhdv->bhtv', qc, C)
        den = jnp.sum(s, axis=-1) + w_inter * jnp.einsum('bhtd,bhd->bht', qc, n)
        h = num / jnp.maximum(jnp.abs(den), jnp.exp(-m_t))[..., None]
        bL = b[..., -1]
        dec = bL[..., None] - b + ic
        m_new = jnp.maximum(bL + m, jnp.max(dec, axis=-1))
        wk = jnp.exp(dec - m_new[..., None])
        wc = jnp.exp(bL + m - m_new)
        C_new = wc[..., None, None] * C + jnp.einsum('bhs,bhsd,bhsv->bhdv', wk, kc, vc)
        n_new = wc[..., None] * n + jnp.einsum('bhs,bhsd->bhd', wk, kc)
        return (C_new, n_new, m_new), h

    xs = tuple(_to_chunks(a.astype(f32), L) for a in (q, k, v, ig, lf))
    carry, hs = lax.scan(step, state, xs)
    return _from_chunks(hs), carry


def _gla_scan(q, k, v, la, state):
    L = GLA_CHUNK
    mask = jnp.tril(jnp.ones((L, L), dtype=bool))[:, :, None]
    f32 = jnp.float32

    def step(S, inp):
        qc, kc, vc, lc = inp
        b = jnp.cumsum(lc, axis=2)
        rel = b[:, :, :, None, :] - b[:, :, None, :, :]
        rel = jnp.where(mask, rel, -jnp.inf)
        A = jnp.einsum('bhtd,bhsd,bhtsd->bhts', qc, kc, jnp.exp(rel))
        o = jnp.einsum('bhts,bhsv->bhtv', A, vc) + jnp.einsum('bhtd,bhdv->bhtv', qc * jnp.exp(b), S)
        bL = b[:, :, -1]
        S_new = jnp.exp(bL)[..., None] * S + jnp.einsum('bhsd,bhsv->bhdv', kc * jnp.exp(bL[:, :, None] - b), vc)
        return S_new, o

    xs = tuple(_to_chunks(a.astype(f32), L) for a in (q, k, v, la))
    S, hs = lax.scan(step, state, xs)
    return _from_chunks(hs), S


def _bidirectional(scan, ctx_f, ctx_b, lat_f, lat_b, state0):
    flip = lambda a: jnp.flip(a, axis=2)
    h_cf, s_f = scan(*ctx_f, state0)
    h_lf, _ = scan(*lat_f, s_f)
    h_cb, s_b = scan(*[flip(a) for a in ctx_b], state0)
    h_lb, _ = scan(*[flip(a) for a in lat_b], s_b)
    return h_cf + flip(h_cb), h_lf + flip(h_lb)


def _project(u, w_in, ml_conv_w, ml_conv_b, ml_gate_b, gla_gate_w2, gla_gate_b):
    f32 = jnp.float32
    b, t, _ = u.shape
    z = u @ w_in
    ml_q, ml_k, ml_v, ml_o, ml_g, g_q, g_k, g_v, g_g, g_lr = _split_cols(z)
    qk = jax.nn.silu(_centred_conv(jnp.concatenate([ml_q, ml_k], axis=-1), ml_conv_w, ml_conv_b))
    mq = _heads(qk[..., :ML_QK], ML_HEADS) * (ML_DK ** -0.5)
    mk = _heads(qk[..., ML_QK:], ML_HEADS)
    mv = _heads(ml_v, ML_HEADS)
    gates = (ml_g.astype(f32).reshape(b, t, 4, ML_HEADS) + ml_gate_b.astype(f32)).transpose(2, 0, 3, 1)
    ml_fwd = (mq, mk, mv, gates[0], jax.nn.log_sigmoid(gates[1]))
    ml_bwd = (mq, mk, mv, gates[2], jax.nn.log_sigmoid(gates[3]))
    gq = _heads(g_q, GLA_HEADS) * (GLA_DK ** -0.5)
    gk = _heads(g_k, GLA_HEADS)
    gv = _heads(g_v, GLA_HEADS)
    lr = g_lr.astype(f32).reshape(b, t, 2, GLA_RANK)
    pre = jnp.einsum('btzr,zrc->zbtc', lr, gla_gate_w2.astype(f32)) + gla_gate_b.astype(f32)[:, None, None, :]
    la = jax.nn.log_sigmoid(pre) / GLA_NORMALIZER
    la = la.reshape(2, b, t, GLA_HEADS, GLA_DK).transpose(0, 1, 3, 2, 4)
    return ml_fwd, ml_bwd, ml_o, (gq, gk, gv, la[0]), (gq, gk, gv, la[1]), g_g


def _hybrid_mixer(u_lat, u_ctx, w_in, ml_conv_w, ml_conv_b, ml_gate_b, ml_norm_g,
                  gla_gate_w2, gla_gate_b, gla_norm_g, w_out, ctx_out):
    p_lat = _project(u_lat, w_in, ml_conv_w, ml_conv_b, ml_gate_b, gla_gate_w2, gla_gate_b)
    p_ctx = _project(u_ctx, w_in, ml_conv_w, ml_conv_b, ml_gate_b, gla_gate_w2, gla_gate_b)
    b = u_lat.shape[0]
    f32 = jnp.float32
    ml_state0 = (jnp.zeros((b, ML_HEADS, ML_DK, ML_DV), f32), jnp.zeros((b, ML_HEADS, ML_DK), f32),
                 jnp.zeros((b, ML_HEADS), f32))
    gla_state0 = jnp.zeros((b, GLA_HEADS, GLA_DK, GLA_DV), f32)
    ml_c, ml_l = _bidirectional(_mlstm_scan, p_ctx[0], p_ctx[1], p_lat[0], p_lat[1], ml_state0)
    gla_c, gla_l = _bidirectional(_gla_scan, p_ctx[3], p_ctx[4], p_lat[3], p_lat[4], gla_state0)

    def merge(ml_h, ml_o, gla_h, g_g, dtype):
        a = _merge_heads_norm(ml_h, ml_norm_g, dtype) * jax.nn.sigmoid(ml_o)
        g = _merge_heads_norm(gla_h, gla_norm_g, dtype) * jax.nn.silu(g_g)
        return jnp.concatenate([a, g], axis=-1) @ w_out

    y_lat = merge(ml_l, p_lat[2], gla_l, p_lat[5], u_lat.dtype)
    y_ctx = merge(ml_c, p_ctx[2], gla_c, p_ctx[5], u_ctx.dtype) if ctx_out else None
    return y_lat, y_ctx


def _moe(u, router_w, router_b, w_gu, b_gu, w_down, b_down):
    bsz, t, d = u.shape
    tok = u.reshape(bsz * t, d)
    logits = (tok @ router_w + router_b).astype(jnp.float32)
    top_v, top_i = lax.top_k(logits, TOP_K)
    probs = jax.nn.softmax(top_v, axis=-1)
    gates = jnp.sum(jax.nn.one_hot(top_i, N_EXPERTS, dtype=jnp.float32) * probs[..., None], axis=1).astype(tok.dtype)
    out = jnp.zeros_like(tok)
    for e in range(N_EXPERTS):
        gu = tok @ w_gu[e] + b_gu[e]
        gate = jnp.minimum(gu[:, :D_EXPERT], SWIGLU_LIMIT)
        up = jnp.clip(gu[:, D_EXPERT:], -SWIGLU_LIMIT, SWIGLU_LIMIT)
        glu = gate * jax.nn.sigmoid(SWIGLU_ALPHA * gate)
        out = out + gates[:, e:e + 1] * (((up + 1) * glu) @ w_down[e] + b_down[e])
    return out.reshape(bsz, t, d)


def setup_inputs(seed: int = 0) -> dict:
    key = jax.random.key(seed)
    ks = jax.random.split(key, 24)
    f32 = jnp.float32
    nrm = lambda k, s, sc: jax.random.normal(k, s, f32) * sc
    D = D_MODEL
    f_bias = jnp.linspace(ML_F_BIAS_LO, ML_F_BIAS_HI, ML_HEADS, dtype=f32)
    zero_h = jnp.zeros((ML_HEADS,), f32)
    gate_base = jnp.stack([zero_h, f_bias, zero_h, f_bias])
    return {
        'x': nrm(ks[0], (BATCH, SEQ, D), 1.0),
        'c': nrm(ks[1], (BATCH, D), 1.0),
        'ctx': nrm(ks[2], (BATCH, CTX_LEN, D), 1.0),
        'c_ctx': nrm(ks[3], (D,), 1.0),
        'ada_w': nrm(ks[4], (DEPTH, D, N_MOD * D), 0.5 * D ** -0.5),
        'ada_b': nrm(ks[5], (DEPTH, N_MOD * D), 0.02),
        'norm1_g': 1.0 + nrm(ks[6], (DEPTH, D), 0.02),
        'w_in': nrm(ks[7], (DEPTH, D, IN_WIDTH), D ** -0.5),
        'ml_conv_w': nrm(ks[8], (DEPTH, ML_CONV, 2 * ML_QK), ML_CONV ** -0.5),
        'ml_conv_b': nrm(ks[9], (DEPTH, 2 * ML_QK), 0.02),
        'ml_gate_b': gate_base[None] + nrm(ks[10], (DEPTH, 4, ML_HEADS), 0.1),
        'ml_norm_g': 1.0 + nrm(ks[11], (DEPTH, ML_V), 0.02),
        'gla_gate_w2': nrm(ks[12], (DEPTH, 2, GLA_RANK, GLA_QK), GLA_RANK ** -0.5),
        'gla_gate_b': 1.0 + nrm(ks[13], (DEPTH, 2, GLA_QK), 0.1),
        'gla_norm_g': 1.0 + nrm(ks[14], (DEPTH, GLA_V), 0.02),
        'w_out': nrm(ks[15], (DEPTH, MIX_WIDTH, D), MIX_WIDTH ** -0.5),
        'norm2_g': 1.0 + nrm(ks[16], (DEPTH, D), 0.02),
        'router_w': nrm(ks[17], (DEPTH, D, N_EXPERTS), D ** -0.5),
        'router_b': nrm(ks[18], (DEPTH, N_EXPERTS), 0.01),
        'moe_w_gu': nrm(ks[19], (DEPTH, N_EXPERTS, D, 2 * D_EXPERT), D ** -0.5),
        'moe_b_gu': nrm(ks[20], (DEPTH, N_EXPERTS, 2 * D_EXPERT), 0.02),
        'moe_w_down': nrm(ks[21], (DEPTH, N_EXPERTS, D_EXPERT, D), D_EXPERT ** -0.5),
        'moe_b_down': nrm(ks[22], (DEPTH, N_EXPERTS, D), 0.02),
        'final_norm_g': 1.0 + nrm(ks[23], (D,), 0.02),
    }


def reference(x, c, ctx, c_ctx, ada_w, ada_b, norm1_g, w_in, ml_conv_w, ml_conv_b, ml_gate_b,
              ml_norm_g, gla_gate_w2, gla_gate_b, gla_norm_g, w_out, norm2_g, router_w, router_b,
              moe_w_gu, moe_b_gu, moe_w_down, moe_b_down, final_norm_g):
    T = x.shape[1]
    ROWS = T // GRID_W
    x = x + _grid_sincos(ROWS, x.shape[-1], x.dtype)[None]
    for l in range(DEPTH):
        last = l == DEPTH - 1
        m_lat = _adaln(c, ada_w[l], ada_b[l])
        m_ctx = _adaln(c_ctx[None, :], ada_w[l], ada_b[l])
        u_lat = _modulate(_rms(x, norm1_g[l]), m_lat[:, 0], m_lat[:, 1])
        u_ctx = _modulate(_rms(ctx, norm1_g[l]), m_ctx[:, 0], m_ctx[:, 1])
        y_lat, y_ctx = _hybrid_mixer(u_lat, u_ctx, w_in[l], ml_conv_w[l], ml_conv_b[l], ml_gate_b[l],
                                     ml_norm_g[l], gla_gate_w2[l], gla_gate_b[l], gla_norm_g[l], w_out[l],
                                     not last)
        x = x + m_lat[:, 2] * y_lat
        x = x + m_lat[:, 5] * _moe(_modulate(_rms(x, norm2_g[l]), m_lat[:, 3], m_lat[:, 4]), router_w[l],
                                   router_b[l], moe_w_gu[l], moe_b_gu[l], moe_w_down[l], moe_b_down[l])
        if not last:
            ctx = ctx + m_ctx[:, 2] * y_ctx
            ctx = ctx + m_ctx[:, 5] * _moe(_modulate(_rms(ctx, norm2_g[l]), m_ctx[:, 3], m_ctx[:, 4]),
                                           router_w[l], router_b[l], moe_w_gu[l], moe_b_gu[l],
                                           moe_w_down[l], moe_b_down[l])
    return _rms(x, final_norm_g)
```

```python
import functools

import numpy as np
import jax
import jax.numpy as jnp
from jax import lax
from jax.experimental import pallas as pl
from jax.experimental.pallas import tpu as pltpu

F32 = jnp.float32
BF16 = jnp.bfloat16
HIGHEST = lax.Precision.HIGHEST

EPS = 1e-6
GRID_W = 64
POS_BASE = 10000.0
N_MOD = 6
N_HEADS = 4
DK = 64
DV = 128
QK_W = N_HEADS * 2 * DK
V_W = N_HEADS * DV
GROUP_W = QK_W + 2 * V_W
GATE_W = 128
ML_CONV = 3
GLA_RANK = 16
GLA_NORMALIZER = 16.0
TOP_K = 4
SWIGLU_LIMIT = 7.0
SWIGLU_ALPHA = 1.702
LANES = 128
ML_CHUNK = 128
GLA_CHUNK = 64
CONV_ROWS = 128
VMEM_LIMIT = 56 * 1024 * 1024


def _params(sem):
    return pltpu.CompilerParams(dimension_semantics=sem, vmem_limit_bytes=VMEM_LIMIT)


def _sigmoid(x):
    return 1.0 / (1.0 + jnp.exp(-x))


def _log_sigmoid(x):
    return jnp.minimum(x, 0.0) - jnp.log1p(jnp.exp(-jnp.abs(x)))


def _dot(a, b, precision=None):
    return jnp.dot(a, b, preferred_element_type=F32, precision=precision)


def _dot_nt(a, b):
    return lax.dot_general(a, b, (((1,), (1,)), ((), ())), preferred_element_type=F32)


def _dot_tn(a, b):
    return lax.dot_general(a, b, (((0,), (0,)), ((), ())), preferred_element_type=F32)


def _adaln_body(c_ref, w_ref, b_ref, o_ref):
    c = c_ref[...]
    s = c * _sigmoid(c)
    o_ref[...] = _dot(s, w_ref[...], HIGHEST) + b_ref[...]


def _adaln(cond, w, b):
    rows, d = cond.shape
    n = w.shape[1]
    tn = 512
    return pl.pallas_call(
        _adaln_body,
        grid=(n // tn,),
        in_specs=[pl.BlockSpec((rows, d), lambda j: (0, 0)),
                  pl.BlockSpec((d, tn), lambda j: (0, j)),
                  pl.BlockSpec((1, tn), lambda j: (0, j))],
        out_specs=pl.BlockSpec((rows, tn), lambda j: (0, j)),
        out_shape=jax.ShapeDtypeStruct((rows, n), F32),
        compiler_params=_params(("arbitrary",)),
        name="adaln",
    )(cond, w, b.reshape(1, n))


def _rms_mod(x, g, scale, shift):
    y = x * lax.rsqrt(jnp.mean(x * x, axis=-1, keepdims=True) + EPS)
    return (y * g) * (1.0 + scale) + shift


def _inproj_body(x_ref, pe_ref, sc_ref, sh_ref, g_ref, wm_ref, wg_ref, zm_ref, zg_ref, gt_ref):
    x = x_ref[0] + pe_ref[...]
    u = _rms_mod(x, g_ref[...], sc_ref[0], sh_ref[0]).astype(BF16)
    z = _dot(u, wm_ref[...])
    zm_ref[0] = z[:, :GROUP_W].astype(BF16)
    zg_ref[0] = z[:, GROUP_W:].astype(BF16)
    gt_ref[0] = _dot(u, wg_ref[...])


def _inproj(x, pe, scale, shift, g, w_main, w_gate, tm):
    bsz, t, d = x.shape
    per_batch = scale.shape[0] == bsz and bsz > 1
    mod_map = (lambda i, b: (b, 0, 0)) if per_batch else (lambda i, b: (0, 0, 0))
    return pl.pallas_call(
        _inproj_body,
        grid=(t // tm, bsz),
        in_specs=[pl.BlockSpec((1, tm, d), lambda i, b: (b, i, 0)),
                  pl.BlockSpec((tm, d), lambda i, b: (i, 0)),
                  pl.BlockSpec((1, 1, d), mod_map),
                  pl.BlockSpec((1, 1, d), mod_map),
                  pl.BlockSpec((1, d), lambda i, b: (0, 0)),
                  pl.BlockSpec((d, 2 * GROUP_W), lambda i, b: (0, 0)),
                  pl.BlockSpec((d, GATE_W), lambda i, b: (0, 0))],
        out_specs=[pl.BlockSpec((1, tm, GROUP_W), lambda i, b: (b, i, 0)),
                   pl.BlockSpec((1, tm, GROUP_W), lambda i, b: (b, i, 0)),
                   pl.BlockSpec((1, tm, GATE_W), lambda i, b: (b, i, 0))],
        out_shape=[jax.ShapeDtypeStruct((bsz, t, GROUP_W), BF16),
                   jax.ShapeDtypeStruct((bsz, t, GROUP_W), BF16),
                   jax.ShapeDtypeStruct((bsz, t, GATE_W), F32)],
        compiler_params=_params(("arbitrary", "arbitrary")),
        name="inproj",
    )(x, pe, scale, shift, g, w_main, w_gate)


def _mlstm_body(zl_ref, zc_ref, gl_ref, gc_ref, cw_ref, cb_ref, gb_ref, ng_ref, o_ref,
                q_s, k_s, hf_s, cc_s, bc_s, cr_s, st_s, mp_s, c_s, *, t_lat, t_ctx):
    L = ML_CHUNK
    nc_ctx = t_ctx // L
    nc_lat = t_lat // L
    nc = nc_ctx + nc_lat
    scale = DK ** -0.5

    cw = cw_ref[...]
    cb = cb_ref[...]

    def conv_pass(z_ref, n, dst0):
        nb = n // CONV_ROWS
        row = lax.broadcasted_iota(jnp.int32, (CONV_ROWS, QK_W), 0)
        lane = lax.broadcasted_iota(jnp.int32, (CONV_ROWS, QK_W), 1)
        low = (lane % LANES) < DK

        def body(r, carry):
            r0 = pl.multiple_of(r * CONV_ROWS, CONV_ROWS)
            zc = z_ref[0, pl.ds(r0, CONV_ROWS), 0:QK_W].astype(F32)
            p0 = pl.multiple_of(jnp.maximum(r0 - 16, 0), 16)
            prev = z_ref[0, pl.ds(p0, 16), 0:QK_W].astype(F32)[15:16]
            prev = jnp.where(r > 0, prev, 0.0)
            n0 = pl.multiple_of(jnp.minimum(r0 + CONV_ROWS, n - 16), 16)
            nxt = z_ref[0, pl.ds(n0, 16), 0:QK_W].astype(F32)[0:1]
            nxt = jnp.where(r < nb - 1, nxt, 0.0)
            up = jnp.where(row == 0, prev, pltpu.roll(zc, 1, axis=0))
            dn = jnp.where(row == CONV_ROWS - 1, nxt, pltpu.roll(zc, CONV_ROWS - 1, axis=0))
            y = cw[0:1] * up + cw[1:2] * zc + cw[2:3] * dn + cb
            y = y * _sigmoid(y)
            q = jnp.where(low, y * scale, 0.0)
            k = jnp.where(low, pltpu.roll(y, QK_W - DK, axis=1), 0.0)
            d0 = pl.multiple_of(dst0 + r0, CONV_ROWS)
            q_s[pl.ds(d0, CONV_ROWS), :] = q.astype(BF16)
            k_s[pl.ds(d0, CONV_ROWS), :] = k.astype(BF16)
            return carry

        lax.fori_loop(0, nb, body, 0)

    conv_pass(zc_ref, t_ctx, 0)
    conv_pass(zl_ref, t_lat, t_ctx)

    rowi = lax.broadcasted_iota(jnp.int32, (L, L), 0)
    coli = lax.broadcasted_iota(jnp.int32, (L, L), 1)
    tri_lo = (coli <= rowi)
    tri_up = (coli >= rowi)
    tri_lo_f = tri_lo.astype(F32)
    tri_up_f = tri_up.astype(F32)
    lane_g = lax.broadcasted_iota(jnp.int32, (L, GATE_W), 1)
    gb = gb_ref[...]

    def gate_pass(g_ref, nchunks, g0):
        def body(i, carry):
            r0 = pl.multiple_of(i * L, L)
            gates = g_ref[0, pl.ds(r0, L), :] + gb
            lf = pltpu.roll(_log_sigmoid(gates), GATE_W - 2 * N_HEADS, axis=1)
            b = jnp.where(lane_g < N_HEADS, _dot(tri_lo_f, lf, HIGHEST), _dot(tri_up_f, lf, HIGHEST))
            c = gates - b
            cc_s[g0 + i] = c
            bc_s[g0 + i] = b
            cr_s[g0 + i] = c.T[0:8, :]
            st_s[g0 + i, 0:1, :] = jnp.max(c, axis=0, keepdims=True)
            st_s[g0 + i, 1:2, :] = jnp.sum(lf, axis=0, keepdims=True)
            return carry

        lax.fori_loop(0, nchunks, body, 0)

    gate_pass(gc_ref, nc_ctx, 0)
    gate_pass(gl_ref, nc_lat, nc_ctx)

    lane1 = lax.broadcasted_iota(jnp.int32, (1, GATE_W), 1)
    fwd_order = list(range(nc))
    bwd_order = list(range(nc_ctx - 1, -1, -1)) + list(range(nc - 1, nc_ctx - 1, -1))
    mp = {}
    for d, order in enumerate((fwd_order, bwd_order)):
        m = jnp.zeros((1, GATE_W), F32)
        for g in order:
            mp[(d, g)] = m
            m = st_s[g, 1:2, :] + jnp.maximum(m, st_s[g, 0:1, :])
    for g in range(nc):
        mp_s[g, 0:1, :] = jnp.where(lane1 < N_HEADS, mp[(0, g)], mp[(1, g)])

    ones_col = (lax.broadcasted_iota(jnp.int32, (L, DV), 1) == 0).astype(BF16)
    c_s[...] = jnp.zeros(c_s.shape, F32)
    neg_inf = -jnp.inf

    def step(g, z_ref, r_src, r_q, d, with_out, r_out):
        mask = tri_lo if d == 0 else tri_up
        for h in range(N_HEADS):
            j = d * N_HEADS + h
            hs = slice(h * DV, (h + 1) * DV)
            q = q_s[pl.ds(r_q, L), hs]
            k = k_s[pl.ds(r_q, L), hs]
            v = z_ref[0, pl.ds(r_src, L), QK_W + h * DV:QK_W + (h + 1) * DV]
            vext = jnp.concatenate([v, ones_col], axis=1)
            c_col = cc_s[g][:, j:j + 1]
            mprev = mp_s[g][0:1, j:j + 1]
            cmax = st_s[g][0:1, j:j + 1]
            mlast = jnp.maximum(mprev, cmax)
            state = c_s[j]
            if with_out:
                c_row = cr_s[g][j:j + 1, :]
                cm = jnp.where(mask, c_row, neg_inf)
                m_col = jnp.maximum(jnp.max(cm, axis=1, keepdims=True), mprev)
                w = jnp.exp(cm - m_col)
                s = _dot_nt(q, k)
                p = (s * w).astype(BF16)
                nd = _dot(p, vext) + jnp.exp(mprev - m_col) * _dot(q, state.astype(BF16))
                num = nd[:, :DV]
                den = nd[:, DV:DV + 1]
                b_col = bc_s[g][:, j:j + 1]
                hh = num / jnp.maximum(jnp.abs(den), jnp.exp(-(b_col + m_col)))
                if d == 0:
                    hf_s[pl.ds(r_out, L), hs] = hh
                else:
                    tot = hf_s[pl.ds(r_out, L), hs] + hh
                    tot = tot * lax.rsqrt(jnp.mean(tot * tot, axis=-1, keepdims=True) + EPS)
                    og = z_ref[0, pl.ds(r_src, L), QK_W + V_W + h * DV:QK_W + V_W + (h + 1) * DV].astype(F32)
                    o_ref[0, pl.ds(r_out, L), hs] = (tot * ng_ref[:, hs] * _sigmoid(og)).astype(BF16)
            wk = jnp.exp(c_col - mlast)
            vw = (wk * vext.astype(F32)).astype(BF16)
            c_s[j] = jnp.exp(mprev - mlast) * state + _dot_tn(k, vw)

    def ctx_loop(d):
        def body(i, carry):
            ii = i if d == 0 else nc_ctx - 1 - i
            r0 = pl.multiple_of(ii * L, L)
            step(ii, zc_ref, r0, r0, d, False, None)
            return carry
        lax.fori_loop(0, nc_ctx, body, 0)

    def lat_loop(d):
        def body(i, carry):
            ii = i if d == 0 else nc_lat - 1 - i
            r0 = pl.multiple_of(ii * L, L)
            rq = pl.multiple_of(t_ctx + ii * L, L)
            step(nc_ctx + ii, zl_ref, r0, rq, d, True, r0)
            return carry
        lax.fori_loop(0, nc_lat, body, 0)

    for d in (0, 1):
        ctx_loop(d)
        lat_loop(d)


def _mlstm(zm_lat, zm_ctx, gt_lat, gt_ctx, conv_w, conv_b, gate_b, norm_g):
    bsz, t_lat, _ = zm_lat.shape
    t_ctx = zm_ctx.shape[1]
    nc = (t_lat + t_ctx) // ML_CHUNK
    body = functools.partial(_mlstm_body, t_lat=t_lat, t_ctx=t_ctx)
    full = lambda b: (0, 0)
    return pl.pallas_call(
        body,
        grid=(bsz,),
        in_specs=[pl.BlockSpec((1, t_lat, GROUP_W), lambda b: (b, 0, 0)),
                  pl.BlockSpec((1, t_ctx, GROUP_W), lambda b: (b, 0, 0)),
                  pl.BlockSpec((1, t_lat, GATE_W), lambda b: (b, 0, 0)),
                  pl.BlockSpec((1, t_ctx, GATE_W), lambda b: (b, 0, 0)),
                  pl.BlockSpec((ML_CONV, QK_W), full),
                  pl.BlockSpec((1, QK_W), full),
                  pl.BlockSpec((1, GATE_W), full),
                  pl.BlockSpec((1, V_W), full)],
        out_specs=pl.BlockSpec((1, t_lat, V_W), lambda b: (b, 0, 0)),
        out_shape=jax.ShapeDtypeStruct((bsz, t_lat, V_W), BF16),
        scratch_shapes=[pltpu.VMEM((t_ctx + t_lat, QK_W), BF16),
                        pltpu.VMEM((t_ctx + t_lat, QK_W), BF16),
                        pltpu.VMEM((t_lat, V_W), F32),
                        pltpu.VMEM((nc, ML_CHUNK, GATE_W), F32),
                        pltpu.VMEM((nc, ML_CHUNK, GATE_W), F32),
                        pltpu.VMEM((nc, 8, ML_CHUNK), F32),
                        pltpu.VMEM((nc, 8, GATE_W), F32),
                        pltpu.VMEM((nc, 8, GATE_W), F32),
                        pltpu.VMEM((2 * N_HEADS, 2 * DK, 2 * DV), F32)],
        compiler_params=_params(("arbitrary",)),
        name="mlstm",
    )(zm_lat, zm_ctx, gt_lat, gt_ctx, conv_w, conv_b, gate_b, norm_g)


def _gla_body(zl_ref, zc_ref, gl_ref, gc_ref, w2_ref, b2_ref, ng_ref, o_ref, of_s, s_s, *, t_lat, t_ctx):
    L = GLA_CHUNK
    nc_ctx = t_ctx // L
    nc_lat = t_lat // L
    scale = DK ** -0.5
    half = N_HEADS * LANES

    rowi = lax.broadcasted_iota(jnp.int32, (L, L), 0)
    coli = lax.broadcasted_iota(jnp.int32, (L, L), 1)
    tri_lo = (coli <= rowi)
    tri_up = (coli >= rowi)
    tri_f = (tri_lo.astype(F32), tri_up.astype(F32))
    low = lax.broadcasted_iota(jnp.int32, (L, LANES), 1) < DK
    s_s[...] = jnp.zeros(s_s.shape, F32)

    def step(z_ref, g_ref, r0, d, with_out):
        mask = tri_lo if d == 0 else tri_up
        gates = g_ref[0, pl.ds(r0, L), :]
        pre = _dot(gates, w2_ref[:, d * half:(d + 1) * half], HIGHEST) + b2_ref[:, d * half:(d + 1) * half]
        la = _log_sigmoid(pre) * (1.0 / GLA_NORMALIZER)
        b_all = _dot(tri_f[d], la, HIGHEST)
        for h in range(N_HEADS):
            j = d * N_HEADS + h
            hs = slice(h * LANES, (h + 1) * LANES)
            b2 = b_all[:, hs]
            b_end = b2[L - 1:L] if d == 0 else b2[0:1]
            qk = z_ref[0, pl.ds(r0, L), hs].astype(F32)
            qkt = qk * jnp.exp(jnp.where(low, b2, -b2))
            kt = jnp.where(low, pltpu.roll(qkt, DK, axis=1), 0.0)
            v = z_ref[0, pl.ds(r0, L), QK_W + h * DV:QK_W + (h + 1) * DV]
            state = s_s[j]
            if with_out:
                qt = jnp.where(low, qkt * scale, 0.0).astype(BF16)
                a = jnp.where(mask, _dot_nt(qt, kt.astype(BF16)), 0.0)
                o = _dot(a.astype(BF16), v) + _dot_nt(qt, state.astype(BF16))
                if d == 0:
                    of_s[pl.ds(r0, L), hs] = o
                else:
                    tot = of_s[pl.ds(r0, L), hs] + o
                    tot = tot * lax.rsqrt(jnp.mean(tot * tot, axis=-1, keepdims=True) + EPS)
                    gg = z_ref[0, pl.ds(r0, L), QK_W + V_W + h * DV:QK_W + V_W + (h + 1) * DV].astype(F32)
                    o_ref[0, pl.ds(r0, L), hs] = (tot * ng_ref[:, hs] * (gg * _sigmoid(gg))).astype(BF16)
            e_end = jnp.exp(b_end)
            s_s[j] = state * e_end + _dot_tn(v, (kt * e_end).astype(BF16))

    def loop(z_ref, g_ref, n, d, with_out):
        def body(i, carry):
            ii = i if d == 0 else n - 1 - i
            step(z_ref, g_ref, pl.multiple_of(ii * L, L), d, with_out)
            return carry
        lax.fori_loop(0, n, body, 0)

    for d in (0, 1):
        loop(zc_ref, gc_ref, nc_ctx, d, False)
        loop(zl_ref, gl_ref, nc_lat, d, True)


def _gla(zg_lat, zg_ctx, gt_lat, gt_ctx, w2_ext, b2_ext, norm_g):
    bsz, t_lat, _ = zg_lat.shape
    t_ctx = zg_ctx.shape[1]
    body = functools.partial(_gla_body, t_lat=t_lat, t_ctx=t_ctx)
    full = lambda b: (0, 0)
    return pl.pallas_call(
        body,
        grid=(bsz,),
        in_specs=[pl.BlockSpec((1, t_lat, GROUP_W), lambda b: (b, 0, 0)),
                  pl.BlockSpec((1, t_ctx, GROUP_W), lambda b: (b, 0, 0)),
                  pl.BlockSpec((1, t_lat, GATE_W), lambda b: (b, 0, 0)),
                  pl.BlockSpec((1, t_ctx, GATE_W), lambda b: (b, 0, 0)),
                  pl.BlockSpec((GATE_W, 2 * N_HEADS * LANES), full),
                  pl.BlockSpec((1, 2 * N_HEADS * LANES), full),
                  pl.BlockSpec((1, V_W), full)],
        out_specs=pl.BlockSpec((1, t_lat, V_W), lambda b: (b, 0, 0)),
        out_shape=jax.ShapeDtypeStruct((bsz, t_lat, V_W), BF16),
        scratch_shapes=[pltpu.VMEM((t_lat, V_W), F32),
                        pltpu.VMEM((2 * N_HEADS, DV, LANES), F32)],
        compiler_params=_params(("arbitrary",)),
        name="gla",
    )(zg_lat, zg_ctx, gt_lat, gt_ctx, w2_ext, b2_ext, norm_g)


def _outproj_body(x_ref, pe_ref, a_ref, g_ref, wo_ref, g1_ref, sc_ref, sh_ref, n2_ref, rw_ref, rb_ref,
                  x1_ref, u2_ref, gates_ref):
    x = x_ref[0] + pe_ref[...]
    y = _dot(a_ref[0], wo_ref[0:V_W, :]) + _dot(g_ref[0], wo_ref[V_W:2 * V_W, :])
    x1 = x + g1_ref[0] * y
    x1_ref[0] = x1
    u2 = _rms_mod(x1, n2_ref[...], sc_ref[0], sh_ref[0])
    u2_ref[0] = u2.astype(BF16)
    logits = _dot(u2, rw_ref[...], HIGHEST) + rb_ref[...]
    lane = lax.broadcasted_iota(jnp.int32, logits.shape, 1).astype(F32)
    picked = []
    vals = []
    for _ in range(TOP_K):
        m = jnp.max(logits, axis=1, keepdims=True)
        idx = jnp.min(jnp.where(logits == m, lane, float(LANES)), axis=1, keepdims=True)
        onehot = lane == idx
        picked.append(onehot)
        vals.append(m)
        logits = jnp.where(onehot, -jnp.inf, logits)
    exps = [jnp.exp(v - vals[0]) for v in vals]
    denom = exps[0]
    for e in exps[1:]:
        denom = denom + e
    gates = jnp.zeros(logits.shape, F32)
    for onehot, e in zip(picked, exps):
        gates = gates + jnp.where(onehot, e / denom, 0.0)
    gates_ref[0] = gates


def _outproj(x, pe, a, g, w_out, gate1, scale2, shift2, norm2_g, router_w, router_b, tm):
    bsz, t, d = x.shape
    tok = lambda i, b: (b, i, 0)
    mod = lambda i, b: (b, 0, 0)
    full = lambda i, b: (0, 0)
    return pl.pallas_call(
        _outproj_body,
        grid=(t // tm, bsz),
        in_specs=[pl.BlockSpec((1, tm, d), tok),
                  pl.BlockSpec((tm, d), lambda i, b: (i, 0)),
                  pl.BlockSpec((1, tm, V_W), tok),
                  pl.BlockSpec((1, tm, V_W), tok),
                  pl.BlockSpec((2 * V_W, d), full),
                  pl.BlockSpec((1, 1, d), mod),
                  pl.BlockSpec((1, 1, d), mod),
                  pl.BlockSpec((1, 1, d), mod),
                  pl.BlockSpec((1, d), full),
                  pl.BlockSpec((d, LANES), full),
                  pl.BlockSpec((1, LANES), full)],
        out_specs=[pl.BlockSpec((1, tm, d), tok),
                   pl.BlockSpec((1, tm, d), tok),
                   pl.BlockSpec((1, tm, LANES), tok)],
        out_shape=[jax.ShapeDtypeStruct((bsz, t, d), F32),
                   jax.ShapeDtypeStruct((bsz, t, d), BF16),
                   jax.ShapeDtypeStruct((bsz, t, LANES), F32)],
        compiler_params=_params(("arbitrary", "arbitrary")),
        name="outproj_router",
    )(x, pe, a, g, w_out, gate1, scale2, shift2, norm2_g, router_w, router_b)


def _moe_body(u_ref, gates_ref, x1_ref, wgu_ref, bgu_ref, wd_ref, bd_ref, g5_ref, fg_ref, o_ref, acc_s,
              *, n_experts, d_expert):
    e = pl.program_id(1)

    @pl.when(e == 0)
    def _():
        acc_s[...] = jnp.zeros(acc_s.shape, F32)

    gu = _dot(u_ref[0], wgu_ref[0]) + bgu_ref[0]
    gate = jnp.minimum(gu[:, :d_expert], SWIGLU_LIMIT)
    up = jnp.clip(gu[:, d_expert:], -SWIGLU_LIMIT, SWIGLU_LIMIT)
    act = ((up + 1.0) * (gate * _sigmoid(SWIGLU_ALPHA * gate))).astype(BF16)
    y = _dot(act, wd_ref[0]) + bd_ref[0]
    gates = gates_ref[0]
    lane = lax.broadcasted_iota(jnp.int32, gates.shape, 1)
    ge = jnp.sum(jnp.where(lane == e, gates, 0.0), axis=1, keepdims=True)
    acc_s[...] += ge * y

    @pl.when(e == n_experts - 1)
    def _():
        xo = x1_ref[0] + g5_ref[0] * acc_s[...]
        o_ref[0] = xo * lax.rsqrt(jnp.mean(xo * xo, axis=-1, keepdims=True) + EPS) * fg_ref[...]


def _moe(u2, gates, x1, w_gu, b_gu, w_down, b_down, gate5, final_g, tm):
    bsz, t, d = x1.shape
    n_experts, _, two_de = w_gu.shape
    d_expert = two_de // 2
    body = functools.partial(_moe_body, n_experts=n_experts, d_expert=d_expert)
    nt = t // tm
    tok = lambda i, e: (i // nt, i % nt, 0)
    return pl.pallas_call(
        body,
        grid=(bsz * nt, n_experts),
        in_specs=[pl.BlockSpec((1, tm, d), tok),
                  pl.BlockSpec((1, tm, LANES), tok),
                  pl.BlockSpec((1, tm, d), tok),
                  pl.BlockSpec((1, d, two_de), lambda i, e: (e, 0, 0)),
                  pl.BlockSpec((1, 1, two_de), lambda i, e: (e, 0, 0)),
                  pl.BlockSpec((1, d_expert, d), lambda i, e: (e, 0, 0)),
                  pl.BlockSpec((1, 1, d), lambda i, e: (e, 0, 0)),
                  pl.BlockSpec((1, 1, d), lambda i, e: (i // nt, 0, 0)),
                  pl.BlockSpec((1, d), lambda i, e: (0, 0))],
        out_specs=pl.BlockSpec((1, tm, d), tok),
        out_shape=jax.ShapeDtypeStruct((bsz, t, d), F32),
        scratch_shapes=[pltpu.VMEM((tm, d), F32)],
        compiler_params=_params(("arbitrary", "arbitrary")),
        name="moe_dense",
    )(u2, gates, x1, w_gu, b_gu, w_down, b_down, gate5, final_g)


def _grid_sincos(rows, d):
    nf = d // 4
    omega = 1.0 / (POS_BASE ** (jnp.arange(nf, dtype=F32) / nf))
    r = jnp.broadcast_to(jnp.arange(rows, dtype=F32)[:, None, None] * omega, (rows, GRID_W, nf))
    cl = jnp.broadcast_to(jnp.arange(GRID_W, dtype=F32)[None, :, None] * omega, (rows, GRID_W, nf))
    pe = jnp.concatenate([jnp.sin(r), jnp.cos(r), jnp.sin(cl), jnp.cos(cl)], axis=-1)
    return pe.reshape(rows * GRID_W, d)


def _pack_qk(wq, wk):
    lead = wq.shape[:-1]
    q = wq.reshape(lead + (N_HEADS, DK))
    k = wk.reshape(lead + (N_HEADS, DK))
    return jnp.concatenate([q, k], axis=-1).reshape(lead + (QK_W,))


def _row_tile(t):
    for tm in (512, 256, 128):
        if t % tm == 0:
            return tm
    raise ValueError(f"sequence length {t} must be a multiple of 128")


def kernel(x, c, ctx, c_ctx, ada_w, ada_b, norm1_g, w_in, ml_conv_w, ml_conv_b, ml_gate_b, ml_norm_g,
           gla_gate_w2, gla_gate_b, gla_norm_g, w_out, norm2_g, router_w, router_b, moe_w_gu, moe_b_gu,
           moe_w_down, moe_b_down, final_norm_g):
    bsz, t_lat, d = x.shape
    t_ctx = ctx.shape[1]
    assert ada_w.shape[0] == 1, "single-layer block"
    assert t_lat % ML_CHUNK == 0 and t_ctx % ML_CHUNK == 0 and t_lat % GRID_W == 0
    n_experts = router_w.shape[-1]
    assert n_experts <= LANES

    ml_qk, ml_v = N_HEADS * DK, N_HEADS * DV
    sizes = (ml_qk, ml_qk, ml_v, ml_v, 4 * N_HEADS, ml_qk, ml_qk, ml_v, ml_v, 2 * GLA_RANK)
    offs = [int(o) for o in np.cumsum(sizes)[:-1]]
    w_mq, w_mk, w_mv, w_mo, w_mg, w_gq, w_gk, w_gv, w_gg, w_glr = jnp.split(w_in[0], offs, axis=-1)
    w_main = jnp.concatenate([_pack_qk(w_mq, w_mk), w_mv, w_mo, _pack_qk(w_gq, w_gk), w_gv, w_gg],
                             axis=-1).astype(BF16)
    gate_perm = np.concatenate([np.arange(N_HEADS) + g * N_HEADS for g in (0, 2, 1, 3)])
    w_gate = jnp.concatenate([w_mg[:, gate_perm], w_glr,
                              jnp.zeros((d, GATE_W - 4 * N_HEADS - 2 * GLA_RANK), F32)], axis=-1).astype(BF16)
    gate_b = jnp.concatenate([ml_gate_b[0].reshape(-1)[gate_perm],
                              jnp.zeros((GATE_W - 4 * N_HEADS,), F32)]).reshape(1, GATE_W)
    conv_w = _pack_qk(ml_conv_w[0][:, :ml_qk], ml_conv_w[0][:, ml_qk:])
    conv_b = _pack_qk(ml_conv_b[0][:ml_qk], ml_conv_b[0][ml_qk:]).reshape(1, QK_W)
    w2 = gla_gate_w2[0].reshape(2, GLA_RANK, N_HEADS, DK)
    w2 = jnp.concatenate([w2, w2], axis=-1).reshape(2, GLA_RANK, N_HEADS * LANES)
    w2_ext = jnp.zeros((GATE_W, 2 * N_HEADS * LANES), F32)
    for dd in range(2):
        r0 = 4 * N_HEADS + dd * GLA_RANK
        w2_ext = w2_ext.at[r0:r0 + GLA_RANK, dd * N_HEADS * LANES:(dd + 1) * N_HEADS * LANES].set(w2[dd])
    b2 = gla_gate_b[0].reshape(2, N_HEADS, DK)
    b2_ext = jnp.concatenate([b2, b2], axis=-1).reshape(1, 2 * N_HEADS * LANES)
    router_w_p = jnp.concatenate([router_w[0], jnp.zeros((d, LANES - n_experts), F32)], axis=-1)
    router_b_p = jnp.concatenate([router_b[0], jnp.full((LANES - n_experts,), -jnp.inf, F32)]).reshape(1, LANES)

    cond = jnp.concatenate([c, c_ctx[None, :], jnp.zeros(((-bsz - 1) % 8, d), F32)], axis=0)
    mod = _adaln(cond, ada_w[0], ada_b[0]).reshape(cond.shape[0], N_MOD, 1, d)
    m_lat = mod[:bsz]
    m_ctx = mod[bsz:bsz + 1]

    pe = _grid_sincos(t_lat // GRID_W, d)
    g1 = norm1_g[0].reshape(1, d)
    tm = _row_tile(t_lat)
    zm_lat, zg_lat, gt_lat = _inproj(x, pe, m_lat[:, 1], m_lat[:, 0], g1, w_main, w_gate, tm)
    zm_ctx, zg_ctx, gt_ctx = _inproj(ctx, jnp.zeros((t_ctx, d), F32), m_ctx[:, 1], m_ctx[:, 0], g1,
                                     w_main, w_gate, _row_tile(t_ctx))

    a = _mlstm(zm_lat, zm_ctx, gt_lat, gt_ctx, conv_w, conv_b, gate_b, ml_norm_g[0].reshape(1, V_W))
    g = _gla(zg_lat, zg_ctx, gt_lat, gt_ctx, w2_ext, b2_ext, gla_norm_g[0].reshape(1, V_W))

    x1, u2, gates = _outproj(x, pe, a, g, w_out[0].astype(BF16), m_lat[:, 2], m_lat[:, 4], m_lat[:, 3],
                             norm2_g[0].reshape(1, d), router_w_p, router_b_p, tm)

    return _moe(u2, gates, x1, moe_w_gu[0].astype(BF16), moe_b_gu[0][:, None, :], moe_w_down[0].astype(BF16),
                moe_b_down[0][:, None, :], m_lat[:, 5], final_norm_g.reshape(1, d), tm)
```

```python
import functools

import numpy as np
import jax
import jax.numpy as jnp
from jax import lax
from jax.experimental import pallas as pl
from jax.experimental.pallas import tpu as pltpu

F32 = jnp.float32
BF16 = jnp.bfloat16
HIGHEST = lax.Precision.HIGHEST

EPS = 1e-6
GRID_W = 64
POS_BASE = 10000.0
N_MOD = 6
N_HEADS = 4
DK = 64
DV = 128
QK_W = N_HEADS * 2 * DK
V_W = N_HEADS * DV
GROUP_W = QK_W + 2 * V_W
GATE_W = 128
ML_CONV = 3
GLA_RANK = 16
GLA_NORMALIZER = 16.0
TOP_K = 4
SWIGLU_LIMIT = 7.0
SWIGLU_ALPHA = 1.702
LANES = 128
ML_CHUNK = 128
GLA_CHUNK = 64
CONV_ROWS = 128
EXPERT_TILE = 512
VMEM_LIMIT = 56 * 1024 * 1024


def _params(sem):
    return pltpu.CompilerParams(dimension_semantics=sem, vmem_limit_bytes=VMEM_LIMIT)


def _sigmoid(x):
    return 1.0 / (1.0 + jnp.exp(-x))


def _log_sigmoid(x):
    return jnp.minimum(x, 0.0) - jnp.log1p(jnp.exp(-jnp.abs(x)))


def _dot(a, b, precision=None):
    return jnp.dot(a, b, preferred_element_type=F32, precision=precision)


def _dot_nt(a, b):
    return lax.dot_general(a, b, (((1,), (1,)), ((), ())), preferred_element_type=F32)


def _dot_tn(a, b):
    return lax.dot_general(a, b, (((0,), (0,)), ((), ())), preferred_element_type=F32)


def _adaln_body(c_ref, w_ref, b_ref, o_ref):
    c = c_ref[...]
    s = c * _sigmoid(c)
    o_ref[...] = _dot(s, w_ref[...], HIGHEST) + b_ref[...]


def _adaln(cond, w, b):
    rows, d = cond.shape
    n = w.shape[1]
    tn = 512
    return pl.pallas_call(
        _adaln_body,
        grid=(n // tn,),
        in_specs=[pl.BlockSpec((rows, d), lambda j: (0, 0)),
                  pl.BlockSpec((d, tn), lambda j: (0, j)),
                  pl.BlockSpec((1, tn), lambda j: (0, j))],
        out_specs=pl.BlockSpec((rows, tn), lambda j: (0, j)),
        out_shape=jax.ShapeDtypeStruct((rows, n), F32),
        compiler_params=_params(("arbitrary",)),
        name="adaln",
    )(cond, w, b.reshape(1, n))


def _rms_mod(x, g, scale, shift):
    y = x * lax.rsqrt(jnp.mean(x * x, axis=-1, keepdims=True) + EPS)
    return (y * g) * (1.0 + scale) + shift


def _inproj_body(x_ref, pe_ref, sc_ref, sh_ref, g_ref, wm_ref, wg_ref, zm_ref, zg_ref, gt_ref):
    x = x_ref[0] + pe_ref[...]
    u = _rms_mod(x, g_ref[...], sc_ref[0], sh_ref[0]).astype(BF16)
    z = _dot(u, wm_ref[...])
    zm_ref[0] = z[:, :GROUP_W].astype(BF16)
    zg_ref[0] = z[:, GROUP_W:].astype(BF16)
    gt_ref[0] = _dot(u, wg_ref[...])


def _inproj(x, pe, scale, shift, g, w_main, w_gate, tm):
    bsz, t, d = x.shape
    per_batch = scale.shape[0] == bsz and bsz > 1
    mod_map = (lambda i, b: (b, 0, 0)) if per_batch else (lambda i, b: (0, 0, 0))
    return pl.pallas_call(
        _inproj_body,
        grid=(t // tm, bsz),
        in_specs=[pl.BlockSpec((1, tm, d), lambda i, b: (b, i, 0)),
                  pl.BlockSpec((tm, d), lambda i, b: (i, 0)),
                  pl.BlockSpec((1, 1, d), mod_map),
                  pl.BlockSpec((1, 1, d), mod_map),
                  pl.BlockSpec((1, d), lambda i, b: (0, 0)),
                  pl.BlockSpec((d, 2 * GROUP_W), lambda i, b: (0, 0)),
                  pl.BlockSpec((d, GATE_W), lambda i, b: (0, 0))],
        out_specs=[pl.BlockSpec((1, tm, GROUP_W), lambda i, b: (b, i, 0)),
                   pl.BlockSpec((1, tm, GROUP_W), lambda i, b: (b, i, 0)),
                   pl.BlockSpec((1, tm, GATE_W), lambda i, b: (b, i, 0))],
        out_shape=[jax.ShapeDtypeStruct((bsz, t, GROUP_W), BF16),
                   jax.ShapeDtypeStruct((bsz, t, GROUP_W), BF16),
                   jax.ShapeDtypeStruct((bsz, t, GATE_W), F32)],
        compiler_params=_params(("arbitrary", "arbitrary")),
        name="inproj",
    )(x, pe, scale, shift, g, w_main, w_gate)


def _mlstm_body(zl_ref, zc_ref, gl_ref, gc_ref, cw_ref, cb_ref, gb_ref, ng_ref, o_ref,
                q_s, k_s, hf_s, cc_s, bc_s, cr_s, st_s, mp_s, c_s, *, t_lat, t_ctx):
    L = ML_CHUNK
    nc_ctx = t_ctx // L
    nc_lat = t_lat // L
    nc = nc_ctx + nc_lat
    scale = DK ** -0.5

    cw = cw_ref[...]
    cb = cb_ref[...]

    def conv_pass(z_ref, n, dst0):
        nb = n // CONV_ROWS
        row = lax.broadcasted_iota(jnp.int32, (CONV_ROWS, QK_W), 0)
        lane = lax.broadcasted_iota(jnp.int32, (CONV_ROWS, QK_W), 1)
        low = (lane % LANES) < DK

        def body(r, carry):
            r0 = pl.multiple_of(r * CONV_ROWS, CONV_ROWS)
            zc = z_ref[0, pl.ds(r0, CONV_ROWS), 0:QK_W].astype(F32)
            p0 = pl.multiple_of(jnp.maximum(r0 - 16, 0), 16)
            prev = z_ref[0, pl.ds(p0, 16), 0:QK_W].astype(F32)[15:16]
            prev = jnp.where(r > 0, prev, 0.0)
            n0 = pl.multiple_of(jnp.minimum(r0 + CONV_ROWS, n - 16), 16)
            nxt = z_ref[0, pl.ds(n0, 16), 0:QK_W].astype(F32)[0:1]
            nxt = jnp.where(r < nb - 1, nxt, 0.0)
            up = jnp.where(row == 0, prev, pltpu.roll(zc, 1, axis=0))
            dn = jnp.where(row == CONV_ROWS - 1, nxt, pltpu.roll(zc, CONV_ROWS - 1, axis=0))
            y = cw[0:1] * up + cw[1:2] * zc + cw[2:3] * dn + cb
            y = y * _sigmoid(y)
            q = jnp.where(low, y * scale, 0.0)
            k = jnp.where(low, pltpu.roll(y, QK_W - DK, axis=1), 0.0)
            d0 = pl.multiple_of(dst0 + r0, CONV_ROWS)
            q_s[pl.ds(d0, CONV_ROWS), :] = q.astype(BF16)
            k_s[pl.ds(d0, CONV_ROWS), :] = k.astype(BF16)
            return carry

        lax.fori_loop(0, nb, body, 0)

    conv_pass(zc_ref, t_ctx, 0)
    conv_pass(zl_ref, t_lat, t_ctx)

    rowi = lax.broadcasted_iota(jnp.int32, (L, L), 0)
    coli = lax.broadcasted_iota(jnp.int32, (L, L), 1)
    tri_lo = (coli <= rowi)
    tri_up = (coli >= rowi)
    tri_lo_f = tri_lo.astype(F32)
    tri_up_f = tri_up.astype(F32)
    lane_g = lax.broadcasted_iota(jnp.int32, (L, GATE_W), 1)
    gb = gb_ref[...]

    def gate_pass(g_ref, nchunks, g0):
        def body(i, carry):
            r0 = pl.multiple_of(i * L, L)
            gates = g_ref[0, pl.ds(r0, L), :] + gb
            lf = pltpu.roll(_log_sigmoid(gates), GATE_W - 2 * N_HEADS, axis=1)
            b = jnp.where(lane_g < N_HEADS, _dot(tri_lo_f, lf, HIGHEST), _dot(tri_up_f, lf, HIGHEST))
            c = gates - b
            cc_s[g0 + i] = c
            bc_s[g0 + i] = b
            cr_s[g0 + i] = c.T[0:8, :]
            st_s[g0 + i, 0:1, :] = jnp.max(c, axis=0, keepdims=True)
            st_s[g0 + i, 1:2, :] = jnp.sum(lf, axis=0, keepdims=True)
            return carry

        lax.fori_loop(0, nchunks, body, 0)

    gate_pass(gc_ref, nc_ctx, 0)
    gate_pass(gl_ref, nc_lat, nc_ctx)

    lane1 = lax.broadcasted_iota(jnp.int32, (1, GATE_W), 1)
    fwd_order = list(range(nc))
    bwd_order = list(range(nc_ctx - 1, -1, -1)) + list(range(nc - 1, nc_ctx - 1, -1))
    mp = {}
    for d, order in enumerate((fwd_order, bwd_order)):
        m = jnp.zeros((1, GATE_W), F32)
        for g in order:
            mp[(d, g)] = m
            m = st_s[g, 1:2, :] + jnp.maximum(m, st_s[g, 0:1, :])
    for g in range(nc):
        mp_s[g, 0:1, :] = jnp.where(lane1 < N_HEADS, mp[(0, g)], mp[(1, g)])

    ones_col = (lax.broadcasted_iota(jnp.int32, (L, DV), 1) == 0).astype(BF16)
    c_s[...] = jnp.zeros(c_s.shape, F32)
    neg_inf = -jnp.inf

    def step(g, z_ref, r_src, r_q, d, with_out, r_out):
        mask = tri_lo if d == 0 else tri_up
        for h in range(N_HEADS):
            j = d * N_HEADS + h
            hs = slice(h * DV, (h + 1) * DV)
            q = q_s[pl.ds(r_q, L), hs]
            k = k_s[pl.ds(r_q, L), hs]
            v = z_ref[0, pl.ds(r_src, L), QK_W + h * DV:QK_W + (h + 1) * DV]
            vext = jnp.concatenate([v, ones_col], axis=1)
            c_col = cc_s[g][:, j:j + 1]
            mprev = mp_s[g][0:1, j:j + 1]
            cmax = st_s[g][0:1, j:j + 1]
            mlast = jnp.maximum(mprev, cmax)
            state = c_s[j]
            if with_out:
                c_row = cr_s[g][j:j + 1, :]
                cm = jnp.where(mask, c_row, neg_inf)
                m_col = jnp.maximum(jnp.max(cm, axis=1, keepdims=True), mprev)
                w = jnp.exp(cm - m_col)
                s = _dot_nt(q, k)
                p = (s * w).astype(BF16)
                nd = _dot(p, vext) + jnp.exp(mprev - m_col) * _dot(q, state.astype(BF16))
                num = nd[:, :DV]
                den = nd[:, DV:DV + 1]
                b_col = bc_s[g][:, j:j + 1]
                hh = num / jnp.maximum(jnp.abs(den), jnp.exp(-(b_col + m_col)))
                if d == 0:
                    hf_s[pl.ds(r_out, L), hs] = hh
                else:
                    tot = hf_s[pl.ds(r_out, L), hs] + hh
                    tot = tot * lax.rsqrt(jnp.mean(tot * tot, axis=-1, keepdims=True) + EPS)
                    og = z_ref[0, pl.ds(r_src, L), QK_W + V_W + h * DV:QK_W + V_W + (h + 1) * DV].astype(F32)
                    o_ref[0, pl.ds(r_out, L), hs] = (tot * ng_ref[:, hs] * _sigmoid(og)).astype(BF16)
            wk = jnp.exp(c_col - mlast)
            vw = (wk * vext.astype(F32)).astype(BF16)
            c_s[j] = jnp.exp(mprev - mlast) * state + _dot_tn(k, vw)

    def ctx_loop(d):
        def body(i, carry):
            ii = i if d == 0 else nc_ctx - 1 - i
            r0 = pl.multiple_of(ii * L, L)
            step(ii, zc_ref, r0, r0, d, False, None)
            return carry
        lax.fori_loop(0, nc_ctx, body, 0)

    def lat_loop(d):
        def body(i, carry):
            ii = i if d == 0 else nc_lat - 1 - i
            r0 = pl.multiple_of(ii * L, L)
            rq = pl.multiple_of(t_ctx + ii * L, L)
            step(nc_ctx + ii, zl_ref, r0, rq, d, True, r0)
            return carry
        lax.fori_loop(0, nc_lat, body, 0)

    for d in (0, 1):
        ctx_loop(d)
        lat_loop(d)


def _mlstm(zm_lat, zm_ctx, gt_lat, gt_ctx, conv_w, conv_b, gate_b, norm_g):
    bsz, t_lat, _ = zm_lat.shape
    t_ctx = zm_ctx.shape[1]
    nc = (t_lat + t_ctx) // ML_CHUNK
    body = functools.partial(_mlstm_body, t_lat=t_lat, t_ctx=t_ctx)
    full = lambda b: (0, 0)
    return pl.pallas_call(
        body,
        grid=(bsz,),
        in_specs=[pl.BlockSpec((1, t_lat, GROUP_W), lambda b: (b, 0, 0)),
                  pl.BlockSpec((1, t_ctx, GROUP_W), lambda b: (b, 0, 0)),
                  pl.BlockSpec((1, t_lat, GATE_W), lambda b: (b, 0, 0)),
                  pl.BlockSpec((1, t_ctx, GATE_W), lambda b: (b, 0, 0)),
                  pl.BlockSpec((ML_CONV, QK_W), full),
                  pl.BlockSpec((1, QK_W), full),
                  pl.BlockSpec((1, GATE_W), full),
                  pl.BlockSpec((1, V_W), full)],
        out_specs=pl.BlockSpec((1, t_lat, V_W), lambda b: (b, 0, 0)),
        out_shape=jax.ShapeDtypeStruct((bsz, t_lat, V_W), BF16),
        scratch_shapes=[pltpu.VMEM((t_ctx + t_lat, QK_W), BF16),
                        pltpu.VMEM((t_ctx + t_lat, QK_W), BF16),
                        pltpu.VMEM((t_lat, V_W), F32),
                        pltpu.VMEM((nc, ML_CHUNK, GATE_W), F32),
                        pltpu.VMEM((nc, ML_CHUNK, GATE_W), F32),
                        pltpu.VMEM((nc, 8, ML_CHUNK), F32),
                        pltpu.VMEM((nc, 8, GATE_W), F32),
                        pltpu.VMEM((nc, 8, GATE_W), F32),
                        pltpu.VMEM((2 * N_HEADS, 2 * DK, 2 * DV), F32)],
        compiler_params=_params(("arbitrary",)),
        name="mlstm",
    )(zm_lat, zm_ctx, gt_lat, gt_ctx, conv_w, conv_b, gate_b, norm_g)


def _gla_body(zl_ref, zc_ref, gl_ref, gc_ref, w2_ref, b2_ref, ng_ref, o_ref, of_s, s_s, *, t_lat, t_ctx):
    L = GLA_CHUNK
    nc_ctx = t_ctx // L
    nc_lat = t_lat // L
    scale = DK ** -0.5
    half = N_HEADS * LANES

    rowi = lax.broadcasted_iota(jnp.int32, (L, L), 0)
    coli = lax.broadcasted_iota(jnp.int32, (L, L), 1)
    tri_lo = (coli <= rowi)
    tri_up = (coli >= rowi)
    tri_f = (tri_lo.astype(F32), tri_up.astype(F32))
    low = lax.broadcasted_iota(jnp.int32, (L, LANES), 1) < DK
    s_s[...] = jnp.zeros(s_s.shape, F32)

    def step(z_ref, g_ref, r0, d, with_out):
        mask = tri_lo if d == 0 else tri_up
        gates = g_ref[0, pl.ds(r0, L), :]
        pre = _dot(gates, w2_ref[:, d * half:(d + 1) * half], HIGHEST) + b2_ref[:, d * half:(d + 1) * half]
        la = _log_sigmoid(pre) * (1.0 / GLA_NORMALIZER)
        b_all = _dot(tri_f[d], la, HIGHEST)
        for h in range(N_HEADS):
            j = d * N_HEADS + h
            hs = slice(h * LANES, (h + 1) * LANES)
            b2 = b_all[:, hs]
            b_end = b2[L - 1:L] if d == 0 else b2[0:1]
            qk = z_ref[0, pl.ds(r0, L), hs].astype(F32)
            qkt = qk * jnp.exp(jnp.where(low, b2, -b2))
            kt = jnp.where(low, pltpu.roll(qkt, DK, axis=1), 0.0)
            v = z_ref[0, pl.ds(r0, L), QK_W + h * DV:QK_W + (h + 1) * DV]
            state = s_s[j]
            if with_out:
                qt = jnp.where(low, qkt * scale, 0.0).astype(BF16)
                a = jnp.where(mask, _dot_nt(qt, kt.astype(BF16)), 0.0)
                o = _dot(a.astype(BF16), v) + _dot_nt(qt, state.astype(BF16))
                if d == 0:
                    of_s[pl.ds(r0, L), hs] = o
                else:
                    tot = of_s[pl.ds(r0, L), hs] + o
                    tot = tot * lax.rsqrt(jnp.mean(tot * tot, axis=-1, keepdims=True) + EPS)
                    gg = z_ref[0, pl.ds(r0, L), QK_W + V_W + h * DV:QK_W + V_W + (h + 1) * DV].astype(F32)
                    o_ref[0, pl.ds(r0, L), hs] = (tot * ng_ref[:, hs] * (gg * _sigmoid(gg))).astype(BF16)
            e_end = jnp.exp(b_end)
            s_s[j] = state * e_end + _dot_tn(v, (kt * e_end).astype(BF16))

    def loop(z_ref, g_ref, n, d, with_out):
        def body(i, carry):
            ii = i if d == 0 else n - 1 - i
            step(z_ref, g_ref, pl.multiple_of(ii * L, L), d, with_out)
            return carry
        lax.fori_loop(0, n, body, 0)

    for d in (0, 1):
        loop(zc_ref, gc_ref, nc_ctx, d, False)
        loop(zl_ref, gl_ref, nc_lat, d, True)


def _gla(zg_lat, zg_ctx, gt_lat, gt_ctx, w2_ext, b2_ext, norm_g):
    bsz, t_lat, _ = zg_lat.shape
    t_ctx = zg_ctx.shape[1]
    body = functools.partial(_gla_body, t_lat=t_lat, t_ctx=t_ctx)
    full = lambda b: (0, 0)
    return pl.pallas_call(
        body,
        grid=(bsz,),
        in_specs=[pl.BlockSpec((1, t_lat, GROUP_W), lambda b: (b, 0, 0)),
                  pl.BlockSpec((1, t_ctx, GROUP_W), lambda b: (b, 0, 0)),
                  pl.BlockSpec((1, t_lat, GATE_W), lambda b: (b, 0, 0)),
                  pl.BlockSpec((1, t_ctx, GATE_W), lambda b: (b, 0, 0)),
                  pl.BlockSpec((GATE_W, 2 * N_HEADS * LANES), full),
                  pl.BlockSpec((1, 2 * N_HEADS * LANES), full),
                  pl.BlockSpec((1, V_W), full)],
        out_specs=pl.BlockSpec((1, t_lat, V_W), lambda b: (b, 0, 0)),
        out_shape=jax.ShapeDtypeStruct((bsz, t_lat, V_W), BF16),
        scratch_shapes=[pltpu.VMEM((t_lat, V_W), F32),
                        pltpu.VMEM((2 * N_HEADS, DV, LANES), F32)],
        compiler_params=_params(("arbitrary",)),
        name="gla",
    )(zg_lat, zg_ctx, gt_lat, gt_ctx, w2_ext, b2_ext, norm_g)


def _outproj_body(x_ref, pe_ref, a_ref, g_ref, wo_ref, g1_ref, sc_ref, sh_ref, n2_ref, rw_ref, rb_ref,
                  x1_ref, u2_ref, route_ref):
    x = x_ref[0] + pe_ref[...]
    y = _dot(a_ref[0], wo_ref[0:V_W, :]) + _dot(g_ref[0], wo_ref[V_W:2 * V_W, :])
    x1 = x + g1_ref[0] * y
    x1_ref[0] = x1
    u2 = _rms_mod(x1, n2_ref[...], sc_ref[0], sh_ref[0])
    u2_ref[0] = u2
    logits = _dot(u2, rw_ref[...], HIGHEST) + rb_ref[...]
    lane = lax.broadcasted_iota(jnp.int32, logits.shape, 1).astype(F32)
    ids = []
    vals = []
    for _ in range(TOP_K):
        m = jnp.max(logits, axis=1, keepdims=True)
        idx = jnp.min(jnp.where(logits == m, lane, float(LANES)), axis=1, keepdims=True)
        ids.append(idx)
        vals.append(m)
        logits = jnp.where(lane == idx, -jnp.inf, logits)
    exps = [jnp.exp(v - vals[0]) for v in vals]
    denom = exps[0]
    for e in exps[1:]:
        denom = denom + e
    route = jnp.zeros(logits.shape, F32)
    for k in range(TOP_K):
        route = jnp.where(lane == float(k), ids[k], route)
        route = jnp.where(lane == float(TOP_K + k), exps[k] / denom, route)
    route_ref[0] = route


def _outproj(x, pe, a, g, w_out, gate1, scale2, shift2, norm2_g, router_w, router_b, tm):
    bsz, t, d = x.shape
    tok = lambda i, b: (b, i, 0)
    mod = lambda i, b: (b, 0, 0)
    full = lambda i, b: (0, 0)
    return pl.pallas_call(
        _outproj_body,
        grid=(t // tm, bsz),
        in_specs=[pl.BlockSpec((1, tm, d), tok),
                  pl.BlockSpec((tm, d), lambda i, b: (i, 0)),
                  pl.BlockSpec((1, tm, V_W), tok),
                  pl.BlockSpec((1, tm, V_W), tok),
                  pl.BlockSpec((2 * V_W, d), full),
                  pl.BlockSpec((1, 1, d), mod),
                  pl.BlockSpec((1, 1, d), mod),
                  pl.BlockSpec((1, 1, d), mod),
                  pl.BlockSpec((1, d), full),
                  pl.BlockSpec((d, LANES), full),
                  pl.BlockSpec((1, LANES), full)],
        out_specs=[pl.BlockSpec((1, tm, d), tok),
                   pl.BlockSpec((1, tm, d), tok),
                   pl.BlockSpec((1, tm, LANES), tok)],
        out_shape=[jax.ShapeDtypeStruct((bsz, t, d), F32),
                   jax.ShapeDtypeStruct((bsz, t, d), F32),
                   jax.ShapeDtypeStruct((bsz, t, LANES), F32)],
        compiler_params=_params(("arbitrary", "arbitrary")),
        name="outproj_router",
    )(x, pe, a, g, w_out, gate1, scale2, shift2, norm2_g, router_w, router_b)


def _route_pos_body(route_ref, pos_ref, cnt_ref, run_s, off_s, *, tile):
    p = pl.program_id(0)
    j = pl.program_id(1)
    nb = pl.num_programs(1)
    tb = route_ref.shape[0]
    route = route_ref[...]
    lane = lax.broadcasted_iota(jnp.int32, (tb, LANES), 1).astype(F32)
    onehots = [lane == route[:, k:k + 1] for k in range(TOP_K)]
    oh = onehots[0].astype(F32)
    for o in onehots[1:]:
        oh = oh + o.astype(F32)
    col_counts = jnp.sum(oh, axis=0, keepdims=True)

    @pl.when((p == 0) & (j == 0))
    def _():
        run_s[...] = jnp.zeros(run_s.shape, F32)

    @pl.when(p == 0)
    def _():
        run_s[...] += col_counts

    @pl.when((p == 0) & (j == nb - 1))
    def _():
        cnt_ref[...] = run_s[...]

    @pl.when((p == 1) & (j == 0))
    def _():
        cpad = jnp.floor((run_s[...] + float(tile - 1)) * (1.0 / tile)) * float(tile)
        cpad8 = jnp.broadcast_to(cpad, (8, LANES))
        lane8 = lax.broadcasted_iota(jnp.int32, (8, LANES), 1)
        inc = cpad8
        s = 1
        while s < LANES:
            inc = inc + jnp.where(lane8 >= s, pltpu.roll(inc, s, axis=1), 0.0)
            s *= 2
        off_s[...] = (inc - cpad8)[0:1]

    @pl.when(p == 1)
    def _():
        r = lax.broadcasted_iota(jnp.int32, (tb, tb), 0)
        c = lax.broadcasted_iota(jnp.int32, (tb, tb), 1)
        before = _dot((c < r).astype(BF16), oh.astype(BF16))
        base = off_s[...] + before
        pos = jnp.zeros((tb, LANES), F32)
        for k in range(TOP_K):
            pk = jnp.sum(jnp.where(onehots[k], base, 0.0), axis=1, keepdims=True)
            pos = jnp.where(lane == float(k), pk, pos)
        pos_ref[...] = pos[:, 0:TOP_K].astype(jnp.int32)
        off_s[...] += col_counts


def _route_pos(route, tile):
    n = route.shape[0]
    tb = 512 if n % 512 == 0 else 256
    nb = n // tb
    return pl.pallas_call(
        functools.partial(_route_pos_body, tile=tile),
        grid=(2, nb),
        in_specs=[pl.BlockSpec((tb, LANES), lambda p, j: (j, 0))],
        out_specs=[pl.BlockSpec((tb, TOP_K), lambda p, j: (p * j, 0)),
                   pl.BlockSpec((1, LANES), lambda p, j: (0, 0))],
        out_shape=[jax.ShapeDtypeStruct((n, TOP_K), jnp.int32),
                   jax.ShapeDtypeStruct((1, LANES), F32)],
        scratch_shapes=[pltpu.VMEM((1, LANES), F32), pltpu.VMEM((1, LANES), F32)],
        compiler_params=_params(("arbitrary", "arbitrary")),
        name="route_pos",
    )(route)


def _dispatch_body(zrow_ref, zflag_ref, nu_ref, pos_ref, u_ref, xs_ref, zero_s, sem, *, n_experts, tile):
    tb = u_ref.shape[0]
    n_tiles = xs_ref.shape[0] // tile

    @pl.when(pl.program_id(0) == 0)
    def _():
        zero_s[...] = jnp.zeros(zero_s.shape, F32)

        def zero_tile(r0):
            cp = pltpu.make_async_copy(zero_s, xs_ref.at[pl.ds(pl.multiple_of(r0, tile), tile)], sem)
            cp.start()
            cp.wait()

        for e in range(n_experts):
            @pl.when(zflag_ref[e] == 1)
            def _():
                zero_tile(zrow_ref[e])

        def tail(t, carry):
            zero_tile(t * tile)
            return carry

        lax.fori_loop(nu_ref[0], n_tiles, tail, 0)

    def row(r, carry):
        for k in range(TOP_K):
            p = pos_ref[0, 0, r * TOP_K + k]
            pltpu.make_async_copy(u_ref.at[pl.ds(r, 1)], xs_ref.at[pl.ds(p, 1)], sem).start()
        return carry

    lax.fori_loop(0, tb, row, 0, unroll=8)
    for k in range(TOP_K):
        pltpu.make_async_copy(u_ref, xs_ref.at[pl.ds(0, tb)], sem).wait()


def _dispatch(u2, pos, zrow, zflag, n_used, n_rows, tile):
    n, d = u2.shape
    tb = 512 if n % 512 == 0 else 256
    nb = n // tb
    n_experts = zrow.shape[0]
    grid_spec = pltpu.PrefetchScalarGridSpec(
        num_scalar_prefetch=3,
        grid=(nb,),
        in_specs=[pl.BlockSpec((1, 1, tb * TOP_K), lambda i, zr, zf, nu: (i, 0, 0), memory_space=pltpu.SMEM),
                  pl.BlockSpec((tb, d), lambda i, zr, zf, nu: (i, 0))],
        out_specs=pl.BlockSpec(memory_space=pl.ANY),
        scratch_shapes=[pltpu.VMEM((tile, d), F32), pltpu.SemaphoreType.DMA(())],
    )
    return pl.pallas_call(
        functools.partial(_dispatch_body, n_experts=n_experts, tile=tile),
        grid_spec=grid_spec,
        out_shape=jax.ShapeDtypeStruct((n_rows, d), F32),
        compiler_params=_params(("arbitrary",)),
        name="dispatch",
    )(zrow, zflag, n_used, pos.reshape(nb, 1, tb * TOP_K), u2)


def _experts_body(te_ref, nu_ref, x_ref, wgu_ref, bgu_ref, wd_ref, bd_ref, y_ref, *, d_expert):
    t = pl.program_id(0)

    @pl.when(t < nu_ref[0])
    def _():
        gu = _dot(x_ref[...].astype(BF16), wgu_ref[0]) + bgu_ref[0]
        gate = jnp.minimum(gu[:, :d_expert], SWIGLU_LIMIT)
        up = jnp.clip(gu[:, d_expert:], -SWIGLU_LIMIT, SWIGLU_LIMIT)
        act = ((up + 1.0) * (gate * _sigmoid(SWIGLU_ALPHA * gate))).astype(BF16)
        y_ref[...] = _dot(act, wd_ref[0]) + bd_ref[0]

    @pl.when(t >= nu_ref[0])
    def _():
        y_ref[...] = jnp.zeros(y_ref.shape, F32)


def _experts(xs, tile_expert, n_used, w_gu, b_gu, w_down, b_down, tile):
    n_rows, d = xs.shape
    n_experts, _, two_de = w_gu.shape
    d_expert = two_de // 2
    grid_spec = pltpu.PrefetchScalarGridSpec(
        num_scalar_prefetch=2,
        grid=(n_rows // tile,),
        in_specs=[pl.BlockSpec((tile, d), lambda t, te, nu: (jnp.minimum(t, nu[0] - 1), 0)),
                  pl.BlockSpec((1, d, two_de), lambda t, te, nu: (te[t], 0, 0)),
                  pl.BlockSpec((1, 1, two_de), lambda t, te, nu: (te[t], 0, 0)),
                  pl.BlockSpec((1, d_expert, d), lambda t, te, nu: (te[t], 0, 0)),
                  pl.BlockSpec((1, 1, d), lambda t, te, nu: (te[t], 0, 0))],
        out_specs=pl.BlockSpec((tile, d), lambda t, te, nu: (t, 0)),
    )
    return pl.pallas_call(
        functools.partial(_experts_body, d_expert=d_expert),
        grid_spec=grid_spec,
        out_shape=jax.ShapeDtypeStruct((n_rows, d), F32),
        compiler_params=_params(("arbitrary",)),
        name="experts",
    )(tile_expert, n_used, xs, w_gu, b_gu, w_down, b_down)


def _combine_body(pos_ref, posn_ref, route_ref, x1_ref, g5_ref, fg_ref, ys_ref, o_ref, buf, sem):
    i = pl.program_id(0)
    n = pl.num_programs(0)
    tb = x1_ref.shape[0]
    slot = i % 2

    def issue(p_ref, s):
        def row(r, carry):
            for k in range(TOP_K):
                p = p_ref[0, 0, r * TOP_K + k]
                pltpu.make_async_copy(ys_ref.at[pl.ds(p, 1)], buf.at[s, k, pl.ds(r, 1)], sem.at[s]).start()
            return carry
        lax.fori_loop(0, tb, row, 0, unroll=8)

    @pl.when(i == 0)
    def _():
        issue(pos_ref, 0)

    @pl.when(i + 1 < n)
    def _():
        issue(posn_ref, 1 - slot)

    for k in range(TOP_K):
        pltpu.make_async_copy(ys_ref.at[pl.ds(0, tb)], buf.at[slot, k], sem.at[slot]).wait()
    route = route_ref[...]
    moe = route[:, TOP_K:TOP_K + 1] * buf[slot, 0]
    for k in range(1, TOP_K):
        moe = moe + route[:, TOP_K + k:TOP_K + k + 1] * buf[slot, k]
    xo = x1_ref[...] + g5_ref[0] * moe
    o_ref[...] = xo * lax.rsqrt(jnp.mean(xo * xo, axis=-1, keepdims=True) + EPS) * fg_ref[...]


def _combine(ys, pos, route, x1, gate5, final_g, t_lat):
    n, d = x1.shape
    tb = 256
    nb = n // tb
    per_b = t_lat // tb
    pos3 = pos.reshape(nb, 1, tb * TOP_K)
    return pl.pallas_call(
        _combine_body,
        grid=(nb,),
        in_specs=[pl.BlockSpec((1, 1, tb * TOP_K), lambda i: (i, 0, 0), memory_space=pltpu.SMEM),
                  pl.BlockSpec((1, 1, tb * TOP_K), lambda i: (jnp.minimum(i + 1, nb - 1), 0, 0),
                               memory_space=pltpu.SMEM),
                  pl.BlockSpec((tb, LANES), lambda i: (i, 0)),
                  pl.BlockSpec((tb, d), lambda i: (i, 0)),
                  pl.BlockSpec((1, 1, d), lambda i: (i // per_b, 0, 0)),
                  pl.BlockSpec((1, d), lambda i: (0, 0)),
                  pl.BlockSpec(memory_space=pl.ANY)],
        out_specs=pl.BlockSpec((tb, d), lambda i: (i, 0)),
        out_shape=jax.ShapeDtypeStruct((n, d), F32),
        scratch_shapes=[pltpu.VMEM((2, TOP_K, tb, d), F32), pltpu.SemaphoreType.DMA((2,))],
        compiler_params=_params(("arbitrary",)),
        name="combine",
    )(pos3, pos3, route, x1, gate5, final_g, ys)


def _moe(u2, route, x1, w_gu, b_gu, w_down, b_down, gate5, final_g):
    bsz, t_lat, d = x1.shape
    n = bsz * t_lat
    n_experts = w_gu.shape[0]
    tile = EXPERT_TILE
    route2 = route.reshape(n, LANES)
    pos, counts = _route_pos(route2, tile)
    cnt = counts[0, :n_experts].astype(jnp.int32)
    cpad = ((cnt + tile - 1) // tile) * tile
    ends = jnp.cumsum(cpad)
    n_rows = n * TOP_K + n_experts * tile
    n_tiles = n_rows // tile
    tile_expert = jnp.minimum(
        jnp.sum(jnp.arange(n_tiles, dtype=jnp.int32)[:, None] * tile >= ends[None, :], axis=1), n_experts - 1
    ).astype(jnp.int32)
    n_used = (ends[-1:] // tile).astype(jnp.int32)
    zrow = jnp.maximum(ends - tile, 0).astype(jnp.int32)
    zflag = (cnt % tile != 0).astype(jnp.int32)
    xs = _dispatch(u2.reshape(n, d), pos, zrow, zflag, n_used, n_rows, tile)
    ys = _experts(xs, tile_expert, n_used, w_gu, b_gu, w_down, b_down, tile)
    out = _combine(ys, pos, route2, x1.reshape(n, d), gate5, final_g, t_lat)
    return out.reshape(bsz, t_lat, d)


def _grid_sincos(rows, d):
    nf = d // 4
    omega = 1.0 / (POS_BASE ** (jnp.arange(nf, dtype=F32) / nf))
    r = jnp.broadcast_to(jnp.arange(rows, dtype=F32)[:, None, None] * omega, (rows, GRID_W, nf))
    cl = jnp.broadcast_to(jnp.arange(GRID_W, dtype=F32)[None, :, None] * omega, (rows, GRID_W, nf))
    pe = jnp.concatenate([jnp.sin(r), jnp.cos(r), jnp.sin(cl), jnp.cos(cl)], axis=-1)
    return pe.reshape(rows * GRID_W, d)


def _pack_qk(wq, wk):
    lead = wq.shape[:-1]
    q = wq.reshape(lead + (N_HEADS, DK))
    k = wk.reshape(lead + (N_HEADS, DK))
    return jnp.concatenate([q, k], axis=-1).reshape(lead + (QK_W,))


def _row_tile(t):
    for tm in (512, 256, 128):
        if t % tm == 0:
            return tm
    raise ValueError(f"sequence length {t} must be a multiple of 128")


def kernel(x, c, ctx, c_ctx, ada_w, ada_b, norm1_g, w_in, ml_conv_w, ml_conv_b, ml_gate_b, ml_norm_g,
           gla_gate_w2, gla_gate_b, gla_norm_g, w_out, norm2_g, router_w, router_b, moe_w_gu, moe_b_gu,
           moe_w_down, moe_b_down, final_norm_g):
    bsz, t_lat, d = x.shape
    t_ctx = ctx.shape[1]
    assert ada_w.shape[0] == 1, "single-layer block"
    assert t_lat % ML_CHUNK == 0 and t_ctx % ML_CHUNK == 0 and t_lat % GRID_W == 0
    n_experts = router_w.shape[-1]
    assert n_experts <= LANES

    ml_qk, ml_v = N_HEADS * DK, N_HEADS * DV
    sizes = (ml_qk, ml_qk, ml_v, ml_v, 4 * N_HEADS, ml_qk, ml_qk, ml_v, ml_v, 2 * GLA_RANK)
    offs = [int(o) for o in np.cumsum(sizes)[:-1]]
    w_mq, w_mk, w_mv, w_mo, w_mg, w_gq, w_gk, w_gv, w_gg, w_glr = jnp.split(w_in[0], offs, axis=-1)
    w_main = jnp.concatenate([_pack_qk(w_mq, w_mk), w_mv, w_mo, _pack_qk(w_gq, w_gk), w_gv, w_gg],
                             axis=-1).astype(BF16)
    gate_perm = np.concatenate([np.arange(N_HEADS) + g * N_HEADS for g in (0, 2, 1, 3)])
    w_gate = jnp.concatenate([w_mg[:, gate_perm], w_glr,
                              jnp.zeros((d, GATE_W - 4 * N_HEADS - 2 * GLA_RANK), F32)], axis=-1).astype(BF16)
    gate_b = jnp.concatenate([ml_gate_b[0].reshape(-1)[gate_perm],
                              jnp.zeros((GATE_W - 4 * N_HEADS,), F32)]).reshape(1, GATE_W)
    conv_w = _pack_qk(ml_conv_w[0][:, :ml_qk], ml_conv_w[0][:, ml_qk:])
    conv_b = _pack_qk(ml_conv_b[0][:ml_qk], ml_conv_b[0][ml_qk:]).reshape(1, QK_W)
    w2 = gla_gate_w2[0].reshape(2, GLA_RANK, N_HEADS, DK)
    w2 = jnp.concatenate([w2, w2], axis=-1).reshape(2, GLA_RANK, N_HEADS * LANES)
    w2_ext = jnp.zeros((GATE_W, 2 * N_HEADS * LANES), F32)
    for dd in range(2):
        r0 = 4 * N_HEADS + dd * GLA_RANK
        w2_ext = w2_ext.at[r0:r0 + GLA_RANK, dd * N_HEADS * LANES:(dd + 1) * N_HEADS * LANES].set(w2[dd])
    b2 = gla_gate_b[0].reshape(2, N_HEADS, DK)
    b2_ext = jnp.concatenate([b2, b2], axis=-1).reshape(1, 2 * N_HEADS * LANES)
    router_w_p = jnp.concatenate([router_w[0], jnp.zeros((d, LANES - n_experts), F32)], axis=-1)
    router_b_p = jnp.concatenate([router_b[0], jnp.full((LANES - n_experts,), -jnp.inf, F32)]).reshape(1, LANES)

    cond = jnp.concatenate([c, c_ctx[None, :], jnp.zeros(((-bsz - 1) % 8, d), F32)], axis=0)
    mod = _adaln(cond, ada_w[0], ada_b[0]).reshape(cond.shape[0], N_MOD, 1, d)
    m_lat = mod[:bsz]
    m_ctx = mod[bsz:bsz + 1]

    pe = _grid_sincos(t_lat // GRID_W, d)
    g1 = norm1_g[0].reshape(1, d)
    tm = _row_tile(t_lat)
    zm_lat, zg_lat, gt_lat = _inproj(x, pe, m_lat[:, 1], m_lat[:, 0], g1, w_main, w_gate, tm)
    zm_ctx, zg_ctx, gt_ctx = _inproj(ctx, jnp.zeros((t_ctx, d), F32), m_ctx[:, 1], m_ctx[:, 0], g1,
                                     w_main, w_gate, _row_tile(t_ctx))

    a = _mlstm(zm_lat, zm_ctx, gt_lat, gt_ctx, conv_w, conv_b, gate_b, ml_norm_g[0].reshape(1, V_W))
    g = _gla(zg_lat, zg_ctx, gt_lat, gt_ctx, w2_ext, b2_ext, gla_norm_g[0].reshape(1, V_W))

    x1, u2, route = _outproj(x, pe, a, g, w_out[0].astype(BF16), m_lat[:, 2], m_lat[:, 4], m_lat[:, 3],
                             norm2_g[0].reshape(1, d), router_w_p, router_b_p, tm)

    return _moe(u2, route, x1, moe_w_gu[0].astype(BF16), moe_b_gu[0][:, None, :], moe_w_down[0].astype(BF16),
                moe_b_down[0][:, None, :], m_lat[:, 5], final_norm_g.reshape(1, d))
```

```python
import functools

import numpy as np
import jax
import jax.numpy as jnp
from jax import lax
from jax.experimental import pallas as pl
from jax.experimental.pallas import tpu as pltpu

F32 = jnp.float32
BF16 = jnp.bfloat16
HIGHEST = lax.Precision.HIGHEST

EPS = 1e-6
GRID_W = 64
POS_BASE = 10000.0
N_MOD = 6
N_HEADS = 4
DK = 64
DV = 128
QK_W = N_HEADS * 2 * DK
V_W = N_HEADS * DV
GROUP_W = QK_W + 2 * V_W
GATE_W = 128
ML_CONV = 3
GLA_RANK = 16
GLA_NORMALIZER = 16.0
TOP_K = 4
SWIGLU_LIMIT = 7.0
SWIGLU_ALPHA = 1.702
LANES = 128
ML_CHUNK = 128
GLA_CHUNK = 64
CONV_ROWS = 128
EXPERT_TILE = 512
VMEM_LIMIT = 56 * 1024 * 1024


def _params(sem):
    return pltpu.CompilerParams(dimension_semantics=sem, vmem_limit_bytes=VMEM_LIMIT)


def _sigmoid(x):
    return 1.0 / (1.0 + jnp.exp(-x))


def _log_sigmoid(x):
    return jnp.minimum(x, 0.0) - jnp.log1p(jnp.exp(-jnp.abs(x)))


def _dot(a, b, precision=None):
    return jnp.dot(a, b, preferred_element_type=F32, precision=precision)


def _split2(x):
    hi = x.astype(BF16)
    lo = (x - hi.astype(F32)).astype(BF16)
    return hi, lo


def _dot3(a, b):
    a_hi, a_lo = _split2(a)
    b_hi, b_lo = _split2(b)
    return _dot(a_hi, b_hi) + _dot(a_lo, b_hi) + _dot(a_hi, b_lo)


def _dot_nt(a, b):
    return lax.dot_general(a, b, (((1,), (1,)), ((), ())), preferred_element_type=F32)


def _dot_tn(a, b):
    return lax.dot_general(a, b, (((0,), (0,)), ((), ())), preferred_element_type=F32)


def _adaln_body(c_ref, w_ref, b_ref, o_ref):
    c = c_ref[...]
    s = c * _sigmoid(c)
    o_ref[...] = _dot3(s, w_ref[...]) + b_ref[...]


def _adaln(cond, w, b):
    rows, d = cond.shape
    n = w.shape[1]
    tn = 512
    return pl.pallas_call(
        _adaln_body,
        grid=(n // tn,),
        in_specs=[pl.BlockSpec((rows, d), lambda j: (0, 0)),
                  pl.BlockSpec((d, tn), lambda j: (0, j)),
                  pl.BlockSpec((1, tn), lambda j: (0, j))],
        out_specs=pl.BlockSpec((rows, tn), lambda j: (0, j)),
        out_shape=jax.ShapeDtypeStruct((rows, n), F32),
        compiler_params=_params(("arbitrary",)),
        name="adaln",
    )(cond, w, b.reshape(1, n))


def _rms_mod(x, g, scale, shift):
    y = x * lax.rsqrt(jnp.mean(x * x, axis=-1, keepdims=True) + EPS)
    return (y * g) * (1.0 + scale) + shift


def _inproj_body(x_ref, pe_ref, sc_ref, sh_ref, g_ref, wm_ref, wg_ref, zm_ref, zg_ref, gt_ref):
    x = x_ref[0] + pe_ref[...]
    u = _rms_mod(x, g_ref[...], sc_ref[0], sh_ref[0]).astype(BF16)
    z = _dot(u, wm_ref[...])
    zm_ref[0] = z[:, :GROUP_W].astype(BF16)
    zg_ref[0] = z[:, GROUP_W:].astype(BF16)
    gt_ref[0] = _dot(u, wg_ref[...])


def _inproj(x, pe, scale, shift, g, w_main, w_gate, tm):
    bsz, t, d = x.shape
    per_batch = scale.shape[0] == bsz and bsz > 1
    mod_map = (lambda i, b: (b, 0, 0)) if per_batch else (lambda i, b: (0, 0, 0))
    return pl.pallas_call(
        _inproj_body,
        grid=(t // tm, bsz),
        in_specs=[pl.BlockSpec((1, tm, d), lambda i, b: (b, i, 0)),
                  pl.BlockSpec((tm, d), lambda i, b: (i, 0)),
                  pl.BlockSpec((1, 1, d), mod_map),
                  pl.BlockSpec((1, 1, d), mod_map),
                  pl.BlockSpec((1, d), lambda i, b: (0, 0)),
                  pl.BlockSpec((d, 2 * GROUP_W), lambda i, b: (0, 0)),
                  pl.BlockSpec((d, GATE_W), lambda i, b: (0, 0))],
        out_specs=[pl.BlockSpec((1, tm, GROUP_W), lambda i, b: (b, i, 0)),
                   pl.BlockSpec((1, tm, GROUP_W), lambda i, b: (b, i, 0)),
                   pl.BlockSpec((1, tm, GATE_W), lambda i, b: (b, i, 0))],
        out_shape=[jax.ShapeDtypeStruct((bsz, t, GROUP_W), BF16),
                   jax.ShapeDtypeStruct((bsz, t, GROUP_W), BF16),
                   jax.ShapeDtypeStruct((bsz, t, GATE_W), F32)],
        compiler_params=_params(("arbitrary", "arbitrary")),
        name="inproj",
    )(x, pe, scale, shift, g, w_main, w_gate)


def _scan_rows(x, op, ident, reverse, rowi):
    n = x.shape[0]
    k = 1
    while k < n:
        if reverse:
            shifted = jnp.where(rowi < n - k, pltpu.roll(x, n - k, axis=0), ident)
        else:
            shifted = jnp.where(rowi >= k, pltpu.roll(x, k, axis=0), ident)
        x = op(x, shifted)
        k *= 2
    return x


GRP_NEG_M, GRP_ONE, GRP_NEG_B, GRP_MPREV = 0, 2, 4, 6
HEAD_DIRS = 2 * N_HEADS


def _mlstm_body(zl_ref, zc_ref, gl_ref, gc_ref, cw_ref, cb_ref, gb_ref, ng_ref, o_ref,
                q_s, kt_s, acc_s, cc_s, bc_s, st_s, mp_s, a_s, b0_s, wk_s, wc_s, c_s, *, t_lat, t_ctx):
    L = ML_CHUNK
    nc_ctx = t_ctx // L
    nc_lat = t_lat // L
    nc = nc_ctx + nc_lat
    scale = DK ** -0.5

    cw = cw_ref[...]
    cb = cb_ref[...]

    def conv_pass(z_ref, n, dst0):
        nb = n // CONV_ROWS
        row = lax.broadcasted_iota(jnp.int32, (CONV_ROWS, QK_W), 0)
        lane = lax.broadcasted_iota(jnp.int32, (CONV_ROWS, QK_W), 1)
        low = (lane % LANES) < DK

        def body(r, carry):
            r0 = pl.multiple_of(r * CONV_ROWS, CONV_ROWS)
            zc = z_ref[0, pl.ds(r0, CONV_ROWS), 0:QK_W].astype(F32)
            p0 = pl.multiple_of(jnp.maximum(r0 - 16, 0), 16)
            prev = z_ref[0, pl.ds(p0, 16), 0:QK_W].astype(F32)[15:16]
            prev = jnp.where(r > 0, prev, 0.0)
            n0 = pl.multiple_of(jnp.minimum(r0 + CONV_ROWS, n - 16), 16)
            nxt = z_ref[0, pl.ds(n0, 16), 0:QK_W].astype(F32)[0:1]
            nxt = jnp.where(r < nb - 1, nxt, 0.0)
            up = jnp.where(row == 0, prev, pltpu.roll(zc, 1, axis=0))
            dn = jnp.where(row == CONV_ROWS - 1, nxt, pltpu.roll(zc, CONV_ROWS - 1, axis=0))
            y = cw[0:1] * up + cw[1:2] * zc + cw[2:3] * dn + cb
            y = y * _sigmoid(y)
            q = jnp.where(low, y * scale, 0.0)
            k = jnp.where(low, pltpu.roll(y, QK_W - DK, axis=1), 0.0)
            d0 = pl.multiple_of(dst0 + r0, CONV_ROWS)
            q_s[pl.ds(d0, CONV_ROWS), :] = q.astype(BF16)
            for h in range(N_HEADS):
                kt_s[h, :, pl.ds(d0, CONV_ROWS)] = k[:, h * LANES:(h + 1) * LANES].T.astype(BF16)
            return carry

        lax.fori_loop(0, nb, body, 0)

    conv_pass(zc_ref, t_ctx, 0)
    conv_pass(zl_ref, t_lat, t_ctx)

    rowi = lax.broadcasted_iota(jnp.int32, (L, LANES), 0)
    lane_g = lax.broadcasted_iota(jnp.int32, (L, GATE_W), 1)
    fwd_lane = lane_g < N_HEADS
    gb = gb_ref[...]

    def gate_pass(g_ref, nchunks, g0):
        def body(i, carry):
            r0 = pl.multiple_of(i * L, L)
            gates = g_ref[0, pl.ds(r0, L), :] + gb
            lf = pltpu.roll(_log_sigmoid(gates), GATE_W - HEAD_DIRS, axis=1)
            b = jnp.where(fwd_lane, _scan_rows(lf, jnp.add, 0.0, False, rowi),
                          _scan_rows(lf, jnp.add, 0.0, True, rowi))
            c = gates - b
            cc_s[g0 + i] = c
            bc_s[g0 + i] = b
            st_s[g0 + i, 0:1, :] = jnp.max(c, axis=0, keepdims=True)
            st_s[g0 + i, 1:2, :] = jnp.sum(lf, axis=0, keepdims=True)
            return carry

        lax.fori_loop(0, nchunks, body, 0)

    gate_pass(gc_ref, nc_ctx, 0)
    gate_pass(gl_ref, nc_lat, nc_ctx)

    lane1 = lax.broadcasted_iota(jnp.int32, (1, GATE_W), 1)
    fwd_order = list(range(nc))
    bwd_order = list(range(nc_ctx - 1, -1, -1)) + list(range(nc - 1, nc_ctx - 1, -1))
    mp = {}
    for d, order in enumerate((fwd_order, bwd_order)):
        m = jnp.zeros((1, GATE_W), F32)
        for g in order:
            mp[(d, g)] = m
            m = st_s[g, 1:2, :] + jnp.maximum(m, st_s[g, 0:1, :])
    for g in range(nc):
        mp_s[g, 0:1, :] = jnp.where(lane1 < N_HEADS, mp[(0, g)], mp[(1, g)])

    def group_mask(grp):
        return (lane_g >= HEAD_DIRS * grp) & (lane_g < HEAD_DIRS * (grp + 1))

    def ab_body(g, carry):
        c = cc_s[g]
        b = bc_s[g]
        mprev = mp_s[g, 0:1, :]
        mloc = jnp.where(fwd_lane, _scan_rows(c, jnp.maximum, -jnp.inf, False, rowi),
                         _scan_rows(c, jnp.maximum, -jnp.inf, True, rowi))
        m = jnp.maximum(mloc, mprev)
        mlast = jnp.maximum(mprev, st_s[g, 0:1, :])
        a = jnp.where(group_mask(GRP_ONE) | group_mask(GRP_ONE + 1), 1.0, 0.0)
        pieces = ((-m, GRP_NEG_M), (-b, GRP_NEG_B), (jnp.broadcast_to(mprev, (L, GATE_W)), GRP_MPREV))
        for val, grp in pieces:
            for t, part in enumerate(_split2(val)):
                shift = HEAD_DIRS * (grp + t)
                part = part.astype(F32)
                if shift:
                    part = pltpu.roll(part, shift, axis=1)
                a = jnp.where(group_mask(grp + t), part, a)
        a_s[g] = a.astype(BF16)
        cr_hi, cr_lo = _split2(c.T)
        b0 = jnp.concatenate([jnp.ones((2 * HEAD_DIRS, L), F32), cr_hi[0:HEAD_DIRS].astype(F32),
                              cr_lo[0:HEAD_DIRS].astype(F32), jnp.zeros((LANES - 4 * HEAD_DIRS, L), F32)], axis=0)
        b0_s[g] = b0.astype(BF16)
        wk_s[g] = jnp.exp((c - mlast).T[0:HEAD_DIRS])
        wc_s[g] = jnp.exp(jnp.broadcast_to(mprev - mlast, (L, GATE_W)).T[0:HEAD_DIRS])
        return carry

    lax.fori_loop(0, nc, ab_body, 0)

    def row_group(*grps):
        hit = rowi < 0
        for grp in grps:
            hit = hit | ((rowi >= HEAD_DIRS * grp) & (rowi < HEAD_DIRS * (grp + 2)))
        return hit.astype(BF16)

    bconst = jnp.concatenate([row_group(GRP_NEG_M, GRP_MPREV), row_group(GRP_NEG_M, GRP_NEG_B)], axis=1)

    colL = lax.broadcasted_iota(jnp.int32, (L, L), 1)
    rowL = lax.broadcasted_iota(jnp.int32, (L, L), 0)
    masks = (colL <= rowL, colL >= rowL)
    head_lane = [(lane_g % HEAD_DIRS) == j for j in range(HEAD_DIRS)]
    ones_blk = jnp.ones((L, DV), BF16)
    c_s[...] = jnp.zeros(c_s.shape, F32)

    def twice(x):
        return jnp.concatenate([x, x], axis=1)

    def step(g, z_ref, r_src, r_q, d, with_out, r_out):
        a_g = a_s[g]
        for h in range(N_HEADS):
            j = d * N_HEADS + h
            hs = slice(h * DV, (h + 1) * DV)
            kt = kt_s[h, :, pl.ds(r_q, L)]
            v = z_ref[0, pl.ds(r_src, L), QK_W + h * DV:QK_W + (h + 1) * DV]
            vext = jnp.concatenate([v, ones_blk], axis=1)
            state = c_s[j]
            if with_out:
                a_j = jnp.where(head_lane[j], a_g, jnp.zeros_like(a_g))
                ec = _dot(a_j, bconst)
                q = q_s[pl.ds(r_q, L), hs]
                w = jnp.where(masks[d], jnp.exp(_dot(a_j, b0_s[g])), 0.0)
                p = (_dot(q, kt) * w).astype(BF16)
                nd = _dot(p, vext) + twice(jnp.exp(ec[:, 0:LANES])) * _dot(q, state.astype(BF16))
                hh = nd[:, :DV] / jnp.maximum(jnp.abs(nd[:, DV:]), jnp.exp(ec[:, LANES:2 * LANES]))
                acc_s[d, pl.ds(r_out, L), hs] = hh
            ktw = (kt.astype(F32) * wk_s[g, j:j + 1, :]).astype(BF16)
            c_s[j] = twice(wc_s[g, j:j + 1, :]) * state + _dot(ktw, vext)

    def ctx_body(i, carry):
        for d in (0, 1):
            ii = i if d == 0 else nc_ctx - 1 - i
            r0 = pl.multiple_of(ii * L, L)
            step(ii, zc_ref, r0, r0, d, False, None)
        return carry

    def lat_body(i, carry):
        for d in (0, 1):
            ii = i if d == 0 else nc_lat - 1 - i
            r0 = pl.multiple_of(ii * L, L)
            step(nc_ctx + ii, zl_ref, r0, pl.multiple_of(t_ctx + ii * L, L), d, True, r0)
        return carry

    lax.fori_loop(0, nc_ctx, ctx_body, 0)
    lax.fori_loop(0, nc_lat, lat_body, 0)

    def out_body(i, carry):
        r0 = pl.multiple_of(i * L, L)
        for h in range(N_HEADS):
            hs = slice(h * DV, (h + 1) * DV)
            tot = acc_s[0, pl.ds(r0, L), hs] + acc_s[1, pl.ds(r0, L), hs]
            tot = tot * lax.rsqrt(jnp.mean(tot * tot, axis=-1, keepdims=True) + EPS)
            og = zl_ref[0, pl.ds(r0, L), QK_W + V_W + h * DV:QK_W + V_W + (h + 1) * DV].astype(F32)
            o_ref[0, pl.ds(r0, L), hs] = (tot * ng_ref[:, hs] * _sigmoid(og)).astype(BF16)
        return carry

    lax.fori_loop(0, nc_lat, out_body, 0)


def _mlstm(zm_lat, zm_ctx, gt_lat, gt_ctx, conv_w, conv_b, gate_b, norm_g):
    bsz, t_lat, _ = zm_lat.shape
    t_ctx = zm_ctx.shape[1]
    nc = (t_lat + t_ctx) // ML_CHUNK
    body = functools.partial(_mlstm_body, t_lat=t_lat, t_ctx=t_ctx)
    full = lambda b: (0, 0)
    return pl.pallas_call(
        body,
        grid=(bsz,),
        in_specs=[pl.BlockSpec((1, t_lat, GROUP_W), lambda b: (b, 0, 0)),
                  pl.BlockSpec((1, t_ctx, GROUP_W), lambda b: (b, 0, 0)),
                  pl.BlockSpec((1, t_lat, GATE_W), lambda b: (b, 0, 0)),
                  pl.BlockSpec((1, t_ctx, GATE_W), lambda b: (b, 0, 0)),
                  pl.BlockSpec((ML_CONV, QK_W), full),
                  pl.BlockSpec((1, QK_W), full),
                  pl.BlockSpec((1, GATE_W), full),
                  pl.BlockSpec((1, V_W), full)],
        out_specs=pl.BlockSpec((1, t_lat, V_W), lambda b: (b, 0, 0)),
        out_shape=jax.ShapeDtypeStruct((bsz, t_lat, V_W), BF16),
        scratch_shapes=[pltpu.VMEM((t_ctx + t_lat, QK_W), BF16),
                        pltpu.VMEM((N_HEADS, LANES, t_ctx + t_lat), BF16),
                        pltpu.VMEM((2, t_lat, V_W), F32),
                        pltpu.VMEM((nc, ML_CHUNK, GATE_W), F32),
                        pltpu.VMEM((nc, ML_CHUNK, GATE_W), F32),
                        pltpu.VMEM((nc, 8, GATE_W), F32),
                        pltpu.VMEM((nc, 8, GATE_W), F32),
                        pltpu.VMEM((nc, ML_CHUNK, GATE_W), BF16),
                        pltpu.VMEM((nc, LANES, ML_CHUNK), BF16),
                        pltpu.VMEM((nc, HEAD_DIRS, ML_CHUNK), F32),
                        pltpu.VMEM((nc, HEAD_DIRS, GATE_W), F32),
                        pltpu.VMEM((HEAD_DIRS, 2 * DK, 2 * DV), F32)],
        compiler_params=_params(("arbitrary",)),
        name="mlstm",
    )(zm_lat, zm_ctx, gt_lat, gt_ctx, conv_w, conv_b, gate_b, norm_g)


def _gla_body(zl_ref, zc_ref, gl_ref, gc_ref, w2_ref, b2_ref, ng_ref, o_ref,
              acc_s, s_s, qt_s, kt_s, ee_s, *, t_lat, t_ctx):
    L = GLA_CHUNK
    nc_ctx = t_ctx // L
    nc_lat = t_lat // L
    scale = DK ** -0.5
    half = N_HEADS * LANES

    rowi = lax.broadcasted_iota(jnp.int32, (L, L), 0)
    coli = lax.broadcasted_iota(jnp.int32, (L, L), 1)
    masks = (coli <= rowi, coli >= rowi)
    tri_b = (masks[0].astype(BF16), masks[1].astype(BF16))
    low = lax.broadcasted_iota(jnp.int32, (L, LANES), 1) < DK
    w2_hi, w2_lo = _split2(w2_ref[...])
    bias2 = b2_ref[...]

    def prep(z_ref, g_ref, nchunks, c0, row0):
        def body(i, carry):
            r0 = pl.multiple_of(i * L, L)
            rq = pl.multiple_of(row0 + i * L, L)
            g_hi, g_lo = _split2(g_ref[0, pl.ds(r0, L), :])
            pre = _dot(g_hi, w2_hi) + _dot(g_lo, w2_hi) + _dot(g_hi, w2_lo) + bias2
            la_hi, la_lo = _split2(_log_sigmoid(pre) * (1.0 / GLA_NORMALIZER))
            for d in (0, 1):
                ds_ = slice(d * half, (d + 1) * half)
                b_all = _dot(tri_b[d], la_hi[:, ds_]) + _dot(tri_b[d], la_lo[:, ds_])
                for h in range(N_HEADS):
                    hs = slice(h * LANES, (h + 1) * LANES)
                    b2 = b_all[:, hs]
                    b_end = b2[L - 1:L] if d == 0 else b2[0:1]
                    qk = z_ref[0, pl.ds(r0, L), hs].astype(F32)
                    qkt = qk * jnp.exp(jnp.where(low, b2, -b2))
                    qt_s[d, pl.ds(rq, L), hs] = jnp.where(low, qkt * scale, 0.0).astype(BF16)
                    kt_s[d, pl.ds(rq, L), hs] = jnp.where(low, pltpu.roll(qkt, DK, axis=1), 0.0).astype(BF16)
                    ee_s[d, c0 + i, 0:1, hs] = jnp.exp(b_end)
            return carry

        lax.fori_loop(0, nchunks, body, 0, unroll=2)

    prep(zc_ref, gc_ref, nc_ctx, 0, 0)
    prep(zl_ref, gl_ref, nc_lat, nc_ctx, t_ctx)

    s_s[...] = jnp.zeros(s_s.shape, F32)

    def step(z_ref, r0, rq, c, d, with_out):
        for h in range(N_HEADS):
            j = d * N_HEADS + h
            hs = slice(h * LANES, (h + 1) * LANES)
            qt = qt_s[d, pl.ds(rq, L), hs]
            kt = kt_s[d, pl.ds(rq, L), hs]
            v = z_ref[0, pl.ds(r0, L), QK_W + h * DV:QK_W + (h + 1) * DV]
            state = s_s[j]
            if with_out:
                a = jnp.where(masks[d], _dot_nt(qt, kt), 0.0).astype(BF16)
                acc_s[d, pl.ds(r0, L), hs] = _dot(a, v) + _dot_nt(qt, state.astype(BF16))
            e_end = ee_s[d, c, 0:1, hs]
            s_s[j] = state * e_end + _dot_tn(v, (kt.astype(F32) * e_end).astype(BF16))

    def loop(z_ref, n, c0, row0, with_out):
        def body(i, carry):
            for d in (0, 1):
                ii = i if d == 0 else n - 1 - i
                r0 = pl.multiple_of(ii * L, L)
                step(z_ref, r0, pl.multiple_of(row0 + ii * L, L), c0 + ii, d, with_out)
            return carry
        lax.fori_loop(0, n, body, 0, unroll=2)

    loop(zc_ref, nc_ctx, 0, 0, False)
    loop(zl_ref, nc_lat, nc_ctx, t_ctx, True)

    def out_body(i, carry):
        r0 = pl.multiple_of(i * L, L)
        for h in range(N_HEADS):
            hs = slice(h * LANES, (h + 1) * LANES)
            tot = acc_s[0, pl.ds(r0, L), hs] + acc_s[1, pl.ds(r0, L), hs]
            tot = tot * lax.rsqrt(jnp.mean(tot * tot, axis=-1, keepdims=True) + EPS)
            gg = zl_ref[0, pl.ds(r0, L), QK_W + V_W + h * DV:QK_W + V_W + (h + 1) * DV].astype(F32)
            o_ref[0, pl.ds(r0, L), hs] = (tot * ng_ref[:, hs] * (gg * _sigmoid(gg))).astype(BF16)
        return carry

    lax.fori_loop(0, nc_lat, out_body, 0)


def _gla(zg_lat, zg_ctx, gt_lat, gt_ctx, w2_ext, b2_ext, norm_g):
    bsz, t_lat, _ = zg_lat.shape
    t_ctx = zg_ctx.shape[1]
    body = functools.partial(_gla_body, t_lat=t_lat, t_ctx=t_ctx)
    full = lambda b: (0, 0)
    return pl.pallas_call(
        body,
        grid=(bsz,),
        in_specs=[pl.BlockSpec((1, t_lat, GROUP_W), lambda b: (b, 0, 0)),
                  pl.BlockSpec((1, t_ctx, GROUP_W), lambda b: (b, 0, 0)),
                  pl.BlockSpec((1, t_lat, GATE_W), lambda b: (b, 0, 0)),
                  pl.BlockSpec((1, t_ctx, GATE_W), lambda b: (b, 0, 0)),
                  pl.BlockSpec((GATE_W, 2 * N_HEADS * LANES), full),
                  pl.BlockSpec((1, 2 * N_HEADS * LANES), full),
                  pl.BlockSpec((1, V_W), full)],
        out_specs=pl.BlockSpec((1, t_lat, V_W), lambda b: (b, 0, 0)),
        out_shape=jax.ShapeDtypeStruct((bsz, t_lat, V_W), BF16),
        scratch_shapes=[pltpu.VMEM((2, t_lat, V_W), F32),
                        pltpu.VMEM((HEAD_DIRS, DV, LANES), F32),
                        pltpu.VMEM((2, t_ctx + t_lat, QK_W), BF16),
                        pltpu.VMEM((2, t_ctx + t_lat, QK_W), BF16),
                        pltpu.VMEM((2, (t_ctx + t_lat) // GLA_CHUNK, 8, QK_W), F32)],
        compiler_params=_params(("arbitrary",)),
        name="gla",
    )(zg_lat, zg_ctx, gt_lat, gt_ctx, w2_ext, b2_ext, norm_g)


def _outproj_body(x_ref, pe_ref, a_ref, g_ref, wo_ref, g1_ref, sc_ref, sh_ref, n2_ref, rw_ref, rb_ref,
                  x1_ref, u2_ref, route_ref):
    x = x_ref[0] + pe_ref[...]
    y = _dot(a_ref[0], wo_ref[0:V_W, :]) + _dot(g_ref[0], wo_ref[V_W:2 * V_W, :])
    x1 = x + g1_ref[0] * y
    x1_ref[0] = x1
    u2 = _rms_mod(x1, n2_ref[...], sc_ref[0], sh_ref[0])
    u2_ref[0] = u2
    logits = _dot3(u2, rw_ref[...]) + rb_ref[...]
    lane = lax.broadcasted_iota(jnp.int32, logits.shape, 1).astype(F32)
    ids = []
    vals = []
    for _ in range(TOP_K):
        m = jnp.max(logits, axis=1, keepdims=True)
        idx = jnp.min(jnp.where(logits == m, lane, float(LANES)), axis=1, keepdims=True)
        ids.append(idx)
        vals.append(m)
        logits = jnp.where(lane == idx, -jnp.inf, logits)
    exps = [jnp.exp(v - vals[0]) for v in vals]
    denom = exps[0]
    for e in exps[1:]:
        denom = denom + e
    route = jnp.zeros(logits.shape, F32)
    for k in range(TOP_K):
        route = jnp.where(lane == float(k), ids[k], route)
        route = jnp.where(lane == float(TOP_K + k), exps[k] / denom, route)
    route_ref[0] = route


def _outproj(x, pe, a, g, w_out, gate1, scale2, shift2, norm2_g, router_w, router_b, tm):
    bsz, t, d = x.shape
    tok = lambda i, b: (b, i, 0)
    mod = lambda i, b: (b, 0, 0)
    full = lambda i, b: (0, 0)
    return pl.pallas_call(
        _outproj_body,
        grid=(t // tm, bsz),
        in_specs=[pl.BlockSpec((1, tm, d), tok),
                  pl.BlockSpec((tm, d), lambda i, b: (i, 0)),
                  pl.BlockSpec((1, tm, V_W), tok),
                  pl.BlockSpec((1, tm, V_W), tok),
                  pl.BlockSpec((2 * V_W, d), full),
                  pl.BlockSpec((1, 1, d), mod),
                  pl.BlockSpec((1, 1, d), mod),
                  pl.BlockSpec((1, 1, d), mod),
                  pl.BlockSpec((1, d), full),
                  pl.BlockSpec((d, LANES), full),
                  pl.BlockSpec((1, LANES), full)],
        out_specs=[pl.BlockSpec((1, tm, d), tok),
                   pl.BlockSpec((1, tm, d), tok),
                   pl.BlockSpec((1, tm, LANES), tok)],
        out_shape=[jax.ShapeDtypeStruct((bsz, t, d), F32),
                   jax.ShapeDtypeStruct((bsz, t, d), F32),
                   jax.ShapeDtypeStruct((bsz, t, LANES), F32)],
        compiler_params=_params(("arbitrary", "arbitrary")),
        name="outproj_router",
    )(x, pe, a, g, w_out, gate1, scale2, shift2, norm2_g, router_w, router_b)


def _route_pos_body(route_ref, pos_ref, cnt_ref, run_s, off_s, *, tile):
    p = pl.program_id(0)
    j = pl.program_id(1)
    nb = pl.num_programs(1)
    tb = route_ref.shape[0]
    route = route_ref[...]
    lane = lax.broadcasted_iota(jnp.int32, (tb, LANES), 1).astype(F32)
    onehots = [lane == route[:, k:k + 1] for k in range(TOP_K)]
    oh = onehots[0].astype(F32)
    for o in onehots[1:]:
        oh = oh + o.astype(F32)
    col_counts = jnp.sum(oh, axis=0, keepdims=True)

    @pl.when((p == 0) & (j == 0))
    def _():
        run_s[...] = jnp.zeros(run_s.shape, F32)

    @pl.when(p == 0)
    def _():
        run_s[...] += col_counts

    @pl.when((p == 0) & (j == nb - 1))
    def _():
        cnt_ref[...] = run_s[...]

    @pl.when((p == 1) & (j == 0))
    def _():
        cpad = jnp.floor((run_s[...] + float(tile - 1)) * (1.0 / tile)) * float(tile)
        cpad8 = jnp.broadcast_to(cpad, (8, LANES))
        lane8 = lax.broadcasted_iota(jnp.int32, (8, LANES), 1)
        inc = cpad8
        s = 1
        while s < LANES:
            inc = inc + jnp.where(lane8 >= s, pltpu.roll(inc, s, axis=1), 0.0)
            s *= 2
        off_s[...] = (inc - cpad8)[0:1]

    @pl.when(p == 1)
    def _():
        r = lax.broadcasted_iota(jnp.int32, (tb, tb), 0)
        c = lax.broadcasted_iota(jnp.int32, (tb, tb), 1)
        before = _dot((c < r).astype(BF16), oh.astype(BF16))
        base = off_s[...] + before
        pos = jnp.zeros((tb, LANES), F32)
        for k in range(TOP_K):
            pk = jnp.sum(jnp.where(onehots[k], base, 0.0), axis=1, keepdims=True)
            pos = jnp.where(lane == float(k), pk, pos)
        pos_ref[...] = pos[:, 0:TOP_K].astype(jnp.int32)
        off_s[...] += col_counts


def _route_pos(route, tile):
    n = route.shape[0]
    tb = 512 if n % 512 == 0 else 256
    nb = n // tb
    return pl.pallas_call(
        functools.partial(_route_pos_body, tile=tile),
        grid=(2, nb),
        in_specs=[pl.BlockSpec((tb, LANES), lambda p, j: (j, 0))],
        out_specs=[pl.BlockSpec((tb, TOP_K), lambda p, j: (p * j, 0)),
                   pl.BlockSpec((1, LANES), lambda p, j: (0, 0))],
        out_shape=[jax.ShapeDtypeStruct((n, TOP_K), jnp.int32),
                   jax.ShapeDtypeStruct((1, LANES), F32)],
        scratch_shapes=[pltpu.VMEM((1, LANES), F32), pltpu.VMEM((1, LANES), F32)],
        compiler_params=_params(("arbitrary", "arbitrary")),
        name="route_pos",
    )(route)


def _dispatch_body(zrow_ref, zflag_ref, nu_ref, pos_ref, u_ref, xs_ref, zero_s, sem, *, n_experts, tile):
    tb = u_ref.shape[0]
    n_tiles = xs_ref.shape[0] // tile

    @pl.when(pl.program_id(0) == 0)
    def _():
        zero_s[...] = jnp.zeros(zero_s.shape, F32)

        def zero_tile(r0):
            cp = pltpu.make_async_copy(zero_s, xs_ref.at[pl.ds(pl.multiple_of(r0, tile), tile)], sem)
            cp.start()
            cp.wait()

        for e in range(n_experts):
            @pl.when(zflag_ref[e] == 1)
            def _():
                zero_tile(zrow_ref[e])

        def tail(t, carry):
            zero_tile(t * tile)
            return carry

        lax.fori_loop(nu_ref[0], n_tiles, tail, 0)

    def row(r, carry):
        for k in range(TOP_K):
            p = pos_ref[0, 0, r * TOP_K + k]
            pltpu.make_async_copy(u_ref.at[pl.ds(r, 1)], xs_ref.at[pl.ds(p, 1)], sem).start()
        return carry

    lax.fori_loop(0, tb, row, 0, unroll=8)
    for k in range(TOP_K):
        pltpu.make_async_copy(u_ref, xs_ref.at[pl.ds(0, tb)], sem).wait()


def _dispatch(u2, pos, zrow, zflag, n_used, n_rows, tile):
    n, d = u2.shape
    tb = 512 if n % 512 == 0 else 256
    nb = n // tb
    n_experts = zrow.shape[0]
    grid_spec = pltpu.PrefetchScalarGridSpec(
        num_scalar_prefetch=3,
        grid=(nb,),
        in_specs=[pl.BlockSpec((1, 1, tb * TOP_K), lambda i, zr, zf, nu: (i, 0, 0), memory_space=pltpu.SMEM),
                  pl.BlockSpec((tb, d), lambda i, zr, zf, nu: (i, 0))],
        out_specs=pl.BlockSpec(memory_space=pl.ANY),
        scratch_shapes=[pltpu.VMEM((tile, d), F32), pltpu.SemaphoreType.DMA(())],
    )
    return pl.pallas_call(
        functools.partial(_dispatch_body, n_experts=n_experts, tile=tile),
        grid_spec=grid_spec,
        out_shape=jax.ShapeDtypeStruct((n_rows, d), F32),
        compiler_params=_params(("arbitrary",)),
        name="dispatch",
    )(zrow, zflag, n_used, pos.reshape(nb, 1, tb * TOP_K), u2)


def _experts_body(te_ref, nu_ref, x_ref, wgu_ref, bgu_ref, wd_ref, bd_ref, y_ref, wgu_s, wd_s, *, d_expert):
    t = pl.program_id(0)

    @pl.when((t == 0) | (te_ref[t] != te_ref[jnp.maximum(t - 1, 0)]))
    def _():
        wgu_s[...] = wgu_ref[0].astype(BF16)
        wd_s[...] = wd_ref[0].astype(BF16)

    @pl.when(t < nu_ref[0])
    def _():
        gu = _dot(x_ref[...].astype(BF16), wgu_s[...]) + bgu_ref[0]
        gate = jnp.minimum(gu[:, :d_expert], SWIGLU_LIMIT)
        up = jnp.clip(gu[:, d_expert:], -SWIGLU_LIMIT, SWIGLU_LIMIT)
        act = ((up + 1.0) * (gate * _sigmoid(SWIGLU_ALPHA * gate))).astype(BF16)
        y_ref[...] = _dot(act, wd_s[...]) + bd_ref[0]

    @pl.when(t >= nu_ref[0])
    def _():
        y_ref[...] = jnp.zeros(y_ref.shape, F32)


def _experts(xs, tile_expert, n_used, w_gu, b_gu, w_down, b_down, tile):
    n_rows, d = xs.shape
    n_experts, _, two_de = w_gu.shape
    d_expert = two_de // 2
    grid_spec = pltpu.PrefetchScalarGridSpec(
        num_scalar_prefetch=2,
        grid=(n_rows // tile,),
        in_specs=[pl.BlockSpec((tile, d), lambda t, te, nu: (jnp.minimum(t, nu[0] - 1), 0)),
                  pl.BlockSpec((1, d, two_de), lambda t, te, nu: (te[t], 0, 0)),
                  pl.BlockSpec((1, 1, two_de), lambda t, te, nu: (te[t], 0, 0)),
                  pl.BlockSpec((1, d_expert, d), lambda t, te, nu: (te[t], 0, 0)),
                  pl.BlockSpec((1, 1, d), lambda t, te, nu: (te[t], 0, 0))],
        out_specs=pl.BlockSpec((tile, d), lambda t, te, nu: (t, 0)),
        scratch_shapes=[pltpu.VMEM((d, two_de), BF16), pltpu.VMEM((d_expert, d), BF16)],
    )
    return pl.pallas_call(
        functools.partial(_experts_body, d_expert=d_expert),
        grid_spec=grid_spec,
        out_shape=jax.ShapeDtypeStruct((n_rows, d), F32),
        compiler_params=_params(("arbitrary",)),
        name="experts",
    )(tile_expert, n_used, xs, w_gu, b_gu, w_down, b_down)


def _combine_body(pos_ref, posn_ref, route_ref, x1_ref, g5_ref, fg_ref, ys_ref, o_ref, buf, sem):
    i = pl.program_id(0)
    n = pl.num_programs(0)
    tb = x1_ref.shape[0]
    slot = i % 2

    def issue(p_ref, s):
        def row(r, carry):
            for k in range(TOP_K):
                p = p_ref[0, 0, r * TOP_K + k]
                pltpu.make_async_copy(ys_ref.at[pl.ds(p, 1)], buf.at[s, k, pl.ds(r, 1)], sem.at[s]).start()
            return carry
        lax.fori_loop(0, tb, row, 0, unroll=8)

    @pl.when(i == 0)
    def _():
        issue(pos_ref, 0)

    @pl.when(i + 1 < n)
    def _():
        issue(posn_ref, 1 - slot)

    for k in range(TOP_K):
        pltpu.make_async_copy(ys_ref.at[pl.ds(0, tb)], buf.at[slot, k], sem.at[slot]).wait()
    route = route_ref[...]
    moe = route[:, TOP_K:TOP_K + 1] * buf[slot, 0]
    for k in range(1, TOP_K):
        moe = moe + route[:, TOP_K + k:TOP_K + k + 1] * buf[slot, k]
    xo = x1_ref[...] + g5_ref[0] * moe
    o_ref[...] = xo * lax.rsqrt(jnp.mean(xo * xo, axis=-1, keepdims=True) + EPS) * fg_ref[...]


def _combine(ys, pos, route, x1, gate5, final_g, t_lat):
    n, d = x1.shape
    tb = 256
    nb = n // tb
    per_b = t_lat // tb
    pos3 = pos.reshape(nb, 1, tb * TOP_K)
    return pl.pallas_call(
        _combine_body,
        grid=(nb,),
        in_specs=[pl.BlockSpec((1, 1, tb * TOP_K), lambda i: (i, 0, 0), memory_space=pltpu.SMEM),
                  pl.BlockSpec((1, 1, tb * TOP_K), lambda i: (jnp.minimum(i + 1, nb - 1), 0, 0),
                               memory_space=pltpu.SMEM),
                  pl.BlockSpec((tb, LANES), lambda i: (i, 0)),
                  pl.BlockSpec((tb, d), lambda i: (i, 0)),
                  pl.BlockSpec((1, 1, d), lambda i: (i // per_b, 0, 0)),
                  pl.BlockSpec((1, d), lambda i: (0, 0)),
                  pl.BlockSpec(memory_space=pl.ANY)],
        out_specs=pl.BlockSpec((tb, d), lambda i: (i, 0)),
        out_shape=jax.ShapeDtypeStruct((n, d), F32),
        scratch_shapes=[pltpu.VMEM((2, TOP_K, tb, d), F32), pltpu.SemaphoreType.DMA((2,))],
        compiler_params=_params(("arbitrary",)),
        name="combine",
    )(pos3, pos3, route, x1, gate5, final_g, ys)


def _moe(u2, route, x1, w_gu, b_gu, w_down, b_down, gate5, final_g):
    bsz, t_lat, d = x1.shape
    n = bsz * t_lat
    n_experts = w_gu.shape[0]
    tile = EXPERT_TILE
    route2 = route.reshape(n, LANES)
    pos, counts = _route_pos(route2, tile)
    cnt = counts[0, :n_experts].astype(jnp.int32)
    cpad = ((cnt + tile - 1) // tile) * tile
    ends = jnp.cumsum(cpad)
    n_rows = n * TOP_K + n_experts * tile
    n_tiles = n_rows // tile
    tile_expert = jnp.minimum(
        jnp.sum(jnp.arange(n_tiles, dtype=jnp.int32)[:, None] * tile >= ends[None, :], axis=1), n_experts - 1
    ).astype(jnp.int32)
    n_used = (ends[-1:] // tile).astype(jnp.int32)
    zrow = jnp.maximum(ends - tile, 0).astype(jnp.int32)
    zflag = (cnt % tile != 0).astype(jnp.int32)
    xs = _dispatch(u2.reshape(n, d), pos, zrow, zflag, n_used, n_rows, tile)
    ys = _experts(xs, tile_expert, n_used, w_gu, b_gu, w_down, b_down, tile)
    out = _combine(ys, pos, route2, x1.reshape(n, d), gate5, final_g, t_lat)
    return out.reshape(bsz, t_lat, d)


def _grid_sincos(rows, d):
    nf = d // 4
    omega = 1.0 / (POS_BASE ** (jnp.arange(nf, dtype=F32) / nf))
    r = jnp.broadcast_to(jnp.arange(rows, dtype=F32)[:, None, None] * omega, (rows, GRID_W, nf))
    cl = jnp.broadcast_to(jnp.arange(GRID_W, dtype=F32)[None, :, None] * omega, (rows, GRID_W, nf))
    pe = jnp.concatenate([jnp.sin(r), jnp.cos(r), jnp.sin(cl), jnp.cos(cl)], axis=-1)
    return pe.reshape(rows * GRID_W, d)


def _pack_qk(wq, wk):
    lead = wq.shape[:-1]
    q = wq.reshape(lead + (N_HEADS, DK))
    k = wk.reshape(lead + (N_HEADS, DK))
    return jnp.concatenate([q, k], axis=-1).reshape(lead + (QK_W,))


def _row_tile(t):
    for tm in (512, 256, 128):
        if t % tm == 0:
            return tm
    raise ValueError(f"sequence length {t} must be a multiple of 128")


def kernel(x, c, ctx, c_ctx, ada_w, ada_b, norm1_g, w_in, ml_conv_w, ml_conv_b, ml_gate_b, ml_norm_g,
           gla_gate_w2, gla_gate_b, gla_norm_g, w_out, norm2_g, router_w, router_b, moe_w_gu, moe_b_gu,
           moe_w_down, moe_b_down, final_norm_g):
    bsz, t_lat, d = x.shape
    t_ctx = ctx.shape[1]
    assert ada_w.shape[0] == 1, "single-layer block"
    assert t_lat % ML_CHUNK == 0 and t_ctx % ML_CHUNK == 0 and t_lat % GRID_W == 0
    n_experts = router_w.shape[-1]
    assert n_experts <= LANES

    ml_qk, ml_v = N_HEADS * DK, N_HEADS * DV
    sizes = (ml_qk, ml_qk, ml_v, ml_v, 4 * N_HEADS, ml_qk, ml_qk, ml_v, ml_v, 2 * GLA_RANK)
    offs = [int(o) for o in np.cumsum(sizes)[:-1]]
    w_mq, w_mk, w_mv, w_mo, w_mg, w_gq, w_gk, w_gv, w_gg, w_glr = jnp.split(w_in[0], offs, axis=-1)
    w_main = jnp.concatenate([_pack_qk(w_mq, w_mk), w_mv, w_mo, _pack_qk(w_gq, w_gk), w_gv, w_gg],
                             axis=-1).astype(BF16)
    gate_perm = np.concatenate([np.arange(N_HEADS) + g * N_HEADS for g in (0, 2, 1, 3)])
    w_gate = jnp.concatenate([w_mg[:, gate_perm], w_glr,
                              jnp.zeros((d, GATE_W - 4 * N_HEADS - 2 * GLA_RANK), F32)], axis=-1).astype(BF16)
    gate_b = jnp.concatenate([ml_gate_b[0].reshape(-1)[gate_perm],
                              jnp.zeros((GATE_W - 4 * N_HEADS,), F32)]).reshape(1, GATE_W)
    conv_w = _pack_qk(ml_conv_w[0][:, :ml_qk], ml_conv_w[0][:, ml_qk:])
    conv_b = _pack_qk(ml_conv_b[0][:ml_qk], ml_conv_b[0][ml_qk:]).reshape(1, QK_W)
    w2 = gla_gate_w2[0].reshape(2, GLA_RANK, N_HEADS, DK)
    w2 = jnp.concatenate([w2, w2], axis=-1).reshape(2, GLA_RANK, N_HEADS * LANES)
    w2_ext = jnp.zeros((GATE_W, 2 * N_HEADS * LANES), F32)
    for dd in range(2):
        r0 = 4 * N_HEADS + dd * GLA_RANK
        w2_ext = w2_ext.at[r0:r0 + GLA_RANK, dd * N_HEADS * LANES:(dd + 1) * N_HEADS * LANES].set(w2[dd])
    b2 = gla_gate_b[0].reshape(2, N_HEADS, DK)
    b2_ext = jnp.concatenate([b2, b2], axis=-1).reshape(1, 2 * N_HEADS * LANES)
    router_w_p = jnp.concatenate([router_w[0], jnp.zeros((d, LANES - n_experts), F32)], axis=-1)
    router_b_p = jnp.concatenate([router_b[0], jnp.full((LANES - n_experts,), -jnp.inf, F32)]).reshape(1, LANES)

    cond = jnp.concatenate([c, c_ctx[None, :], jnp.zeros(((-bsz - 1) % 8, d), F32)], axis=0)
    mod = _adaln(cond, ada_w[0], ada_b[0]).reshape(cond.shape[0], N_MOD, 1, d)
    m_lat = mod[:bsz]
    m_ctx = mod[bsz:bsz + 1]

    pe = _grid_sincos(t_lat // GRID_W, d)
    g1 = norm1_g[0].reshape(1, d)
    tm = _row_tile(t_lat)
    zm_lat, zg_lat, gt_lat = _inproj(x, pe, m_lat[:, 1], m_lat[:, 0], g1, w_main, w_gate, tm)
    zm_ctx, zg_ctx, gt_ctx = _inproj(ctx, jnp.zeros((t_ctx, d), F32), m_ctx[:, 1], m_ctx[:, 0], g1,
                                     w_main, w_gate, _row_tile(t_ctx))

    a = _mlstm(zm_lat, zm_ctx, gt_lat, gt_ctx, conv_w, conv_b, gate_b, ml_norm_g[0].reshape(1, V_W))
    g = _gla(zg_lat, zg_ctx, gt_lat, gt_ctx, w2_ext, b2_ext, gla_norm_g[0].reshape(1, V_W))

    x1, u2, route = _outproj(x, pe, a, g, w_out[0].astype(BF16), m_lat[:, 2], m_lat[:, 4], m_lat[:, 3],
                             norm2_g[0].reshape(1, d), router_w_p, router_b_p, tm)

    return _moe(u2, route, x1, moe_w_gu[0], moe_b_gu[0][:, None, :], moe_w_down[0],
                moe_b_down[0][:, None, :], m_lat[:, 5], final_norm_g.reshape(1, d))
```

```python
import functools

import numpy as np
import jax
import jax.numpy as jnp
from jax import lax
from jax.experimental import pallas as pl
from jax.experimental.pallas import tpu as pltpu

F32 = jnp.float32
BF16 = jnp.bfloat16
HIGHEST = lax.Precision.HIGHEST

EPS = 1e-6
GRID_W = 64
POS_BASE = 10000.0
N_MOD = 6
N_HEADS = 4
DK = 64
DV = 128
QK_W = N_HEADS * 2 * DK
V_W = N_HEADS * DV
GROUP_W = QK_W + 2 * V_W
GATE_W = 128
ML_CONV = 3
GLA_RANK = 16
GLA_NORMALIZER = 16.0
TOP_K = 4
SWIGLU_LIMIT = 7.0
SWIGLU_ALPHA = 1.702
LANES = 128
ML_CHUNK = 128
GLA_CHUNK = 64
CONV_ROWS = 128
EXPERT_TILE = 512
ROUTE_BLOCK = 512
RUN_ALIGN = 8
RUN_SIZES = tuple(s for s in (ROUTE_BLOCK >> i for i in range(ROUTE_BLOCK.bit_length())) if s >= RUN_ALIGN)
VMEM_LIMIT = 56 * 1024 * 1024


def _params(sem):
    return pltpu.CompilerParams(dimension_semantics=sem, vmem_limit_bytes=VMEM_LIMIT)


def _sigmoid(x):
    return 1.0 / (1.0 + jnp.exp(-x))


def _log_sigmoid(x):
    return jnp.minimum(x, 0.0) - jnp.log1p(jnp.exp(-jnp.abs(x)))


def _dot(a, b, precision=None):
    return jnp.dot(a, b, preferred_element_type=F32, precision=precision)


def _split2(x):
    hi = x.astype(BF16)
    lo = (x - hi.astype(F32)).astype(BF16)
    return hi, lo


def _dot3(a, b):
    a_hi, a_lo = _split2(a)
    b_hi, b_lo = _split2(b)
    return _dot(a_hi, b_hi) + _dot(a_lo, b_hi) + _dot(a_hi, b_lo)


def _dot_nt(a, b):
    return lax.dot_general(a, b, (((1,), (1,)), ((), ())), preferred_element_type=F32)


def _dot_tn(a, b):
    return lax.dot_general(a, b, (((0,), (0,)), ((), ())), preferred_element_type=F32)


def _adaln_body(c_ref, w_ref, b_ref, o_ref):
    c = c_ref[...]
    s = c * _sigmoid(c)
    o_ref[...] = _dot3(s, w_ref[...]) + b_ref[...]


def _adaln(cond, w, b):
    rows, d = cond.shape
    n = w.shape[1]
    tn = 512
    return pl.pallas_call(
        _adaln_body,
        grid=(n // tn,),
        in_specs=[pl.BlockSpec((rows, d), lambda j: (0, 0)),
                  pl.BlockSpec((d, tn), lambda j: (0, j)),
                  pl.BlockSpec((1, tn), lambda j: (0, j))],
        out_specs=pl.BlockSpec((rows, tn), lambda j: (0, j)),
        out_shape=jax.ShapeDtypeStruct((rows, n), F32),
        compiler_params=_params(("arbitrary",)),
        name="adaln",
    )(cond, w, b.reshape(1, n))


def _rms_mod(x, g, scale, shift):
    y = x * lax.rsqrt(jnp.mean(x * x, axis=-1, keepdims=True) + EPS)
    return (y * g) * (1.0 + scale) + shift


def _inproj_body(x_ref, pe_ref, sc_ref, sh_ref, g_ref, wm_ref, wg_ref, zm_ref, zg_ref, gt_ref):
    x = x_ref[0] + pe_ref[...]
    u = _rms_mod(x, g_ref[...], sc_ref[0], sh_ref[0]).astype(BF16)
    z = _dot(u, wm_ref[...])
    zm_ref[0] = z[:, :GROUP_W].astype(BF16)
    zg_ref[0] = z[:, GROUP_W:].astype(BF16)
    gt_ref[0] = _dot(u, wg_ref[...])


def _inproj(x, pe, scale, shift, g, w_main, w_gate, tm):
    bsz, t, d = x.shape
    per_batch = scale.shape[0] == bsz and bsz > 1
    mod_map = (lambda i, b: (b, 0, 0)) if per_batch else (lambda i, b: (0, 0, 0))
    return pl.pallas_call(
        _inproj_body,
        grid=(t // tm, bsz),
        in_specs=[pl.BlockSpec((1, tm, d), lambda i, b: (b, i, 0)),
                  pl.BlockSpec((tm, d), lambda i, b: (i, 0)),
                  pl.BlockSpec((1, 1, d), mod_map),
                  pl.BlockSpec((1, 1, d), mod_map),
                  pl.BlockSpec((1, d), lambda i, b: (0, 0)),
                  pl.BlockSpec((d, 2 * GROUP_W), lambda i, b: (0, 0)),
                  pl.BlockSpec((d, GATE_W), lambda i, b: (0, 0))],
        out_specs=[pl.BlockSpec((1, tm, GROUP_W), lambda i, b: (b, i, 0)),
                   pl.BlockSpec((1, tm, GROUP_W), lambda i, b: (b, i, 0)),
                   pl.BlockSpec((1, tm, GATE_W), lambda i, b: (b, i, 0))],
        out_shape=[jax.ShapeDtypeStruct((bsz, t, GROUP_W), BF16),
                   jax.ShapeDtypeStruct((bsz, t, GROUP_W), BF16),
                   jax.ShapeDtypeStruct((bsz, t, GATE_W), F32)],
        compiler_params=_params(("arbitrary", "arbitrary")),
        name="inproj",
    )(x, pe, scale, shift, g, w_main, w_gate)


def _scan_rows(x, op, ident, reverse, rowi):
    n = x.shape[0]
    k = 1
    while k < n:
        if reverse:
            shifted = jnp.where(rowi < n - k, pltpu.roll(x, n - k, axis=0), ident)
        else:
            shifted = jnp.where(rowi >= k, pltpu.roll(x, k, axis=0), ident)
        x = op(x, shifted)
        k *= 2
    return x


GRP_NEG_M, GRP_ONE, GRP_NEG_B, GRP_MPREV = 0, 2, 4, 6
HEAD_DIRS = 2 * N_HEADS


def _mlstm_body(zl_ref, zc_ref, gl_ref, gc_ref, cw_ref, cb_ref, gb_ref, ng_ref, o_ref,
                q_s, kt_s, acc_s, cc_s, bc_s, st_s, mp_s, a_s, b0_s, wk_s, wc_s, c_s, *, t_lat, t_ctx):
    L = ML_CHUNK
    nc_ctx = t_ctx // L
    nc_lat = t_lat // L
    nc = nc_ctx + nc_lat
    scale = DK ** -0.5

    cw = cw_ref[...]
    cb = cb_ref[...]

    def conv_pass(z_ref, n, dst0):
        nb = n // CONV_ROWS
        row = lax.broadcasted_iota(jnp.int32, (CONV_ROWS, QK_W), 0)
        lane = lax.broadcasted_iota(jnp.int32, (CONV_ROWS, QK_W), 1)
        low = (lane % LANES) < DK

        def body(r, carry):
            r0 = pl.multiple_of(r * CONV_ROWS, CONV_ROWS)
            zc = z_ref[0, pl.ds(r0, CONV_ROWS), 0:QK_W].astype(F32)
            p0 = pl.multiple_of(jnp.maximum(r0 - 16, 0), 16)
            prev = z_ref[0, pl.ds(p0, 16), 0:QK_W].astype(F32)[15:16]
            prev = jnp.where(r > 0, prev, 0.0)
            n0 = pl.multiple_of(jnp.minimum(r0 + CONV_ROWS, n - 16), 16)
            nxt = z_ref[0, pl.ds(n0, 16), 0:QK_W].astype(F32)[0:1]
            nxt = jnp.where(r < nb - 1, nxt, 0.0)
            up = jnp.where(row == 0, prev, pltpu.roll(zc, 1, axis=0))
            dn = jnp.where(row == CONV_ROWS - 1, nxt, pltpu.roll(zc, CONV_ROWS - 1, axis=0))
            y = cw[0:1] * up + cw[1:2] * zc + cw[2:3] * dn + cb
            y = y * _sigmoid(y)
            q = jnp.where(low, y * scale, 0.0)
            k = jnp.where(low, pltpu.roll(y, QK_W - DK, axis=1), 0.0)
            d0 = pl.multiple_of(dst0 + r0, CONV_ROWS)
            q_s[pl.ds(d0, CONV_ROWS), :] = q.astype(BF16)
            for h in range(N_HEADS):
                kt_s[h, :, pl.ds(d0, CONV_ROWS)] = k[:, h * LANES:(h + 1) * LANES].T.astype(BF16)
            return carry

        lax.fori_loop(0, nb, body, 0)

    conv_pass(zc_ref, t_ctx, 0)
    conv_pass(zl_ref, t_lat, t_ctx)

    rowi = lax.broadcasted_iota(jnp.int32, (L, LANES), 0)
    lane_g = lax.broadcasted_iota(jnp.int32, (L, GATE_W), 1)
    fwd_lane = lane_g < N_HEADS
    gb = gb_ref[...]

    def gate_pass(g_ref, nchunks, g0):
        def body(i, carry):
            r0 = pl.multiple_of(i * L, L)
            gates = g_ref[0, pl.ds(r0, L), :] + gb
            lf = pltpu.roll(_log_sigmoid(gates), GATE_W - HEAD_DIRS, axis=1)
            b = jnp.where(fwd_lane, _scan_rows(lf, jnp.add, 0.0, False, rowi),
                          _scan_rows(lf, jnp.add, 0.0, True, rowi))
            c = gates - b
            cc_s[g0 + i] = c
            bc_s[g0 + i] = b
            st_s[g0 + i, 0:1, :] = jnp.max(c, axis=0, keepdims=True)
            st_s[g0 + i, 1:2, :] = jnp.sum(lf, axis=0, keepdims=True)
            return carry

        lax.fori_loop(0, nchunks, body, 0)

    gate_pass(gc_ref, nc_ctx, 0)
    gate_pass(gl_ref, nc_lat, nc_ctx)

    lane1 = lax.broadcasted_iota(jnp.int32, (1, GATE_W), 1)
    fwd_order = list(range(nc))
    bwd_order = list(range(nc_ctx - 1, -1, -1)) + list(range(nc - 1, nc_ctx - 1, -1))
    mp = {}
    for d, order in enumerate((fwd_order, bwd_order)):
        m = jnp.zeros((1, GATE_W), F32)
        for g in order:
            mp[(d, g)] = m
            m = st_s[g, 1:2, :] + jnp.maximum(m, st_s[g, 0:1, :])
    for g in range(nc):
        mp_s[g, 0:1, :] = jnp.where(lane1 < N_HEADS, mp[(0, g)], mp[(1, g)])

    def group_mask(grp):
        return (lane_g >= HEAD_DIRS * grp) & (lane_g < HEAD_DIRS * (grp + 1))

    def ab_body(g, carry):
        c = cc_s[g]
        b = bc_s[g]
        mprev = mp_s[g, 0:1, :]
        mloc = jnp.where(fwd_lane, _scan_rows(c, jnp.maximum, -jnp.inf, False, rowi),
                         _scan_rows(c, jnp.maximum, -jnp.inf, True, rowi))
        m = jnp.maximum(mloc, mprev)
        mlast = jnp.maximum(mprev, st_s[g, 0:1, :])
        a = jnp.where(group_mask(GRP_ONE) | group_mask(GRP_ONE + 1), 1.0, 0.0)
        pieces = ((-m, GRP_NEG_M), (-b, GRP_NEG_B), (jnp.broadcast_to(mprev, (L, GATE_W)), GRP_MPREV))
        for val, grp in pieces:
            for t, part in enumerate(_split2(val)):
                shift = HEAD_DIRS * (grp + t)
                part = part.astype(F32)
                if shift:
                    part = pltpu.roll(part, shift, axis=1)
                a = jnp.where(group_mask(grp + t), part, a)
        a_s[g] = a.astype(BF16)
        cr_hi, cr_lo = _split2(c.T)
        b0 = jnp.concatenate([jnp.ones((2 * HEAD_DIRS, L), F32), cr_hi[0:HEAD_DIRS].astype(F32),
                              cr_lo[0:HEAD_DIRS].astype(F32), jnp.zeros((LANES - 4 * HEAD_DIRS, L), F32)], axis=0)
        b0_s[g] = b0.astype(BF16)
        wk_s[g] = jnp.exp((c - mlast).T[0:HEAD_DIRS])
        wc_s[g] = jnp.exp(jnp.broadcast_to(mprev - mlast, (L, GATE_W)).T[0:HEAD_DIRS])
        return carry

    lax.fori_loop(0, nc, ab_body, 0)

    def row_group(*grps):
        hit = rowi < 0
        for grp in grps:
            hit = hit | ((rowi >= HEAD_DIRS * grp) & (rowi < HEAD_DIRS * (grp + 2)))
        return hit.astype(BF16)

    bconst = jnp.concatenate([row_group(GRP_NEG_M, GRP_MPREV), row_group(GRP_NEG_M, GRP_NEG_B)], axis=1)

    colL = lax.broadcasted_iota(jnp.int32, (L, L), 1)
    rowL = lax.broadcasted_iota(jnp.int32, (L, L), 0)
    masks = (colL <= rowL, colL >= rowL)
    head_lane = [(lane_g % HEAD_DIRS) == j for j in range(HEAD_DIRS)]
    ones_blk = jnp.ones((L, DV), BF16)
    c_s[...] = jnp.zeros(c_s.shape, F32)

    def twice(x):
        return jnp.concatenate([x, x], axis=1)

    def step(g, z_ref, r_src, r_q, d, with_out, r_out):
        a_g = a_s[g]
        for h in range(N_HEADS):
            j = d * N_HEADS + h
            hs = slice(h * DV, (h + 1) * DV)
            kt = kt_s[h, :, pl.ds(r_q, L)]
            v = z_ref[0, pl.ds(r_src, L), QK_W + h * DV:QK_W + (h + 1) * DV]
            vext = jnp.concatenate([v, ones_blk], axis=1)
            state = c_s[j]
            if with_out:
                a_j = jnp.where(head_lane[j], a_g, jnp.zeros_like(a_g))
                ec = _dot(a_j, bconst)
                q = q_s[pl.ds(r_q, L), hs]
                w = jnp.where(masks[d], jnp.exp(_dot(a_j, b0_s[g])), 0.0)
                p = (_dot(q, kt) * w).astype(BF16)
                nd = _dot(p, vext) + twice(jnp.exp(ec[:, 0:LANES])) * _dot(q, state.astype(BF16))
                hh = nd[:, :DV] / jnp.maximum(jnp.abs(nd[:, DV:]), jnp.exp(ec[:, LANES:2 * LANES]))
                acc_s[d, pl.ds(r_out, L), hs] = hh
            ktw = (kt.astype(F32) * wk_s[g, j:j + 1, :]).astype(BF16)
            c_s[j] = twice(wc_s[g, j:j + 1, :]) * state + _dot(ktw, vext)

    def ctx_body(i, carry):
        for d in (0, 1):
            ii = i if d == 0 else nc_ctx - 1 - i
            r0 = pl.multiple_of(ii * L, L)
            step(ii, zc_ref, r0, r0, d, False, None)
        return carry

    def lat_body(i, carry):
        for d in (0, 1):
            ii = i if d == 0 else nc_lat - 1 - i
            r0 = pl.multiple_of(ii * L, L)
            step(nc_ctx + ii, zl_ref, r0, pl.multiple_of(t_ctx + ii * L, L), d, True, r0)
        return carry

    lax.fori_loop(0, nc_ctx, ctx_body, 0)
    lax.fori_loop(0, nc_lat, lat_body, 0)

    def out_body(i, carry):
        r0 = pl.multiple_of(i * L, L)
        for h in range(N_HEADS):
            hs = slice(h * DV, (h + 1) * DV)
            tot = acc_s[0, pl.ds(r0, L), hs] + acc_s[1, pl.ds(r0, L), hs]
            tot = tot * lax.rsqrt(jnp.mean(tot * tot, axis=-1, keepdims=True) + EPS)
            og = zl_ref[0, pl.ds(r0, L), QK_W + V_W + h * DV:QK_W + V_W + (h + 1) * DV].astype(F32)
            o_ref[0, pl.ds(r0, L), hs] = (tot * ng_ref[:, hs] * _sigmoid(og)).astype(BF16)
        return carry

    lax.fori_loop(0, nc_lat, out_body, 0)


def _mlstm(zm_lat, zm_ctx, gt_lat, gt_ctx, conv_w, conv_b, gate_b, norm_g):
    bsz, t_lat, _ = zm_lat.shape
    t_ctx = zm_ctx.shape[1]
    nc = (t_lat + t_ctx) // ML_CHUNK
    body = functools.partial(_mlstm_body, t_lat=t_lat, t_ctx=t_ctx)
    full = lambda b: (0, 0)
    return pl.pallas_call(
        body,
        grid=(bsz,),
        in_specs=[pl.BlockSpec((1, t_lat, GROUP_W), lambda b: (b, 0, 0)),
                  pl.BlockSpec((1, t_ctx, GROUP_W), lambda b: (b, 0, 0)),
                  pl.BlockSpec((1, t_lat, GATE_W), lambda b: (b, 0, 0)),
                  pl.BlockSpec((1, t_ctx, GATE_W), lambda b: (b, 0, 0)),
                  pl.BlockSpec((ML_CONV, QK_W), full),
                  pl.BlockSpec((1, QK_W), full),
                  pl.BlockSpec((1, GATE_W), full),
                  pl.BlockSpec((1, V_W), full)],
        out_specs=pl.BlockSpec((1, t_lat, V_W), lambda b: (b, 0, 0)),
        out_shape=jax.ShapeDtypeStruct((bsz, t_lat, V_W), BF16),
        scratch_shapes=[pltpu.VMEM((t_ctx + t_lat, QK_W), BF16),
                        pltpu.VMEM((N_HEADS, LANES, t_ctx + t_lat), BF16),
                        pltpu.VMEM((2, t_lat, V_W), F32),
                        pltpu.VMEM((nc, ML_CHUNK, GATE_W), F32),
                        pltpu.VMEM((nc, ML_CHUNK, GATE_W), F32),
                        pltpu.VMEM((nc, 8, GATE_W), F32),
                        pltpu.VMEM((nc, 8, GATE_W), F32),
                        pltpu.VMEM((nc, ML_CHUNK, GATE_W), BF16),
                        pltpu.VMEM((nc, LANES, ML_CHUNK), BF16),
                        pltpu.VMEM((nc, HEAD_DIRS, ML_CHUNK), F32),
                        pltpu.VMEM((nc, HEAD_DIRS, GATE_W), F32),
                        pltpu.VMEM((HEAD_DIRS, 2 * DK, 2 * DV), F32)],
        compiler_params=_params(("arbitrary",)),
        name="mlstm",
    )(zm_lat, zm_ctx, gt_lat, gt_ctx, conv_w, conv_b, gate_b, norm_g)


def _gla_body(zl_ref, zc_ref, gl_ref, gc_ref, w2_ref, b2_ref, ng_ref, o_ref,
              acc_s, s_s, qt_s, kt_s, ee_s, *, t_lat, t_ctx):
    L = GLA_CHUNK
    nc_ctx = t_ctx // L
    nc_lat = t_lat // L
    scale = DK ** -0.5
    half = N_HEADS * LANES

    rowi = lax.broadcasted_iota(jnp.int32, (L, L), 0)
    coli = lax.broadcasted_iota(jnp.int32, (L, L), 1)
    masks = (coli <= rowi, coli >= rowi)
    tri_b = (masks[0].astype(BF16), masks[1].astype(BF16))
    low = lax.broadcasted_iota(jnp.int32, (L, LANES), 1) < DK
    w2_hi, w2_lo = _split2(w2_ref[...])
    bias2 = b2_ref[...]

    def prep(z_ref, g_ref, nchunks, c0, row0):
        def body(i, carry):
            r0 = pl.multiple_of(i * L, L)
            rq = pl.multiple_of(row0 + i * L, L)
            g_hi, g_lo = _split2(g_ref[0, pl.ds(r0, L), :])
            pre = _dot(g_hi, w2_hi) + _dot(g_lo, w2_hi) + _dot(g_hi, w2_lo) + bias2
            la_hi, la_lo = _split2(_log_sigmoid(pre) * (1.0 / GLA_NORMALIZER))
            for d in (0, 1):
                ds_ = slice(d * half, (d + 1) * half)
                b_all = _dot(tri_b[d], la_hi[:, ds_]) + _dot(tri_b[d], la_lo[:, ds_])
                for h in range(N_HEADS):
                    hs = slice(h * LANES, (h + 1) * LANES)
                    b2 = b_all[:, hs]
                    b_end = b2[L - 1:L] if d == 0 else b2[0:1]
                    qk = z_ref[0, pl.ds(r0, L), hs].astype(F32)
                    qkt = qk * jnp.exp(jnp.where(low, b2, -b2))
                    qt_s[d, pl.ds(rq, L), hs] = jnp.where(low, qkt * scale, 0.0).astype(BF16)
                    kt_s[d, pl.ds(rq, L), hs] = jnp.where(low, pltpu.roll(qkt, DK, axis=1), 0.0).astype(BF16)
                    ee_s[d, c0 + i, 0:1, hs] = jnp.exp(b_end)
            return carry

        lax.fori_loop(0, nchunks, body, 0, unroll=2)

    prep(zc_ref, gc_ref, nc_ctx, 0, 0)
    prep(zl_ref, gl_ref, nc_lat, nc_ctx, t_ctx)

    s_s[...] = jnp.zeros(s_s.shape, F32)

    def step(z_ref, r0, rq, c, d, with_out):
        for h in range(N_HEADS):
            j = d * N_HEADS + h
            hs = slice(h * LANES, (h + 1) * LANES)
            qt = qt_s[d, pl.ds(rq, L), hs]
            kt = kt_s[d, pl.ds(rq, L), hs]
            v = z_ref[0, pl.ds(r0, L), QK_W + h * DV:QK_W + (h + 1) * DV]
            state = s_s[j]
            if with_out:
                a = jnp.where(masks[d], _dot_nt(qt, kt), 0.0).astype(BF16)
                acc_s[d, pl.ds(r0, L), hs] = _dot(a, v) + _dot_nt(qt, state.astype(BF16))
            e_end = ee_s[d, c, 0:1, hs]
            s_s[j] = state * e_end + _dot_tn(v, (kt.astype(F32) * e_end).astype(BF16))

    def loop(z_ref, n, c0, row0, with_out):
        def body(i, carry):
            for d in (0, 1):
                ii = i if d == 0 else n - 1 - i
                r0 = pl.multiple_of(ii * L, L)
                step(z_ref, r0, pl.multiple_of(row0 + ii * L, L), c0 + ii, d, with_out)
            return carry
        lax.fori_loop(0, n, body, 0, unroll=2)

    loop(zc_ref, nc_ctx, 0, 0, False)
    loop(zl_ref, nc_lat, nc_ctx, t_ctx, True)

    def out_body(i, carry):
        r0 = pl.multiple_of(i * L, L)
        for h in range(N_HEADS):
            hs = slice(h * LANES, (h + 1) * LANES)
            tot = acc_s[0, pl.ds(r0, L), hs] + acc_s[1, pl.ds(r0, L), hs]
            tot = tot * lax.rsqrt(jnp.mean(tot * tot, axis=-1, keepdims=True) + EPS)
            gg = zl_ref[0, pl.ds(r0, L), QK_W + V_W + h * DV:QK_W + V_W + (h + 1) * DV].astype(F32)
            o_ref[0, pl.ds(r0, L), hs] = (tot * ng_ref[:, hs] * (gg * _sigmoid(gg))).astype(BF16)
        return carry

    lax.fori_loop(0, nc_lat, out_body, 0)


def _gla(zg_lat, zg_ctx, gt_lat, gt_ctx, w2_ext, b2_ext, norm_g):
    bsz, t_lat, _ = zg_lat.shape
    t_ctx = zg_ctx.shape[1]
    body = functools.partial(_gla_body, t_lat=t_lat, t_ctx=t_ctx)
    full = lambda b: (0, 0)
    return pl.pallas_call(
        body,
        grid=(bsz,),
        in_specs=[pl.BlockSpec((1, t_lat, GROUP_W), lambda b: (b, 0, 0)),
                  pl.BlockSpec((1, t_ctx, GROUP_W), lambda b: (b, 0, 0)),
                  pl.BlockSpec((1, t_lat, GATE_W), lambda b: (b, 0, 0)),
                  pl.BlockSpec((1, t_ctx, GATE_W), lambda b: (b, 0, 0)),
                  pl.BlockSpec((GATE_W, 2 * N_HEADS * LANES), full),
                  pl.BlockSpec((1, 2 * N_HEADS * LANES), full),
                  pl.BlockSpec((1, V_W), full)],
        out_specs=pl.BlockSpec((1, t_lat, V_W), lambda b: (b, 0, 0)),
        out_shape=jax.ShapeDtypeStruct((bsz, t_lat, V_W), BF16),
        scratch_shapes=[pltpu.VMEM((2, t_lat, V_W), F32),
                        pltpu.VMEM((HEAD_DIRS, DV, LANES), F32),
                        pltpu.VMEM((2, t_ctx + t_lat, QK_W), BF16),
                        pltpu.VMEM((2, t_ctx + t_lat, QK_W), BF16),
                        pltpu.VMEM((2, (t_ctx + t_lat) // GLA_CHUNK, 8, QK_W), F32)],
        compiler_params=_params(("arbitrary",)),
        name="gla",
    )(zg_lat, zg_ctx, gt_lat, gt_ctx, w2_ext, b2_ext, norm_g)


def _outproj_body(x_ref, pe_ref, a_ref, g_ref, wo_ref, g1_ref, sc_ref, sh_ref, n2_ref, rw_ref, rb_ref,
                  x1_ref, u2_ref, route_ref):
    x = x_ref[0] + pe_ref[...]
    y = _dot(a_ref[0], wo_ref[0:V_W, :]) + _dot(g_ref[0], wo_ref[V_W:2 * V_W, :])
    x1 = x + g1_ref[0] * y
    x1_ref[0] = x1
    u2 = _rms_mod(x1, n2_ref[...], sc_ref[0], sh_ref[0])
    u2_ref[0] = u2
    logits = _dot3(u2, rw_ref[...]) + rb_ref[...]
    lane = lax.broadcasted_iota(jnp.int32, logits.shape, 1).astype(F32)
    ids = []
    vals = []
    for _ in range(TOP_K):
        m = jnp.max(logits, axis=1, keepdims=True)
        idx = jnp.min(jnp.where(logits == m, lane, float(LANES)), axis=1, keepdims=True)
        ids.append(idx)
        vals.append(m)
        logits = jnp.where(lane == idx, -jnp.inf, logits)
    exps = [jnp.exp(v - vals[0]) for v in vals]
    denom = exps[0]
    for e in exps[1:]:
        denom = denom + e
    route = jnp.zeros(logits.shape, F32)
    for k in range(TOP_K):
        route = jnp.where(lane == float(k), ids[k], route)
        route = jnp.where(lane == float(TOP_K + k), exps[k] / denom, route)
    route_ref[0] = route


def _outproj(x, pe, a, g, w_out, gate1, scale2, shift2, norm2_g, router_w, router_b, tm):
    bsz, t, d = x.shape
    tok = lambda i, b: (b, i, 0)
    mod = lambda i, b: (b, 0, 0)
    full = lambda i, b: (0, 0)
    return pl.pallas_call(
        _outproj_body,
        grid=(t // tm, bsz),
        in_specs=[pl.BlockSpec((1, tm, d), tok),
                  pl.BlockSpec((tm, d), lambda i, b: (i, 0)),
                  pl.BlockSpec((1, tm, V_W), tok),
                  pl.BlockSpec((1, tm, V_W), tok),
                  pl.BlockSpec((2 * V_W, d), full),
                  pl.BlockSpec((1, 1, d), mod),
                  pl.BlockSpec((1, 1, d), mod),
                  pl.BlockSpec((1, 1, d), mod),
                  pl.BlockSpec((1, d), full),
                  pl.BlockSpec((d, LANES), full),
                  pl.BlockSpec((1, LANES), full)],
        out_specs=[pl.BlockSpec((1, tm, d), tok),
                   pl.BlockSpec((1, tm, d), tok),
                   pl.BlockSpec((1, tm, LANES), tok)],
        out_shape=[jax.ShapeDtypeStruct((bsz, t, d), F32),
                   jax.ShapeDtypeStruct((bsz, t, d), F32),
                   jax.ShapeDtypeStruct((bsz, t, LANES), F32)],
        compiler_params=_params(("arbitrary", "arbitrary")),
        name="outproj_router",
    )(x, pe, a, g, w_out, gate1, scale2, shift2, norm2_g, router_w, router_b)


def _lane_excl_cumsum(x):
    x8 = jnp.broadcast_to(x, (8, LANES))
    lane8 = lax.broadcasted_iota(jnp.int32, (8, LANES), 1)
    inc = x8
    s = 1
    while s < LANES:
        inc = inc + jnp.where(lane8 >= s, pltpu.roll(inc, s, axis=1), 0.0)
        s *= 2
    return (inc - x8)[0:1]


def _route_pos_body(route_ref, lic_ref, lit_ref, tab_ref, cnt_ref, run_s, off_s, *, tile):
    p = pl.program_id(0)
    j = pl.program_id(1)
    nb = pl.num_programs(1)
    tb = route_ref.shape[0]
    route = route_ref[...]
    lane = lax.broadcasted_iota(jnp.int32, (tb, LANES), 1).astype(F32)
    onehots = [lane == route[:, k:k + 1] for k in range(TOP_K)]
    oh = onehots[0].astype(F32)
    for o in onehots[1:]:
        oh = oh + o.astype(F32)
    col_counts = jnp.ceil(jnp.sum(oh, axis=0, keepdims=True) * (1.0 / RUN_ALIGN)) * float(RUN_ALIGN)

    @pl.when((p == 0) & (j == 0))
    def _():
        run_s[...] = jnp.zeros(run_s.shape, F32)

    @pl.when(p == 0)
    def _():
        run_s[...] += col_counts

    @pl.when((p == 0) & (j == nb - 1))
    def _():
        cnt_ref[...] = run_s[...]

    @pl.when((p == 1) & (j == 0))
    def _():
        cpad = jnp.floor((run_s[...] + float(tile - 1)) * (1.0 / tile)) * float(tile)
        off_s[...] = _lane_excl_cumsum(cpad)

    @pl.when(p == 1)
    def _():
        r = lax.broadcasted_iota(jnp.int32, (tb, tb), 0)
        c = lax.broadcasted_iota(jnp.int32, (tb, tb), 1)
        before = _dot((c < r).astype(BF16), oh.astype(BF16))
        lstart = _lane_excl_cumsum(col_counts)
        local = lstart + before
        li = jnp.zeros((tb, LANES), F32)
        for k in range(TOP_K):
            lk = jnp.sum(jnp.where(onehots[k], local, 0.0), axis=1, keepdims=True)
            li = jnp.where(lane == float(k), lk, li)
        lic_ref[...] = li
        lit_ref[0] = li.T[0:8]
        row8 = lax.broadcasted_iota(jnp.int32, (8, LANES), 0)
        total = jnp.sum(col_counts, axis=1, keepdims=True)
        tab = jnp.where(row8 == 0, off_s[...],
                        jnp.where(row8 == 1, col_counts, jnp.where(row8 == 2, lstart, jnp.where(row8 == 3, total, 0.0))))
        tab_ref[0] = tab.astype(jnp.int32)
        off_s[...] += col_counts


def _route_pos(route, tile):
    n = route.shape[0]
    tb = ROUTE_BLOCK
    nb = n // tb
    return pl.pallas_call(
        functools.partial(_route_pos_body, tile=tile),
        grid=(2, nb),
        in_specs=[pl.BlockSpec((tb, LANES), lambda p, j: (j, 0))],
        out_specs=[pl.BlockSpec((tb, LANES), lambda p, j: (p * j, 0)),
                   pl.BlockSpec((1, 8, tb), lambda p, j: (p * j, 0, 0)),
                   pl.BlockSpec((1, 8, LANES), lambda p, j: (p * j, 0, 0)),
                   pl.BlockSpec((1, LANES), lambda p, j: (0, 0))],
        out_shape=[jax.ShapeDtypeStruct((n, LANES), F32),
                   jax.ShapeDtypeStruct((nb, 8, tb), F32),
                   jax.ShapeDtypeStruct((nb, 8, LANES), jnp.int32),
                   jax.ShapeDtypeStruct((1, LANES), F32)],
        scratch_shapes=[pltpu.VMEM((1, LANES), F32), pltpu.VMEM((1, LANES), F32)],
        compiler_params=_params(("arbitrary", "arbitrary")),
        name="route_pos",
    )(route)


def _issue_runs(tab_ref, n_experts, make_copy):
    def per_expert(e, carry):
        g = tab_ref[0, 0, e]
        cnt = tab_ref[0, 1, e]
        l = tab_ref[0, 2, e]
        for size in RUN_SIZES:
            part = cnt & size

            @pl.when(part != 0)
            def _():
                make_copy(pl.multiple_of(l, RUN_ALIGN), pl.multiple_of(g, RUN_ALIGN), size).start()

            l = l + part
            g = g + part
        return carry

    lax.fori_loop(0, n_experts, per_expert, 0)


def _dispatch_body(zrow_ref, zflag_ref, nu_ref, tab_ref, lit_ref, u_ref, xs_ref, buf, zero_s, tot_s, sem, zsem,
                   *, n_experts, tile):
    i = pl.program_id(0)
    nb = pl.num_programs(0)
    tb = u_ref.shape[0]
    rows = buf.shape[1]
    n_tiles = xs_ref.shape[0] // tile
    slot = i % 2

    @pl.when(i == 0)
    def _():
        zero_s[...] = jnp.zeros(zero_s.shape, F32)

        def zero_tile(r0):
            cp = pltpu.make_async_copy(zero_s, xs_ref.at[pl.ds(pl.multiple_of(r0, tile), tile)], zsem)
            cp.start()
            cp.wait()

        for e in range(n_experts):
            @pl.when(zflag_ref[e] == 1)
            def _():
                zero_tile(zrow_ref[e])

        def tail(t, carry):
            zero_tile(t * tile)
            return carry

        lax.fori_loop(nu_ref[0], n_tiles, tail, 0)

    def wait_slot(s):
        nrows = pl.multiple_of(tot_s[s], RUN_ALIGN)
        pltpu.make_async_copy(buf.at[s, pl.ds(0, nrows)], xs_ref.at[pl.ds(0, nrows)], sem.at[s]).wait()

    @pl.when(i >= 2)
    def _():
        wait_slot(slot)

    tot_s[slot] = tab_ref[0, 3, 0]

    lit = lit_ref[0]
    ri = lax.broadcasted_iota(jnp.int32, (rows, tb), 0).astype(F32)
    hit = ri == lit[0:1]
    for k in range(1, TOP_K):
        hit = hit | (ri == lit[k:k + 1])
    buf[slot] = _dot(hit.astype(BF16), u_ref[...].astype(BF16))
    _issue_runs(tab_ref, n_experts,
                lambda l, g, size: pltpu.make_async_copy(buf.at[slot, pl.ds(l, size)], xs_ref.at[pl.ds(g, size)],
                                                         sem.at[slot]))

    @pl.when(i == nb - 1)
    def _():
        wait_slot(slot)

    @pl.when((i == nb - 1) & (nb > 1))
    def _():
        wait_slot(1 - slot)


def _dispatch(u2, tab, lit, zrow, zflag, n_used, n_rows, tile):
    n, d = u2.shape
    tb = ROUTE_BLOCK
    nb = n // tb
    n_experts = zrow.shape[0]
    grid_spec = pltpu.PrefetchScalarGridSpec(
        num_scalar_prefetch=3,
        grid=(nb,),
        in_specs=[pl.BlockSpec((1, 8, LANES), lambda i, zr, zf, nu: (i, 0, 0), memory_space=pltpu.SMEM),
                  pl.BlockSpec((1, 8, tb), lambda i, zr, zf, nu: (i, 0, 0)),
                  pl.BlockSpec((tb, d), lambda i, zr, zf, nu: (i, 0))],
        out_specs=pl.BlockSpec(memory_space=pl.ANY),
        scratch_shapes=[pltpu.VMEM((2, tb * TOP_K + n_experts * RUN_ALIGN, d), F32), pltpu.VMEM((tile, d), F32),
                        pltpu.SMEM((2,), jnp.int32), pltpu.SemaphoreType.DMA((2,)), pltpu.SemaphoreType.DMA(())],
    )
    return pl.pallas_call(
        functools.partial(_dispatch_body, n_experts=n_experts, tile=tile),
        grid_spec=grid_spec,
        out_shape=jax.ShapeDtypeStruct((n_rows, d), F32),
        compiler_params=_params(("arbitrary",)),
        name="dispatch",
    )(zrow, zflag, n_used, tab, lit, u2)


def _experts_body(te_ref, nu_ref, x_ref, wgu_ref, bgu_ref, wd_ref, bd_ref, y_ref, wgu_s, wd_s, *, d_expert):
    t = pl.program_id(0)

    @pl.when((t == 0) | (te_ref[t] != te_ref[jnp.maximum(t - 1, 0)]))
    def _():
        wgu_s[...] = wgu_ref[0].astype(BF16)
        wd_s[...] = wd_ref[0].astype(BF16)

    @pl.when(t < nu_ref[0])
    def _():
        gu = _dot(x_ref[...].astype(BF16), wgu_s[...]) + bgu_ref[0]
        gate = jnp.minimum(gu[:, :d_expert], SWIGLU_LIMIT)
        up = jnp.clip(gu[:, d_expert:], -SWIGLU_LIMIT, SWIGLU_LIMIT)
        act = ((up + 1.0) * (gate * _sigmoid(SWIGLU_ALPHA * gate))).astype(BF16)
        y_ref[...] = _dot(act, wd_s[...]) + bd_ref[0]

    @pl.when(t >= nu_ref[0])
    def _():
        y_ref[...] = jnp.zeros(y_ref.shape, F32)


def _experts(xs, tile_expert, n_used, w_gu, b_gu, w_down, b_down, tile):
    n_rows, d = xs.shape
    n_experts, _, two_de = w_gu.shape
    d_expert = two_de // 2
    grid_spec = pltpu.PrefetchScalarGridSpec(
        num_scalar_prefetch=2,
        grid=(n_rows // tile,),
        in_specs=[pl.BlockSpec((tile, d), lambda t, te, nu: (jnp.minimum(t, nu[0] - 1), 0)),
                  pl.BlockSpec((1, d, two_de), lambda t, te, nu: (te[t], 0, 0)),
                  pl.BlockSpec((1, 1, two_de), lambda t, te, nu: (te[t], 0, 0)),
                  pl.BlockSpec((1, d_expert, d), lambda t, te, nu: (te[t], 0, 0)),
                  pl.BlockSpec((1, 1, d), lambda t, te, nu: (te[t], 0, 0))],
        out_specs=pl.BlockSpec((tile, d), lambda t, te, nu: (t, 0)),
        scratch_shapes=[pltpu.VMEM((d, two_de), BF16), pltpu.VMEM((d_expert, d), BF16)],
    )
    return pl.pallas_call(
        functools.partial(_experts_body, d_expert=d_expert),
        grid_spec=grid_spec,
        out_shape=jax.ShapeDtypeStruct((n_rows, d), F32),
        compiler_params=_params(("arbitrary",)),
        name="experts",
    )(tile_expert, n_used, xs, w_gu, b_gu, w_down, b_down)


def _combine_body(tab_ref, tabn_ref, lic_ref, route_ref, x1_ref, g5_ref, fg_ref, ys_ref, o_ref, buf, sem,
                  *, n_experts):
    i = pl.program_id(0)
    n = pl.num_programs(0)
    tb = x1_ref.shape[0]
    rows = buf.shape[1]
    slot = i % 2

    def issue(t_ref, s):
        _issue_runs(t_ref, n_experts,
                    lambda l, g, size: pltpu.make_async_copy(ys_ref.at[pl.ds(g, size)], buf.at[s, pl.ds(l, size)],
                                                             sem.at[s]))

    @pl.when(i == 0)
    def _():
        buf[...] = jnp.zeros(buf.shape, F32)
        issue(tab_ref, 0)

    @pl.when(i + 1 < n)
    def _():
        issue(tabn_ref, 1 - slot)

    nrows = pl.multiple_of(tab_ref[0, 3, 0], RUN_ALIGN)
    pltpu.make_async_copy(ys_ref.at[pl.ds(0, nrows)], buf.at[slot, pl.ds(0, nrows)], sem.at[slot]).wait()

    lane = lax.broadcasted_iota(jnp.int32, (tb, LANES), 1)
    li_hi, li_lo = _split2(lic_ref[...])
    a = jnp.where(lane < TOP_K, li_hi.astype(F32),
                  jnp.where(lane < 2 * TOP_K, pltpu.roll(li_lo.astype(F32), TOP_K, axis=1),
                            jnp.where(lane < 3 * TOP_K, pltpu.roll(route_ref[...], TOP_K, axis=1), 0.0))).astype(BF16)
    rsel = lax.broadcasted_iota(jnp.int32, (LANES, LANES), 0)
    lane_f = lane.astype(F32)
    idx, gate = [], []
    for k in range(TOP_K):
        sel = jnp.concatenate([((rsel == k) | (rsel == TOP_K + k)).astype(BF16),
                               (rsel == 2 * TOP_K + k).astype(BF16)], axis=1)
        bk = _dot(a, sel)
        idx.append(bk[:, :LANES])
        gate.append(bk[:, LANES:])
    blocks = []
    for c in range(rows // LANES):
        col = lane_f + float(c * LANES)
        w = jnp.where(idx[0] == col, gate[0], 0.0)
        for k in range(1, TOP_K):
            w = w + jnp.where(idx[k] == col, gate[k], 0.0)
        blocks.append(w.astype(BF16))
    moe = _dot(jnp.concatenate(blocks, axis=1), buf[slot].astype(BF16))
    xo = x1_ref[...] + g5_ref[0] * moe
    o_ref[...] = xo * lax.rsqrt(jnp.mean(xo * xo, axis=-1, keepdims=True) + EPS) * fg_ref[...]


def _combine(ys, tab, lic, route, x1, gate5, final_g, t_lat, n_experts):
    n, d = x1.shape
    tb = ROUTE_BLOCK
    nb = n // tb
    per_b = t_lat // tb
    return pl.pallas_call(
        functools.partial(_combine_body, n_experts=n_experts),
        grid=(nb,),
        in_specs=[pl.BlockSpec((1, 8, LANES), lambda i: (i, 0, 0), memory_space=pltpu.SMEM),
                  pl.BlockSpec((1, 8, LANES), lambda i: (jnp.minimum(i + 1, nb - 1), 0, 0),
                               memory_space=pltpu.SMEM),
                  pl.BlockSpec((tb, LANES), lambda i: (i, 0)),
                  pl.BlockSpec((tb, LANES), lambda i: (i, 0)),
                  pl.BlockSpec((tb, d), lambda i: (i, 0)),
                  pl.BlockSpec((1, 1, d), lambda i: (i // per_b, 0, 0)),
                  pl.BlockSpec((1, d), lambda i: (0, 0)),
                  pl.BlockSpec(memory_space=pl.ANY)],
        out_specs=pl.BlockSpec((tb, d), lambda i: (i, 0)),
        out_shape=jax.ShapeDtypeStruct((n, d), F32),
        scratch_shapes=[pltpu.VMEM((2, tb * TOP_K + n_experts * RUN_ALIGN, d), F32), pltpu.SemaphoreType.DMA((2,))],
        compiler_params=_params(("arbitrary",)),
        name="combine",
    )(tab, tab, lic, route, x1, gate5, final_g, ys)


def _moe(u2, route, x1, w_gu, b_gu, w_down, b_down, gate5, final_g):
    bsz, t_lat, d = x1.shape
    n = bsz * t_lat
    n_experts = w_gu.shape[0]
    tile = EXPERT_TILE
    route2 = route.reshape(n, LANES)
    lic, lit, tab, counts = _route_pos(route2, tile)
    cnt = counts[0, :n_experts].astype(jnp.int32)
    cpad = ((cnt + tile - 1) // tile) * tile
    ends = jnp.cumsum(cpad)
    n_tiles = -(-(n * TOP_K + (n // ROUTE_BLOCK) * n_experts * RUN_ALIGN) // tile) + n_experts
    n_rows = n_tiles * tile
    tile_expert = jnp.minimum(
        jnp.sum(jnp.arange(n_tiles, dtype=jnp.int32)[:, None] * tile >= ends[None, :], axis=1), n_experts - 1
    ).astype(jnp.int32)
    n_used = (ends[-1:] // tile).astype(jnp.int32)
    zrow = jnp.maximum(ends - tile, 0).astype(jnp.int32)
    zflag = (cnt % tile != 0).astype(jnp.int32)
    xs = _dispatch(u2.reshape(n, d), tab, lit, zrow, zflag, n_used, n_rows, tile)
    ys = _experts(xs, tile_expert, n_used, w_gu, b_gu, w_down, b_down, tile)
    out = _combine(ys, tab, lic, route2, x1.reshape(n, d), gate5, final_g, t_lat, n_experts)
    return out.reshape(bsz, t_lat, d)


def _grid_sincos(rows, d):
    nf = d // 4
    omega = 1.0 / (POS_BASE ** (jnp.arange(nf, dtype=F32) / nf))
    r = jnp.broadcast_to(jnp.arange(rows, dtype=F32)[:, None, None] * omega, (rows, GRID_W, nf))
    cl = jnp.broadcast_to(jnp.arange(GRID_W, dtype=F32)[None, :, None] * omega, (rows, GRID_W, nf))
    pe = jnp.concatenate([jnp.sin(r), jnp.cos(r), jnp.sin(cl), jnp.cos(cl)], axis=-1)
    return pe.reshape(rows * GRID_W, d)


def _pack_qk(wq, wk):
    lead = wq.shape[:-1]
    q = wq.reshape(lead + (N_HEADS, DK))
    k = wk.reshape(lead + (N_HEADS, DK))
    return jnp.concatenate([q, k], axis=-1).reshape(lead + (QK_W,))


def _row_tile(t):
    for tm in (512, 256, 128):
        if t % tm == 0:
            return tm
    raise ValueError(f"sequence length {t} must be a multiple of 128")


def kernel(x, c, ctx, c_ctx, ada_w, ada_b, norm1_g, w_in, ml_conv_w, ml_conv_b, ml_gate_b, ml_norm_g,
           gla_gate_w2, gla_gate_b, gla_norm_g, w_out, norm2_g, router_w, router_b, moe_w_gu, moe_b_gu,
           moe_w_down, moe_b_down, final_norm_g):
    bsz, t_lat, d = x.shape
    t_ctx = ctx.shape[1]
    assert ada_w.shape[0] == 1, "single-layer block"
    assert t_lat % ML_CHUNK == 0 and t_ctx % ML_CHUNK == 0 and t_lat % GRID_W == 0 and t_lat % ROUTE_BLOCK == 0
    n_experts = router_w.shape[-1]
    assert n_experts <= LANES

    ml_qk, ml_v = N_HEADS * DK, N_HEADS * DV
    sizes = (ml_qk, ml_qk, ml_v, ml_v, 4 * N_HEADS, ml_qk, ml_qk, ml_v, ml_v, 2 * GLA_RANK)
    offs = [int(o) for o in np.cumsum(sizes)[:-1]]
    w_mq, w_mk, w_mv, w_mo, w_mg, w_gq, w_gk, w_gv, w_gg, w_glr = jnp.split(w_in[0], offs, axis=-1)
    w_main = jnp.concatenate([_pack_qk(w_mq, w_mk), w_mv, w_mo, _pack_qk(w_gq, w_gk), w_gv, w_gg],
                             axis=-1).astype(BF16)
    gate_perm = np.concatenate([np.arange(N_HEADS) + g * N_HEADS for g in (0, 2, 1, 3)])
    w_gate = jnp.concatenate([w_mg[:, gate_perm], w_glr,
                              jnp.zeros((d, GATE_W - 4 * N_HEADS - 2 * GLA_RANK), F32)], axis=-1).astype(BF16)
    gate_b = jnp.concatenate([ml_gate_b[0].reshape(-1)[gate_perm],
                              jnp.zeros((GATE_W - 4 * N_HEADS,), F32)]).reshape(1, GATE_W)
    conv_w = _pack_qk(ml_conv_w[0][:, :ml_qk], ml_conv_w[0][:, ml_qk:])
    conv_b = _pack_qk(ml_conv_b[0][:ml_qk], ml_conv_b[0][ml_qk:]).reshape(1, QK_W)
    w2 = gla_gate_w2[0].reshape(2, GLA_RANK, N_HEADS, DK)
    w2 = jnp.concatenate([w2, w2], axis=-1).reshape(2, GLA_RANK, N_HEADS * LANES)
    w2_ext = jnp.zeros((GATE_W, 2 * N_HEADS * LANES), F32)
    for dd in range(2):
        r0 = 4 * N_HEADS + dd * GLA_RANK
        w2_ext = w2_ext.at[r0:r0 + GLA_RANK, dd * N_HEADS * LANES:(dd + 1) * N_HEADS * LANES].set(w2[dd])
    b2 = gla_gate_b[0].reshape(2, N_HEADS, DK)
    b2_ext = jnp.concatenate([b2, b2], axis=-1).reshape(1, 2 * N_HEADS * LANES)
    router_w_p = jnp.concatenate([router_w[0], jnp.zeros((d, LANES - n_experts), F32)], axis=-1)
    router_b_p = jnp.concatenate([router_b[0], jnp.full((LANES - n_experts,), -jnp.inf, F32)]).reshape(1, LANES)

    cond = jnp.concatenate([c, c_ctx[None, :], jnp.zeros(((-bsz - 1) % 8, d), F32)], axis=0)
    mod = _adaln(cond, ada_w[0], ada_b[0]).reshape(cond.shape[0], N_MOD, 1, d)
    m_lat = mod[:bsz]
    m_ctx = mod[bsz:bsz + 1]

    pe = _grid_sincos(t_lat // GRID_W, d)
    g1 = norm1_g[0].reshape(1, d)
    tm = _row_tile(t_lat)
    zm_lat, zg_lat, gt_lat = _inproj(x, pe, m_lat[:, 1], m_lat[:, 0], g1, w_main, w_gate, tm)
    zm_ctx, zg_ctx, gt_ctx = _inproj(ctx, jnp.zeros((t_ctx, d), F32), m_ctx[:, 1], m_ctx[:, 0], g1,
                                     w_main, w_gate, _row_tile(t_ctx))

    a = _mlstm(zm_lat, zm_ctx, gt_lat, gt_ctx, conv_w, conv_b, gate_b, ml_norm_g[0].reshape(1, V_W))
    g = _gla(zg_lat, zg_ctx, gt_lat, gt_ctx, w2_ext, b2_ext, gla_norm_g[0].reshape(1, V_W))

    x1, u2, route = _outproj(x, pe, a, g, w_out[0].astype(BF16), m_lat[:, 2], m_lat[:, 4], m_lat[:, 3],
                             norm2_g[0].reshape(1, d), router_w_p, router_b_p, tm)

    return _moe(u2, route, x1, moe_w_gu[0], moe_b_gu[0][:, None, :], moe_w_down[0],
                moe_b_down[0][:, None, :], m_lat[:, 5], final_norm_g.reshape(1, d))
```

```python
import functools
import math

import numpy as np
import jax
import jax.numpy as jnp
from jax import lax
from jax.experimental import pallas as pl
from jax.experimental.pallas import tpu as pltpu

F32 = jnp.float32
BF16 = jnp.bfloat16
HIGHEST = lax.Precision.HIGHEST

EPS = 1e-6
GRID_W = 64
POS_BASE = 10000.0
N_MOD = 6
N_HEADS = 4
DK = 64
DV = 128
QK_W = N_HEADS * 2 * DK
V_W = N_HEADS * DV
GROUP_W = QK_W + 2 * V_W
GATE_W = 128
ML_CONV = 3
GLA_RANK = 16
GLA_NORMALIZER = 16.0
TOP_K = 4
SWIGLU_LIMIT = 7.0
SWIGLU_ALPHA = 1.702
LANES = 128
ML_CHUNK = 128
GLA_CHUNK = 64
GLA_PREP_ROWS = 256
CONV_ROWS = 128
EXPERT_TILE = 512
ROUTE_BLOCK = 512
RUN_ALIGN = 8
RUN_SIZES = tuple(s for s in (ROUTE_BLOCK >> i for i in range(ROUTE_BLOCK.bit_length())) if s >= RUN_ALIGN)
VMEM_LIMIT = 56 * 1024 * 1024


def _params(sem):
    return pltpu.CompilerParams(dimension_semantics=sem, vmem_limit_bytes=VMEM_LIMIT)


def _sigmoid(x):
    return 1.0 / (1.0 + jnp.exp(-x))


def _log_sigmoid(x):
    return jnp.minimum(x, 0.0) - jnp.log1p(jnp.exp(-jnp.abs(x)))


def _dot(a, b, precision=None):
    return jnp.dot(a, b, preferred_element_type=F32, precision=precision)


def _split2(x):
    hi = x.astype(BF16)
    lo = (x - hi.astype(F32)).astype(BF16)
    return hi, lo


def _dot3(a, b):
    a_hi, a_lo = _split2(a)
    b_hi, b_lo = _split2(b)
    return _dot(a_hi, b_hi) + _dot(a_lo, b_hi) + _dot(a_hi, b_lo)


def _dot_nt(a, b):
    return lax.dot_general(a, b, (((1,), (1,)), ((), ())), preferred_element_type=F32)


def _dot_tn(a, b):
    return lax.dot_general(a, b, (((0,), (0,)), ((), ())), preferred_element_type=F32)


def _adaln_body(c_ref, w_ref, b_ref, o_ref):
    c = c_ref[...]
    s = c * _sigmoid(c)
    o_ref[...] = _dot3(s, w_ref[...]) + b_ref[...]


def _adaln(cond, w, b):
    rows, d = cond.shape
    n = w.shape[1]
    tn = 512
    return pl.pallas_call(
        _adaln_body,
        grid=(n // tn,),
        in_specs=[pl.BlockSpec((rows, d), lambda j: (0, 0)),
                  pl.BlockSpec((d, tn), lambda j: (0, j)),
                  pl.BlockSpec((1, tn), lambda j: (0, j))],
        out_specs=pl.BlockSpec((rows, tn), lambda j: (0, j)),
        out_shape=jax.ShapeDtypeStruct((rows, n), F32),
        compiler_params=_params(("arbitrary",)),
        name="adaln",
    )(cond, w, b.reshape(1, n))


def _rms_mod(x, g, scale, shift):
    y = x * lax.rsqrt(jnp.mean(x * x, axis=-1, keepdims=True) + EPS)
    return (y * g) * (1.0 + scale) + shift


def _inproj_body(x_ref, pe_ref, sc_ref, sh_ref, g_ref, wm_ref, wg_ref, zm_ref, zg_ref, gt_ref):
    x = x_ref[0] + pe_ref[...]
    u = _rms_mod(x, g_ref[...], sc_ref[0], sh_ref[0]).astype(BF16)
    z = _dot(u, wm_ref[...])
    zm_ref[0] = z[:, :GROUP_W].astype(BF16)
    zg_ref[0] = z[:, GROUP_W:].astype(BF16)
    gt_ref[0] = _dot(u, wg_ref[...])


def _inproj(x, pe, scale, shift, g, w_main, w_gate, tm):
    bsz, t, d = x.shape
    per_batch = scale.shape[0] == bsz and bsz > 1
    mod_map = (lambda i, b: (b, 0, 0)) if per_batch else (lambda i, b: (0, 0, 0))
    return pl.pallas_call(
        _inproj_body,
        grid=(t // tm, bsz),
        in_specs=[pl.BlockSpec((1, tm, d), lambda i, b: (b, i, 0)),
                  pl.BlockSpec((tm, d), lambda i, b: (i, 0)),
                  pl.BlockSpec((1, 1, d), mod_map),
                  pl.BlockSpec((1, 1, d), mod_map),
                  pl.BlockSpec((1, d), lambda i, b: (0, 0)),
                  pl.BlockSpec((d, 2 * GROUP_W), lambda i, b: (0, 0)),
                  pl.BlockSpec((d, GATE_W), lambda i, b: (0, 0))],
        out_specs=[pl.BlockSpec((1, tm, GROUP_W), lambda i, b: (b, i, 0)),
                   pl.BlockSpec((1, tm, GROUP_W), lambda i, b: (b, i, 0)),
                   pl.BlockSpec((1, tm, GATE_W), lambda i, b: (b, i, 0))],
        out_shape=[jax.ShapeDtypeStruct((bsz, t, GROUP_W), BF16),
                   jax.ShapeDtypeStruct((bsz, t, GROUP_W), BF16),
                   jax.ShapeDtypeStruct((bsz, t, GATE_W), F32)],
        compiler_params=_params(("arbitrary", "arbitrary")),
        name="inproj",
    )(x, pe, scale, shift, g, w_main, w_gate)


def _scan_rows(x, op, ident, reverse, rowi):
    n = x.shape[0]
    k = 1
    while k < n:
        if reverse:
            shifted = jnp.where(rowi < n - k, pltpu.roll(x, n - k, axis=0), ident)
        else:
            shifted = jnp.where(rowi >= k, pltpu.roll(x, k, axis=0), ident)
        x = op(x, shifted)
        k *= 2
    return x


HEAD_DIRS = 2 * N_HEADS


def _mlstm_body(zl_ref, zc_ref, gl_ref, gc_ref, cw_ref, cb_ref, gb_ref, ng_ref, o_ref,
                q_s, kt_s, acc_s, cc_s, bc_s, st_s, mp_s, m_s, en_s, cr_s, wk_s, wc_s, c_s, *, t_lat, t_ctx):
    L = ML_CHUNK
    nc_ctx = t_ctx // L
    nc_lat = t_lat // L
    nc = nc_ctx + nc_lat
    scale = DK ** -0.5

    cw = cw_ref[...]
    cb = cb_ref[...]

    def conv_pass(z_ref, n, dst0):
        nb = n // CONV_ROWS
        row = lax.broadcasted_iota(jnp.int32, (CONV_ROWS, QK_W), 0)
        lane = lax.broadcasted_iota(jnp.int32, (CONV_ROWS, QK_W), 1)
        low = (lane % LANES) < DK

        def body(r, carry):
            r0 = pl.multiple_of(r * CONV_ROWS, CONV_ROWS)
            zc = z_ref[0, pl.ds(r0, CONV_ROWS), 0:QK_W].astype(F32)
            p0 = pl.multiple_of(jnp.maximum(r0 - 16, 0), 16)
            prev = z_ref[0, pl.ds(p0, 16), 0:QK_W].astype(F32)[15:16]
            prev = jnp.where(r > 0, prev, 0.0)
            n0 = pl.multiple_of(jnp.minimum(r0 + CONV_ROWS, n - 16), 16)
            nxt = z_ref[0, pl.ds(n0, 16), 0:QK_W].astype(F32)[0:1]
            nxt = jnp.where(r < nb - 1, nxt, 0.0)
            up = jnp.where(row == 0, prev, pltpu.roll(zc, 1, axis=0))
            dn = jnp.where(row == CONV_ROWS - 1, nxt, pltpu.roll(zc, CONV_ROWS - 1, axis=0))
            y = cw[0:1] * up + cw[1:2] * zc + cw[2:3] * dn + cb
            y = y * _sigmoid(y)
            q = jnp.where(low, y * scale, 0.0)
            k = jnp.where(low, pltpu.roll(y, QK_W - DK, axis=1), 0.0)
            d0 = pl.multiple_of(dst0 + r0, CONV_ROWS)
            q_s[pl.ds(d0, CONV_ROWS), :] = q.astype(BF16)
            for h in range(N_HEADS):
                kt_s[h, :, pl.ds(d0, CONV_ROWS)] = k[:, h * LANES:(h + 1) * LANES].T.astype(BF16)
            return carry

        lax.fori_loop(0, nb, body, 0)

    conv_pass(zc_ref, t_ctx, 0)
    conv_pass(zl_ref, t_lat, t_ctx)

    rowi = lax.broadcasted_iota(jnp.int32, (L, LANES), 0)
    lane_g = lax.broadcasted_iota(jnp.int32, (L, GATE_W), 1)
    fwd_lane = lane_g < N_HEADS
    gb = gb_ref[...]

    def gate_pass(g_ref, nchunks, g0):
        def body(i, carry):
            r0 = pl.multiple_of(i * L, L)
            gates = g_ref[0, pl.ds(r0, L), :] + gb
            lf = pltpu.roll(_log_sigmoid(gates), GATE_W - HEAD_DIRS, axis=1)
            b = jnp.where(fwd_lane, _scan_rows(lf, jnp.add, 0.0, False, rowi),
                          _scan_rows(lf, jnp.add, 0.0, True, rowi))
            c = gates - b
            cc_s[g0 + i] = c
            bc_s[g0 + i] = b
            st_s[g0 + i, 0:1, :] = jnp.max(c, axis=0, keepdims=True)
            st_s[g0 + i, 1:2, :] = jnp.sum(lf, axis=0, keepdims=True)
            return carry

        lax.fori_loop(0, nchunks, body, 0)

    gate_pass(gc_ref, nc_ctx, 0)
    gate_pass(gl_ref, nc_lat, nc_ctx)

    lane1 = lax.broadcasted_iota(jnp.int32, (1, GATE_W), 1)
    fwd_order = list(range(nc))
    bwd_order = list(range(nc_ctx - 1, -1, -1)) + list(range(nc - 1, nc_ctx - 1, -1))
    mp = {}
    for d, order in enumerate((fwd_order, bwd_order)):
        m = jnp.zeros((1, GATE_W), F32)
        for g in order:
            mp[(d, g)] = m
            m = st_s[g, 1:2, :] + jnp.maximum(m, st_s[g, 0:1, :])
    for g in range(nc):
        mp_s[g, 0:1, :] = jnp.where(lane1 < N_HEADS, mp[(0, g)], mp[(1, g)])

    def ab_body(g, carry):
        c = cc_s[g]
        b = bc_s[g]
        mprev = mp_s[g, 0:1, :]
        mloc = jnp.where(fwd_lane, _scan_rows(c, jnp.maximum, -jnp.inf, False, rowi),
                         _scan_rows(c, jnp.maximum, -jnp.inf, True, rowi))
        m = jnp.maximum(mloc, mprev)
        mlast = jnp.maximum(mprev, st_s[g, 0:1, :])
        m_s[g] = m
        en_s[g] = jnp.exp(-(b + m))
        cr_s[g] = c.T[0:HEAD_DIRS]
        wk_s[g] = jnp.exp((c - mlast).T[0:HEAD_DIRS])
        wc_s[g] = jnp.exp(jnp.broadcast_to(mprev - mlast, (L, GATE_W)).T[0:HEAD_DIRS])
        return carry

    lax.fori_loop(0, nc, ab_body, 0)

    colL = lax.broadcasted_iota(jnp.int32, (L, L), 1)
    rowL = lax.broadcasted_iota(jnp.int32, (L, L), 0)
    masks = (colL <= rowL, colL >= rowL)
    ones_blk = jnp.ones((L, DV), BF16)
    c_s[...] = jnp.zeros(c_s.shape, F32)

    def twice(x):
        return jnp.concatenate([x, x], axis=1)

    def step(g, z_ref, r_src, r_q, d, with_out, r_out):
        for h in range(N_HEADS):
            j = d * N_HEADS + h
            hs = slice(h * DV, (h + 1) * DV)
            kt = kt_s[h, :, pl.ds(r_q, L)]
            v = z_ref[0, pl.ds(r_src, L), QK_W + h * DV:QK_W + (h + 1) * DV]
            vext = jnp.concatenate([v, ones_blk], axis=1)
            state = c_s[j]
            if with_out:
                q = q_s[pl.ds(r_q, L), hs]
                mb = jnp.broadcast_to(m_s[g][:, j:j + 1], (L, L))
                w = jnp.where(masks[d], jnp.exp(cr_s[g, j:j + 1, :] - mb), 0.0)
                w_int = jnp.exp(mp_s[g][0:1, j:j + 1] - mb)
                p = jnp.concatenate([(_dot(q, kt) * w).astype(BF16), (q.astype(F32) * w_int).astype(BF16)], axis=1)
                nd = _dot(p, jnp.concatenate([vext, state.astype(BF16)], axis=0))
                hh = nd[:, :DV] / jnp.maximum(jnp.abs(nd[:, DV:]), en_s[g][:, j:j + 1])
                acc_s[d, pl.ds(r_out, L), hs] = hh
            ktw = (kt.astype(F32) * wk_s[g, j:j + 1, :]).astype(BF16)
            c_s[j] = twice(wc_s[g, j:j + 1, :]) * state + _dot(ktw, vext)

    def ctx_body(i, carry):
        for d in (0, 1):
            ii = i if d == 0 else nc_ctx - 1 - i
            r0 = pl.multiple_of(ii * L, L)
            step(ii, zc_ref, r0, r0, d, False, None)
        return carry

    def lat_body(i, carry):
        for d in (0, 1):
            ii = i if d == 0 else nc_lat - 1 - i
            r0 = pl.multiple_of(ii * L, L)
            step(nc_ctx + ii, zl_ref, r0, pl.multiple_of(t_ctx + ii * L, L), d, True, r0)
        return carry

    lax.fori_loop(0, nc_ctx, ctx_body, 0)
    lax.fori_loop(0, nc_lat, lat_body, 0)

    def out_body(i, carry):
        r0 = pl.multiple_of(i * L, L)
        for h in range(N_HEADS):
            hs = slice(h * DV, (h + 1) * DV)
            tot = acc_s[0, pl.ds(r0, L), hs] + acc_s[1, pl.ds(r0, L), hs]
            tot = tot * lax.rsqrt(jnp.mean(tot * tot, axis=-1, keepdims=True) + EPS)
            og = zl_ref[0, pl.ds(r0, L), QK_W + V_W + h * DV:QK_W + V_W + (h + 1) * DV].astype(F32)
            o_ref[0, pl.ds(r0, L), hs] = (tot * ng_ref[:, hs] * _sigmoid(og)).astype(BF16)
        return carry

    lax.fori_loop(0, nc_lat, out_body, 0)


def _mlstm(zm_lat, zm_ctx, gt_lat, gt_ctx, conv_w, conv_b, gate_b, norm_g):
    bsz, t_lat, _ = zm_lat.shape
    t_ctx = zm_ctx.shape[1]
    nc = (t_lat + t_ctx) // ML_CHUNK
    body = functools.partial(_mlstm_body, t_lat=t_lat, t_ctx=t_ctx)
    full = lambda b: (0, 0)
    return pl.pallas_call(
        body,
        grid=(bsz,),
        in_specs=[pl.BlockSpec((1, t_lat, GROUP_W), lambda b: (b, 0, 0)),
                  pl.BlockSpec((1, t_ctx, GROUP_W), lambda b: (b, 0, 0)),
                  pl.BlockSpec((1, t_lat, GATE_W), lambda b: (b, 0, 0)),
                  pl.BlockSpec((1, t_ctx, GATE_W), lambda b: (b, 0, 0)),
                  pl.BlockSpec((ML_CONV, QK_W), full),
                  pl.BlockSpec((1, QK_W), full),
                  pl.BlockSpec((1, GATE_W), full),
                  pl.BlockSpec((1, V_W), full)],
        out_specs=pl.BlockSpec((1, t_lat, V_W), lambda b: (b, 0, 0)),
        out_shape=jax.ShapeDtypeStruct((bsz, t_lat, V_W), BF16),
        scratch_shapes=[pltpu.VMEM((t_ctx + t_lat, QK_W), BF16),
                        pltpu.VMEM((N_HEADS, LANES, t_ctx + t_lat), BF16),
                        pltpu.VMEM((2, t_lat, V_W), F32),
                        pltpu.VMEM((nc, ML_CHUNK, GATE_W), F32),
                        pltpu.VMEM((nc, ML_CHUNK, GATE_W), F32),
                        pltpu.VMEM((nc, 8, GATE_W), F32),
                        pltpu.VMEM((nc, 8, GATE_W), F32),
                        pltpu.VMEM((nc, ML_CHUNK, GATE_W), F32),
                        pltpu.VMEM((nc, ML_CHUNK, GATE_W), F32),
                        pltpu.VMEM((nc, HEAD_DIRS, ML_CHUNK), F32),
                        pltpu.VMEM((nc, HEAD_DIRS, ML_CHUNK), F32),
                        pltpu.VMEM((nc, HEAD_DIRS, GATE_W), F32),
                        pltpu.VMEM((HEAD_DIRS, 2 * DK, 2 * DV), F32)],
        compiler_params=_params(("arbitrary",)),
        name="mlstm",
    )(zm_lat, zm_ctx, gt_lat, gt_ctx, conv_w, conv_b, gate_b, norm_g)


def _gla_body(zl_ref, zc_ref, gl_ref, gc_ref, w2_ref, b2_ref, ng_ref, o_ref,
              acc_s, s_s, qt_s, kt_s, ee_s, *, t_lat, t_ctx):
    L = GLA_CHUNK
    nc_ctx = t_ctx // L
    nc_lat = t_lat // L
    scale = DK ** -0.5
    half = N_HEADS * DK

    rowi = lax.broadcasted_iota(jnp.int32, (L, L), 0)
    coli = lax.broadcasted_iota(jnp.int32, (L, L), 1)
    masks = (coli <= rowi, coli >= rowi)
    tri_b = (masks[0].astype(BF16), masks[1].astype(BF16))
    low = lax.broadcasted_iota(jnp.int32, (L, LANES), 1) < DK
    w2_hi, w2_lo = _split2(w2_ref[...])
    bias2 = b2_ref[...]

    def prep(z_ref, g_ref, n_rows, c0, row0):
        grp = math.gcd(n_rows, GLA_PREP_ROWS)

        def body(gi, carry):
            rg = pl.multiple_of(gi * grp, grp)
            g_hi, g_lo = _split2(g_ref[0, pl.ds(rg, grp), :])
            pre = _dot(g_hi, w2_hi) + _dot(g_lo, w2_hi) + _dot(g_hi, w2_lo) + bias2
            la_hi, la_lo = _split2(_log_sigmoid(pre) * (1.0 / GLA_NORMALIZER))
            for cc in range(grp // L):
                rs = slice(cc * L, (cc + 1) * L)
                r0 = pl.multiple_of(rg + cc * L, L)
                rq = pl.multiple_of(row0 + rg + cc * L, L)
                c = c0 + gi * (grp // L) + cc
                for d in (0, 1):
                    ds_ = slice(d * half, (d + 1) * half)
                    b_all = _dot(tri_b[d], la_hi[rs, ds_]) + _dot(tri_b[d], la_lo[rs, ds_])
                    for hp in range(N_HEADS // 2):
                        pair = b_all[:, hp * LANES:(hp + 1) * LANES]
                        swapped = pltpu.roll(pair, DK, axis=1)
                        for hh in range(2):
                            h = 2 * hp + hh
                            hs = slice(h * LANES, (h + 1) * LANES)
                            b2 = jnp.where(low, pair, swapped) if hh == 0 else jnp.where(low, swapped, pair)
                            b_end = b2[L - 1:L] if d == 0 else b2[0:1]
                            qk = z_ref[0, pl.ds(r0, L), hs].astype(F32)
                            qkt = qk * jnp.exp(jnp.where(low, b2, -b2))
                            qt_s[d, pl.ds(rq, L), hs] = jnp.where(low, qkt * scale, 0.0).astype(BF16)
                            kt_s[d, pl.ds(rq, L), hs] = jnp.where(low, pltpu.roll(qkt, DK, axis=1), 0.0).astype(BF16)
                            ee_s[d, c, 0:1, hs] = jnp.exp(b_end)
            return carry

        lax.fori_loop(0, n_rows // grp, body, 0)

    prep(zc_ref, gc_ref, t_ctx, 0, 0)
    prep(zl_ref, gl_ref, t_lat, nc_ctx, t_ctx)

    s_s[...] = jnp.zeros(s_s.shape, F32)

    def step(z_ref, r0, rq, c, d, with_out):
        for h in range(N_HEADS):
            j = d * N_HEADS + h
            hs = slice(h * LANES, (h + 1) * LANES)
            qt = qt_s[d, pl.ds(rq, L), hs]
            kt = kt_s[d, pl.ds(rq, L), hs]
            v = z_ref[0, pl.ds(r0, L), QK_W + h * DV:QK_W + (h + 1) * DV]
            state = s_s[j]
            if with_out:
                a = jnp.where(masks[d], _dot_nt(qt, kt), 0.0).astype(BF16)
                acc_s[d, pl.ds(r0, L), hs] = _dot(a, v) + _dot_nt(qt, state.astype(BF16))
            e_end = ee_s[d, c, 0:1, hs]
            s_s[j] = state * e_end + _dot_tn(v, (kt.astype(F32) * e_end).astype(BF16))

    def loop(z_ref, n, c0, row0, with_out):
        def body(i, carry):
            for d in (0, 1):
                ii = i if d == 0 else n - 1 - i
                r0 = pl.multiple_of(ii * L, L)
                step(z_ref, r0, pl.multiple_of(row0 + ii * L, L), c0 + ii, d, with_out)
            return carry
        lax.fori_loop(0, n, body, 0, unroll=2)

    loop(zc_ref, nc_ctx, 0, 0, False)
    loop(zl_ref, nc_lat, nc_ctx, t_ctx, True)

    def out_body(i, carry):
        r0 = pl.multiple_of(i * L, L)
        for h in range(N_HEADS):
            hs = slice(h * LANES, (h + 1) * LANES)
            tot = acc_s[0, pl.ds(r0, L), hs] + acc_s[1, pl.ds(r0, L), hs]
            tot = tot * lax.rsqrt(jnp.mean(tot * tot, axis=-1, keepdims=True) + EPS)
            gg = zl_ref[0, pl.ds(r0, L), QK_W + V_W + h * DV:QK_W + V_W + (h + 1) * DV].astype(F32)
            o_ref[0, pl.ds(r0, L), hs] = (tot * ng_ref[:, hs] * (gg * _sigmoid(gg))).astype(BF16)
        return carry

    lax.fori_loop(0, nc_lat, out_body, 0)


def _gla(zg_lat, zg_ctx, gt_lat, gt_ctx, w2_ext, b2_ext, norm_g):
    bsz, t_lat, _ = zg_lat.shape
    t_ctx = zg_ctx.shape[1]
    body = functools.partial(_gla_body, t_lat=t_lat, t_ctx=t_ctx)
    full = lambda b: (0, 0)
    return pl.pallas_call(
        body,
        grid=(bsz,),
        in_specs=[pl.BlockSpec((1, t_lat, GROUP_W), lambda b: (b, 0, 0)),
                  pl.BlockSpec((1, t_ctx, GROUP_W), lambda b: (b, 0, 0)),
                  pl.BlockSpec((1, t_lat, GATE_W), lambda b: (b, 0, 0)),
                  pl.BlockSpec((1, t_ctx, GATE_W), lambda b: (b, 0, 0)),
                  pl.BlockSpec((GATE_W, 2 * N_HEADS * DK), full),
                  pl.BlockSpec((1, 2 * N_HEADS * DK), full),
                  pl.BlockSpec((1, V_W), full)],
        out_specs=pl.BlockSpec((1, t_lat, V_W), lambda b: (b, 0, 0)),
        out_shape=jax.ShapeDtypeStruct((bsz, t_lat, V_W), BF16),
        scratch_shapes=[pltpu.VMEM((2, t_lat, V_W), F32),
                        pltpu.VMEM((HEAD_DIRS, DV, LANES), F32),
                        pltpu.VMEM((2, t_ctx + t_lat, QK_W), BF16),
                        pltpu.VMEM((2, t_ctx + t_lat, QK_W), BF16),
                        pltpu.VMEM((2, (t_ctx + t_lat) // GLA_CHUNK, 8, QK_W), F32)],
        compiler_params=_params(("arbitrary",)),
        name="gla",
    )(zg_lat, zg_ctx, gt_lat, gt_ctx, w2_ext, b2_ext, norm_g)


def _outproj_body(x_ref, pe_ref, a_ref, g_ref, wo_ref, g1_ref, sc_ref, sh_ref, n2_ref, rw_ref, rb_ref,
                  x1_ref, u2_ref, route_ref):
    x = x_ref[0] + pe_ref[...]
    y = _dot(a_ref[0], wo_ref[0:V_W, :]) + _dot(g_ref[0], wo_ref[V_W:2 * V_W, :])
    x1 = x + g1_ref[0] * y
    x1_ref[0] = x1
    u2 = _rms_mod(x1, n2_ref[...], sc_ref[0], sh_ref[0])
    u2_ref[0] = u2
    logits = _dot3(u2, rw_ref[...]) + rb_ref[...]
    lane = lax.broadcasted_iota(jnp.int32, logits.shape, 1).astype(F32)
    ids = []
    vals = []
    for _ in range(TOP_K):
        m = jnp.max(logits, axis=1, keepdims=True)
        idx = jnp.min(jnp.where(logits == m, lane, float(LANES)), axis=1, keepdims=True)
        ids.append(idx)
        vals.append(m)
        logits = jnp.where(lane == idx, -jnp.inf, logits)
    exps = [jnp.exp(v - vals[0]) for v in vals]
    denom = exps[0]
    for e in exps[1:]:
        denom = denom + e
    route = jnp.zeros(logits.shape, F32)
    for k in range(TOP_K):
        route = jnp.where(lane == float(k), ids[k], route)
        route = jnp.where(lane == float(TOP_K + k), exps[k] / denom, route)
    route_ref[0] = route


def _outproj(x, pe, a, g, w_out, gate1, scale2, shift2, norm2_g, router_w, router_b, tm):
    bsz, t, d = x.shape
    tok = lambda i, b: (b, i, 0)
    mod = lambda i, b: (b, 0, 0)
    full = lambda i, b: (0, 0)
    return pl.pallas_call(
        _outproj_body,
        grid=(t // tm, bsz),
        in_specs=[pl.BlockSpec((1, tm, d), tok),
                  pl.BlockSpec((tm, d), lambda i, b: (i, 0)),
                  pl.BlockSpec((1, tm, V_W), tok),
                  pl.BlockSpec((1, tm, V_W), tok),
                  pl.BlockSpec((2 * V_W, d), full),
                  pl.BlockSpec((1, 1, d), mod),
                  pl.BlockSpec((1, 1, d), mod),
                  pl.BlockSpec((1, 1, d), mod),
                  pl.BlockSpec((1, d), full),
                  pl.BlockSpec((d, LANES), full),
                  pl.BlockSpec((1, LANES), full)],
        out_specs=[pl.BlockSpec((1, tm, d), tok),
                   pl.BlockSpec((1, tm, d), tok),
                   pl.BlockSpec((1, tm, LANES), tok)],
        out_shape=[jax.ShapeDtypeStruct((bsz, t, d), F32),
                   jax.ShapeDtypeStruct((bsz, t, d), F32),
                   jax.ShapeDtypeStruct((bsz, t, LANES), F32)],
        compiler_params=_params(("arbitrary", "arbitrary")),
        name="outproj_router",
    )(x, pe, a, g, w_out, gate1, scale2, shift2, norm2_g, router_w, router_b)


def _lane_excl_cumsum(x):
    x8 = jnp.broadcast_to(x, (8, LANES))
    lane8 = lax.broadcasted_iota(jnp.int32, (8, LANES), 1)
    inc = x8
    s = 1
    while s < LANES:
        inc = inc + jnp.where(lane8 >= s, pltpu.roll(inc, s, axis=1), 0.0)
        s *= 2
    return (inc - x8)[0:1]


def _route_pos_body(route_ref, lic_ref, lit_ref, tab_ref, cnt_ref, run_s, off_s, *, tile):
    p = pl.program_id(0)
    j = pl.program_id(1)
    nb = pl.num_programs(1)
    tb = route_ref.shape[0]
    route = route_ref[...]
    lane = lax.broadcasted_iota(jnp.int32, (tb, LANES), 1).astype(F32)
    onehots = [lane == route[:, k:k + 1] for k in range(TOP_K)]
    oh = onehots[0].astype(F32)
    for o in onehots[1:]:
        oh = oh + o.astype(F32)
    col_counts = jnp.ceil(jnp.sum(oh, axis=0, keepdims=True) * (1.0 / RUN_ALIGN)) * float(RUN_ALIGN)

    @pl.when((p == 0) & (j == 0))
    def _():
        run_s[...] = jnp.zeros(run_s.shape, F32)

    @pl.when(p == 0)
    def _():
        run_s[...] += col_counts

    @pl.when((p == 0) & (j == nb - 1))
    def _():
        cnt_ref[...] = run_s[...]

    @pl.when((p == 1) & (j == 0))
    def _():
        cpad = jnp.floor((run_s[...] + float(tile - 1)) * (1.0 / tile)) * float(tile)
        off_s[...] = _lane_excl_cumsum(cpad)

    @pl.when(p == 1)
    def _():
        r = lax.broadcasted_iota(jnp.int32, (tb, tb), 0)
        c = lax.broadcasted_iota(jnp.int32, (tb, tb), 1)
        before = _dot((c < r).astype(BF16), oh.astype(BF16))
        lstart = _lane_excl_cumsum(col_counts)
        local = lstart + before
        li = jnp.zeros((tb, LANES), F32)
        for k in range(TOP_K):
            lk = jnp.sum(jnp.where(onehots[k], local, 0.0), axis=1, keepdims=True)
            li = jnp.where(lane == float(k), lk, li)
        lic_ref[...] = li
        lit_ref[0] = li.T[0:8]
        row8 = lax.broadcasted_iota(jnp.int32, (8, LANES), 0)
        total = jnp.sum(col_counts, axis=1, keepdims=True)
        tab = jnp.where(row8 == 0, off_s[...],
                        jnp.where(row8 == 1, col_counts, jnp.where(row8 == 2, lstart, jnp.where(row8 == 3, total, 0.0))))
        tab_ref[0] = tab.astype(jnp.int32)
        off_s[...] += col_counts


def _route_pos(route, tile):
    n = route.shape[0]
    tb = ROUTE_BLOCK
    nb = n // tb
    return pl.pallas_call(
        functools.partial(_route_pos_body, tile=tile),
        grid=(2, nb),
        in_specs=[pl.BlockSpec((tb, LANES), lambda p, j: (j, 0))],
        out_specs=[pl.BlockSpec((tb, LANES), lambda p, j: (p * j, 0)),
                   pl.BlockSpec((1, 8, tb), lambda p, j: (p * j, 0, 0)),
                   pl.BlockSpec((1, 8, LANES), lambda p, j: (p * j, 0, 0)),
                   pl.BlockSpec((1, LANES), lambda p, j: (0, 0))],
        out_shape=[jax.ShapeDtypeStruct((n, LANES), F32),
                   jax.ShapeDtypeStruct((nb, 8, tb), F32),
                   jax.ShapeDtypeStruct((nb, 8, LANES), jnp.int32),
                   jax.ShapeDtypeStruct((1, LANES), F32)],
        scratch_shapes=[pltpu.VMEM((1, LANES), F32), pltpu.VMEM((1, LANES), F32)],
        compiler_params=_params(("arbitrary", "arbitrary")),
        name="route_pos",
    )(route)


def _issue_runs(tab_ref, n_experts, make_copy):
    def per_expert(e, carry):
        g = tab_ref[0, 0, e]
        cnt = tab_ref[0, 1, e]
        l = tab_ref[0, 2, e]
        for size in RUN_SIZES:
            part = cnt & size

            @pl.when(part != 0)
            def _():
                make_copy(pl.multiple_of(l, RUN_ALIGN), pl.multiple_of(g, RUN_ALIGN), size).start()

            l = l + part
            g = g + part
        return carry

    lax.fori_loop(0, n_experts, per_expert, 0)


def _dispatch_body(zrow_ref, zflag_ref, nu_ref, tab_ref, lit_ref, u_ref, xs_ref, buf, zero_s, tot_s, sem, zsem,
                   *, n_experts, tile):
    i = pl.program_id(0)
    nb = pl.num_programs(0)
    tb = u_ref.shape[0]
    rows = buf.shape[1]
    n_tiles = xs_ref.shape[0] // tile
    slot = i % 2

    @pl.when(i == 0)
    def _():
        zero_s[...] = jnp.zeros(zero_s.shape, F32)

        def zero_tile(r0):
            cp = pltpu.make_async_copy(zero_s, xs_ref.at[pl.ds(pl.multiple_of(r0, tile), tile)], zsem)
            cp.start()
            cp.wait()

        for e in range(n_experts):
            @pl.when(zflag_ref[e] == 1)
            def _():
                zero_tile(zrow_ref[e])

        def tail(t, carry):
            zero_tile(t * tile)
            return carry

        lax.fori_loop(nu_ref[0], n_tiles, tail, 0)

    def wait_slot(s):
        nrows = pl.multiple_of(tot_s[s], RUN_ALIGN)
        pltpu.make_async_copy(buf.at[s, pl.ds(0, nrows)], xs_ref.at[pl.ds(0, nrows)], sem.at[s]).wait()

    @pl.when(i >= 2)
    def _():
        wait_slot(slot)

    tot_s[slot] = tab_ref[0, 3, 0]

    lit = lit_ref[0]
    ri = lax.broadcasted_iota(jnp.int32, (rows, tb), 0).astype(F32)
    hit = ri == lit[0:1]
    for k in range(1, TOP_K):
        hit = hit | (ri == lit[k:k + 1])
    buf[slot] = _dot(hit.astype(BF16), u_ref[...].astype(BF16))
    _issue_runs(tab_ref, n_experts,
                lambda l, g, size: pltpu.make_async_copy(buf.at[slot, pl.ds(l, size)], xs_ref.at[pl.ds(g, size)],
                                                         sem.at[slot]))

    @pl.when(i == nb - 1)
    def _():
        wait_slot(slot)

    @pl.when((i == nb - 1) & (nb > 1))
    def _():
        wait_slot(1 - slot)


def _dispatch(u2, tab, lit, zrow, zflag, n_used, n_rows, tile):
    n, d = u2.shape
    tb = ROUTE_BLOCK
    nb = n // tb
    n_experts = zrow.shape[0]
    grid_spec = pltpu.PrefetchScalarGridSpec(
        num_scalar_prefetch=3,
        grid=(nb,),
        in_specs=[pl.BlockSpec((1, 8, LANES), lambda i, zr, zf, nu: (i, 0, 0), memory_space=pltpu.SMEM),
                  pl.BlockSpec((1, 8, tb), lambda i, zr, zf, nu: (i, 0, 0)),
                  pl.BlockSpec((tb, d), lambda i, zr, zf, nu: (i, 0))],
        out_specs=pl.BlockSpec(memory_space=pl.ANY),
        scratch_shapes=[pltpu.VMEM((2, tb * TOP_K + n_experts * RUN_ALIGN, d), F32), pltpu.VMEM((tile, d), F32),
                        pltpu.SMEM((2,), jnp.int32), pltpu.SemaphoreType.DMA((2,)), pltpu.SemaphoreType.DMA(())],
    )
    return pl.pallas_call(
        functools.partial(_dispatch_body, n_experts=n_experts, tile=tile),
        grid_spec=grid_spec,
        out_shape=jax.ShapeDtypeStruct((n_rows, d), F32),
        compiler_params=_params(("arbitrary",)),
        name="dispatch",
    )(zrow, zflag, n_used, tab, lit, u2)


def _experts_body(te_ref, nu_ref, x_ref, wgu_ref, bgu_ref, wd_ref, bd_ref, y_ref, wgu_s, wd_s, *, d_expert):
    t = pl.program_id(0)

    @pl.when((t == 0) | (te_ref[t] != te_ref[jnp.maximum(t - 1, 0)]))
    def _():
        wgu_s[...] = wgu_ref[0].astype(BF16)
        wd_s[...] = wd_ref[0].astype(BF16)

    @pl.when(t < nu_ref[0])
    def _():
        gu = _dot(x_ref[...].astype(BF16), wgu_s[...]) + bgu_ref[0]
        gate = jnp.minimum(gu[:, :d_expert], SWIGLU_LIMIT)
        up = jnp.clip(gu[:, d_expert:], -SWIGLU_LIMIT, SWIGLU_LIMIT)
        act = ((up + 1.0) * (gate * _sigmoid(SWIGLU_ALPHA * gate))).astype(BF16)
        y_ref[...] = _dot(act, wd_s[...]) + bd_ref[0]

    @pl.when(t >= nu_ref[0])
    def _():
        y_ref[...] = jnp.zeros(y_ref.shape, F32)


def _experts(xs, tile_expert, n_used, w_gu, b_gu, w_down, b_down, tile):
    n_rows, d = xs.shape
    n_experts, _, two_de = w_gu.shape
    d_expert = two_de // 2
    grid_spec = pltpu.PrefetchScalarGridSpec(
        num_scalar_prefetch=2,
        grid=(n_rows // tile,),
        in_specs=[pl.BlockSpec((tile, d), lambda t, te, nu: (jnp.minimum(t, nu[0] - 1), 0)),
                  pl.BlockSpec((1, d, two_de), lambda t, te, nu: (te[t], 0, 0)),
                  pl.BlockSpec((1, 1, two_de), lambda t, te, nu: (te[t], 0, 0)),
                  pl.BlockSpec((1, d_expert, d), lambda t, te, nu: (te[t], 0, 0)),
                  pl.BlockSpec((1, 1, d), lambda t, te, nu: (te[t], 0, 0))],
        out_specs=pl.BlockSpec((tile, d), lambda t, te, nu: (t, 0)),
        scratch_shapes=[pltpu.VMEM((d, two_de), BF16), pltpu.VMEM((d_expert, d), BF16)],
    )
    return pl.pallas_call(
        functools.partial(_experts_body, d_expert=d_expert),
        grid_spec=grid_spec,
        out_shape=jax.ShapeDtypeStruct((n_rows, d), F32),
        compiler_params=_params(("arbitrary",)),
        name="experts",
    )(tile_expert, n_used, xs, w_gu, b_gu, w_down, b_down)


def _combine_body(tab_ref, tabn_ref, lic_ref, route_ref, x1_ref, g5_ref, fg_ref, ys_ref, o_ref, buf, sem,
                  *, n_experts):
    i = pl.program_id(0)
    n = pl.num_programs(0)
    tb = x1_ref.shape[0]
    rows = buf.shape[1]
    slot = i % 2

    def issue(t_ref, s):
        _issue_runs(t_ref, n_experts,
                    lambda l, g, size: pltpu.make_async_copy(ys_ref.at[pl.ds(g, size)], buf.at[s, pl.ds(l, size)],
                                                             sem.at[s]))

    @pl.when(i == 0)
    def _():
        buf[...] = jnp.zeros(buf.shape, F32)
        issue(tab_ref, 0)

    @pl.when(i + 1 < n)
    def _():
        issue(tabn_ref, 1 - slot)

    nrows = pl.multiple_of(tab_ref[0, 3, 0], RUN_ALIGN)
    pltpu.make_async_copy(ys_ref.at[pl.ds(0, nrows)], buf.at[slot, pl.ds(0, nrows)], sem.at[slot]).wait()

    lane = lax.broadcasted_iota(jnp.int32, (tb, LANES), 1)
    li_hi, li_lo = _split2(lic_ref[...])
    a = jnp.where(lane < TOP_K, li_hi.astype(F32),
                  jnp.where(lane < 2 * TOP_K, pltpu.roll(li_lo.astype(F32), TOP_K, axis=1),
                            jnp.where(lane < 3 * TOP_K, pltpu.roll(route_ref[...], TOP_K, axis=1), 0.0))).astype(BF16)
    rsel = lax.broadcasted_iota(jnp.int32, (LANES, LANES), 0)
    lane_f = lane.astype(F32)
    idx, gate = [], []
    for k in range(TOP_K):
        sel = jnp.concatenate([((rsel == k) | (rsel == TOP_K + k)).astype(BF16),
                               (rsel == 2 * TOP_K + k).astype(BF16)], axis=1)
        bk = _dot(a, sel)
        idx.append(bk[:, :LANES])
        gate.append(bk[:, LANES:])
    blocks = []
    for c in range(rows // LANES):
        col = lane_f + float(c * LANES)
        w = jnp.where(idx[0] == col, gate[0], 0.0)
        for k in range(1, TOP_K):
            w = w + jnp.where(idx[k] == col, gate[k], 0.0)
        blocks.append(w.astype(BF16))
    moe = _dot(jnp.concatenate(blocks, axis=1), buf[slot].astype(BF16))
    xo = x1_ref[...] + g5_ref[0] * moe
    o_ref[...] = xo * lax.rsqrt(jnp.mean(xo * xo, axis=-1, keepdims=True) + EPS) * fg_ref[...]


def _combine(ys, tab, lic, route, x1, gate5, final_g, t_lat, n_experts):
    n, d = x1.shape
    tb = ROUTE_BLOCK
    nb = n // tb
    per_b = t_lat // tb
    return pl.pallas_call(
        functools.partial(_combine_body, n_experts=n_experts),
        grid=(nb,),
        in_specs=[pl.BlockSpec((1, 8, LANES), lambda i: (i, 0, 0), memory_space=pltpu.SMEM),
                  pl.BlockSpec((1, 8, LANES), lambda i: (jnp.minimum(i + 1, nb - 1), 0, 0),
                               memory_space=pltpu.SMEM),
                  pl.BlockSpec((tb, LANES), lambda i: (i, 0)),
                  pl.BlockSpec((tb, LANES), lambda i: (i, 0)),
                  pl.BlockSpec((tb, d), lambda i: (i, 0)),
                  pl.BlockSpec((1, 1, d), lambda i: (i // per_b, 0, 0)),
                  pl.BlockSpec((1, d), lambda i: (0, 0)),
                  pl.BlockSpec(memory_space=pl.ANY)],
        out_specs=pl.BlockSpec((tb, d), lambda i: (i, 0)),
        out_shape=jax.ShapeDtypeStruct((n, d), F32),
        scratch_shapes=[pltpu.VMEM((2, tb * TOP_K + n_experts * RUN_ALIGN, d), F32), pltpu.SemaphoreType.DMA((2,))],
        compiler_params=_params(("arbitrary",)),
        name="combine",
    )(tab, tab, lic, route, x1, gate5, final_g, ys)


def _moe(u2, route, x1, w_gu, b_gu, w_down, b_down, gate5, final_g):
    bsz, t_lat, d = x1.shape
    n = bsz * t_lat
    n_experts = w_gu.shape[0]
    tile = EXPERT_TILE
    route2 = route.reshape(n, LANES)
    lic, lit, tab, counts = _route_pos(route2, tile)
    cnt = counts[0, :n_experts].astype(jnp.int32)
    cpad = ((cnt + tile - 1) // tile) * tile
    ends = jnp.cumsum(cpad)
    n_tiles = -(-(n * TOP_K + (n // ROUTE_BLOCK) * n_experts * RUN_ALIGN) // tile) + n_experts
    n_rows = n_tiles * tile
    tile_expert = jnp.minimum(
        jnp.sum(jnp.arange(n_tiles, dtype=jnp.int32)[:, None] * tile >= ends[None, :], axis=1), n_experts - 1
    ).astype(jnp.int32)
    n_used = (ends[-1:] // tile).astype(jnp.int32)
    zrow = jnp.maximum(ends - tile, 0).astype(jnp.int32)
    zflag = (cnt % tile != 0).astype(jnp.int32)
    xs = _dispatch(u2.reshape(n, d), tab, lit, zrow, zflag, n_used, n_rows, tile)
    ys = _experts(xs, tile_expert, n_used, w_gu, b_gu, w_down, b_down, tile)
    out = _combine(ys, tab, lic, route2, x1.reshape(n, d), gate5, final_g, t_lat, n_experts)
    return out.reshape(bsz, t_lat, d)


def _grid_sincos(rows, d):
    nf = d // 4
    omega = 1.0 / (POS_BASE ** (jnp.arange(nf, dtype=F32) / nf))
    r = jnp.broadcast_to(jnp.arange(rows, dtype=F32)[:, None, None] * omega, (rows, GRID_W, nf))
    cl = jnp.broadcast_to(jnp.arange(GRID_W, dtype=F32)[None, :, None] * omega, (rows, GRID_W, nf))
    pe = jnp.concatenate([jnp.sin(r), jnp.cos(r), jnp.sin(cl), jnp.cos(cl)], axis=-1)
    return pe.reshape(rows * GRID_W, d)


def _pack_qk(wq, wk):
    lead = wq.shape[:-1]
    q = wq.reshape(lead + (N_HEADS, DK))
    k = wk.reshape(lead + (N_HEADS, DK))
    return jnp.concatenate([q, k], axis=-1).reshape(lead + (QK_W,))


def _row_tile(t):
    for tm in (512, 256, 128):
        if t % tm == 0:
            return tm
    raise ValueError(f"sequence length {t} must be a multiple of 128")


def kernel(x, c, ctx, c_ctx, ada_w, ada_b, norm1_g, w_in, ml_conv_w, ml_conv_b, ml_gate_b, ml_norm_g,
           gla_gate_w2, gla_gate_b, gla_norm_g, w_out, norm2_g, router_w, router_b, moe_w_gu, moe_b_gu,
           moe_w_down, moe_b_down, final_norm_g):
    bsz, t_lat, d = x.shape
    t_ctx = ctx.shape[1]
    assert ada_w.shape[0] == 1, "single-layer block"
    assert t_lat % ML_CHUNK == 0 and t_ctx % ML_CHUNK == 0 and t_lat % GRID_W == 0 and t_lat % ROUTE_BLOCK == 0
    n_experts = router_w.shape[-1]
    assert n_experts <= LANES

    ml_qk, ml_v = N_HEADS * DK, N_HEADS * DV
    sizes = (ml_qk, ml_qk, ml_v, ml_v, 4 * N_HEADS, ml_qk, ml_qk, ml_v, ml_v, 2 * GLA_RANK)
    offs = [int(o) for o in np.cumsum(sizes)[:-1]]
    w_mq, w_mk, w_mv, w_mo, w_mg, w_gq, w_gk, w_gv, w_gg, w_glr = jnp.split(w_in[0], offs, axis=-1)
    w_main = jnp.concatenate([_pack_qk(w_mq, w_mk), w_mv, w_mo, _pack_qk(w_gq, w_gk), w_gv, w_gg],
                             axis=-1).astype(BF16)
    gate_perm = np.concatenate([np.arange(N_HEADS) + g * N_HEADS for g in (0, 2, 1, 3)])
    w_gate = jnp.concatenate([w_mg[:, gate_perm], w_glr,
                              jnp.zeros((d, GATE_W - 4 * N_HEADS - 2 * GLA_RANK), F32)], axis=-1).astype(BF16)
    gate_b = jnp.concatenate([ml_gate_b[0].reshape(-1)[gate_perm],
                              jnp.zeros((GATE_W - 4 * N_HEADS,), F32)]).reshape(1, GATE_W)
    conv_w = _pack_qk(ml_conv_w[0][:, :ml_qk], ml_conv_w[0][:, ml_qk:])
    conv_b = _pack_qk(ml_conv_b[0][:ml_qk], ml_conv_b[0][ml_qk:]).reshape(1, QK_W)
    gla_qk = N_HEADS * DK
    w2_ext = jnp.zeros((GATE_W, 2 * gla_qk), F32)
    for dd in range(2):
        r0 = 4 * N_HEADS + dd * GLA_RANK
        w2_ext = w2_ext.at[r0:r0 + GLA_RANK, dd * gla_qk:(dd + 1) * gla_qk].set(gla_gate_w2[0, dd])
    b2_ext = gla_gate_b[0].reshape(1, 2 * gla_qk)
    router_w_p = jnp.concatenate([router_w[0], jnp.zeros((d, LANES - n_experts), F32)], axis=-1)
    router_b_p = jnp.concatenate([router_b[0], jnp.full((LANES - n_experts,), -jnp.inf, F32)]).reshape(1, LANES)

    cond = jnp.concatenate([c, c_ctx[None, :], jnp.zeros(((-bsz - 1) % 8, d), F32)], axis=0)
    mod = _adaln(cond, ada_w[0], ada_b[0]).reshape(cond.shape[0], N_MOD, 1, d)
    m_lat = mod[:bsz]
    m_ctx = mod[bsz:bsz + 1]

    pe = _grid_sincos(t_lat // GRID_W, d)
    g1 = norm1_g[0].reshape(1, d)
    tm = _row_tile(t_lat)
    zm_lat, zg_lat, gt_lat = _inproj(x, pe, m_lat[:, 1], m_lat[:, 0], g1, w_main, w_gate, tm)
    zm_ctx, zg_ctx, gt_ctx = _inproj(ctx, jnp.zeros((t_ctx, d), F32), m_ctx[:, 1], m_ctx[:, 0], g1,
                                     w_main, w_gate, _row_tile(t_ctx))

    a = _mlstm(zm_lat, zm_ctx, gt_lat, gt_ctx, conv_w, conv_b, gate_b, ml_norm_g[0].reshape(1, V_W))
    g = _gla(zg_lat, zg_ctx, gt_lat, gt_ctx, w2_ext, b2_ext, gla_norm_g[0].reshape(1, V_W))

    x1, u2, route = _outproj(x, pe, a, g, w_out[0].astype(BF16), m_lat[:, 2], m_lat[:, 4], m_lat[:, 3],
                             norm2_g[0].reshape(1, d), router_w_p, router_b_p, tm)

    return _moe(u2, route, x1, moe_w_gu[0], moe_b_gu[0][:, None, :], moe_w_down[0],
                moe_b_down[0][:, None, :], m_lat[:, 5], final_norm_g.reshape(1, d))
```

```python
import functools
import math

import numpy as np
import jax
import jax.numpy as jnp
from jax import lax
from jax.experimental import pallas as pl
from jax.experimental.pallas import tpu as pltpu

F32 = jnp.float32
BF16 = jnp.bfloat16

EPS = 1e-6
GRID_W = 64
POS_BASE = 10000.0
N_MOD = 6
N_HEADS = 4
DK = 64
DV = 128
QK_W = N_HEADS * 2 * DK
V_W = N_HEADS * DV
GROUP_W = QK_W + 2 * V_W
GATE_W = 128
ML_CONV = 3
GLA_RANK = 16
GLA_NORMALIZER = 16.0
TOP_K = 4
SWIGLU_LIMIT = 7.0
SWIGLU_ALPHA = 1.702
LANES = 128
SUBLANES = 8
BF16_ROWS = 16
ML_CHUNK = 128
GLA_CHUNK = 64
GLA_PREP_ROWS = 256
CONV_ROWS = 128
EXPERT_TILE = 512
ROUTE_BLOCK = 512
RUN_ALIGN = SUBLANES
PERM_ROWS = 256
RUN_SIZES = tuple(s for s in (ROUTE_BLOCK >> i for i in range(ROUTE_BLOCK.bit_length())) if s >= RUN_ALIGN)
VMEM_LIMIT = 56 * 1024 * 1024


def _params(sem):
    return pltpu.CompilerParams(dimension_semantics=sem, vmem_limit_bytes=VMEM_LIMIT)


def _sigmoid(x):
    return 1.0 / (1.0 + jnp.exp(-x))


def _log_sigmoid(x):
    return jnp.minimum(x, 0.0) - jnp.log1p(jnp.exp(-jnp.abs(x)))


def _dot(a, b):
    return jnp.dot(a, b, preferred_element_type=F32)


def _split2(x):
    hi = x.astype(BF16)
    lo = (x - hi.astype(F32)).astype(BF16)
    return hi, lo


def _dot3(a, b):
    a_hi, a_lo = _split2(a)
    b_hi, b_lo = _split2(b)
    return _dot(a_hi, b_hi) + _dot(a_lo, b_hi) + _dot(a_hi, b_lo)


def _dot_nt(a, b):
    return lax.dot_general(a, b, (((1,), (1,)), ((), ())), preferred_element_type=F32)


def _dot_tn(a, b):
    return lax.dot_general(a, b, (((0,), (0,)), ((), ())), preferred_element_type=F32)


def _adaln_body(c_ref, w_ref, b_ref, o_ref):
    c = c_ref[...]
    s = c * _sigmoid(c)
    o_ref[...] = _dot3(s, w_ref[...]) + b_ref[...]


def _adaln(cond, w, b):
    rows, d = cond.shape
    n = w.shape[1]
    tn = 512
    return pl.pallas_call(
        _adaln_body,
        grid=(n // tn,),
        in_specs=[pl.BlockSpec((rows, d), lambda j: (0, 0)),
                  pl.BlockSpec((d, tn), lambda j: (0, j)),
                  pl.BlockSpec((1, tn), lambda j: (0, j))],
        out_specs=pl.BlockSpec((rows, tn), lambda j: (0, j)),
        out_shape=jax.ShapeDtypeStruct((rows, n), F32),
        compiler_params=_params(("arbitrary",)),
        name="adaln",
    )(cond, w, b.reshape(1, n))


def _rms_mod(x, g, scale, shift):
    y = x * lax.rsqrt(jnp.mean(x * x, axis=-1, keepdims=True) + EPS)
    return (y * g) * (1.0 + scale) + shift


def _inproj_body(x_ref, pe_ref, sc_ref, sh_ref, g_ref, wm_ref, wg_ref, zm_ref, zg_ref, gt_ref):
    x = x_ref[0] + pe_ref[...]
    u = _rms_mod(x, g_ref[...], sc_ref[0], sh_ref[0]).astype(BF16)
    z = _dot(u, wm_ref[...])
    zm_ref[0] = z[:, :GROUP_W].astype(BF16)
    zg_ref[0] = z[:, GROUP_W:].astype(BF16)
    gt_ref[0] = _dot(u, wg_ref[...])


def _inproj(x, pe, scale, shift, g, w_main, w_gate, tm):
    bsz, t, d = x.shape
    per_batch = scale.shape[0] == bsz and bsz > 1
    mod_map = (lambda i, b: (b, 0, 0)) if per_batch else (lambda i, b: (0, 0, 0))
    return pl.pallas_call(
        _inproj_body,
        grid=(t // tm, bsz),
        in_specs=[pl.BlockSpec((1, tm, d), lambda i, b: (b, i, 0)),
                  pl.BlockSpec((tm, d), lambda i, b: (i, 0)),
                  pl.BlockSpec((1, 1, d), mod_map),
                  pl.BlockSpec((1, 1, d), mod_map),
                  pl.BlockSpec((1, d), lambda i, b: (0, 0)),
                  pl.BlockSpec((d, 2 * GROUP_W), lambda i, b: (0, 0)),
                  pl.BlockSpec((d, GATE_W), lambda i, b: (0, 0))],
        out_specs=[pl.BlockSpec((1, tm, GROUP_W), lambda i, b: (b, i, 0)),
                   pl.BlockSpec((1, tm, GROUP_W), lambda i, b: (b, i, 0)),
                   pl.BlockSpec((1, tm, GATE_W), lambda i, b: (b, i, 0))],
        out_shape=[jax.ShapeDtypeStruct((bsz, t, GROUP_W), BF16),
                   jax.ShapeDtypeStruct((bsz, t, GROUP_W), BF16),
                   jax.ShapeDtypeStruct((bsz, t, GATE_W), F32)],
        compiler_params=_params(("arbitrary", "arbitrary")),
        name="inproj",
    )(x, pe, scale, shift, g, w_main, w_gate)


def _scan_rows(x, op, ident, reverse, rowi):
    n = x.shape[0]
    k = 1
    while k < n:
        if reverse:
            shifted = jnp.where(rowi < n - k, pltpu.roll(x, n - k, axis=0), ident)
        else:
            shifted = jnp.where(rowi >= k, pltpu.roll(x, k, axis=0), ident)
        x = op(x, shifted)
        k *= 2
    return x


HEAD_DIRS = 2 * N_HEADS


def _mlstm_body(zl_ref, zc_ref, gl_ref, gc_ref, cw_ref, cb_ref, gb_ref, ng_ref, o_ref,
                q_s, kt_s, acc_s, cc_s, bc_s, st_s, mp_s, m_s, en_s, cr_s, wk_s, wc_s, c_s, *, t_lat, t_ctx):
    L = ML_CHUNK
    nc_ctx = t_ctx // L
    nc_lat = t_lat // L
    nc = nc_ctx + nc_lat
    scale = DK ** -0.5

    cw = cw_ref[...]
    cb = cb_ref[...]

    def conv_pass(z_ref, n, dst0):
        nb = n // CONV_ROWS
        row = lax.broadcasted_iota(jnp.int32, (CONV_ROWS, QK_W), 0)
        lane = lax.broadcasted_iota(jnp.int32, (CONV_ROWS, QK_W), 1)
        low = (lane % LANES) < DK

        def body(r, carry):
            r0 = pl.multiple_of(r * CONV_ROWS, CONV_ROWS)
            zc = z_ref[0, pl.ds(r0, CONV_ROWS), 0:QK_W].astype(F32)
            p0 = pl.multiple_of(jnp.maximum(r0 - BF16_ROWS, 0), BF16_ROWS)
            prev = z_ref[0, pl.ds(p0, BF16_ROWS), 0:QK_W].astype(F32)[BF16_ROWS - 1:BF16_ROWS]
            prev = jnp.where(r > 0, prev, 0.0)
            n0 = pl.multiple_of(jnp.minimum(r0 + CONV_ROWS, n - BF16_ROWS), BF16_ROWS)
            nxt = z_ref[0, pl.ds(n0, BF16_ROWS), 0:QK_W].astype(F32)[0:1]
            nxt = jnp.where(r < nb - 1, nxt, 0.0)
            up = jnp.where(row == 0, prev, pltpu.roll(zc, 1, axis=0))
            dn = jnp.where(row == CONV_ROWS - 1, nxt, pltpu.roll(zc, CONV_ROWS - 1, axis=0))
            y = cw[0:1] * up + cw[1:2] * zc + cw[2:3] * dn + cb
            y = y * _sigmoid(y)
            q = jnp.where(low, y * scale, 0.0)
            k = jnp.where(low, pltpu.roll(y, QK_W - DK, axis=1), 0.0)
            d0 = pl.multiple_of(dst0 + r0, CONV_ROWS)
            q_s[pl.ds(d0, CONV_ROWS), :] = q.astype(BF16)
            for h in range(N_HEADS):
                kt_s[h, :, pl.ds(d0, CONV_ROWS)] = k[:, h * LANES:(h + 1) * LANES].T.astype(BF16)
            return carry

        lax.fori_loop(0, nb, body, 0)

    conv_pass(zc_ref, t_ctx, 0)
    conv_pass(zl_ref, t_lat, t_ctx)

    rowi = lax.broadcasted_iota(jnp.int32, (L, LANES), 0)
    lane_g = lax.broadcasted_iota(jnp.int32, (L, GATE_W), 1)
    fwd_lane = lane_g < N_HEADS
    gb = gb_ref[...]

    def gate_pass(g_ref, nchunks, g0):
        def body(i, carry):
            r0 = pl.multiple_of(i * L, L)
            gates = g_ref[0, pl.ds(r0, L), :] + gb
            lf = pltpu.roll(_log_sigmoid(gates), GATE_W - HEAD_DIRS, axis=1)
            b = jnp.where(fwd_lane, _scan_rows(lf, jnp.add, 0.0, False, rowi),
                          _scan_rows(lf, jnp.add, 0.0, True, rowi))
            c = gates - b
            cc_s[g0 + i] = c
            bc_s[g0 + i] = b
            st_s[g0 + i, 0:1, :] = jnp.max(c, axis=0, keepdims=True)
            st_s[g0 + i, 1:2, :] = jnp.sum(lf, axis=0, keepdims=True)
            return carry

        lax.fori_loop(0, nchunks, body, 0)

    gate_pass(gc_ref, nc_ctx, 0)
    gate_pass(gl_ref, nc_lat, nc_ctx)

    lane1 = lax.broadcasted_iota(jnp.int32, (1, GATE_W), 1)
    fwd_order = list(range(nc))
    bwd_order = list(range(nc_ctx - 1, -1, -1)) + list(range(nc - 1, nc_ctx - 1, -1))
    mp = {}
    for d, order in enumerate((fwd_order, bwd_order)):
        m = jnp.zeros((1, GATE_W), F32)
        for g in order:
            mp[(d, g)] = m
            m = st_s[g, 1:2, :] + jnp.maximum(m, st_s[g, 0:1, :])
    for g in range(nc):
        mp_s[g, 0:1, :] = jnp.where(lane1 < N_HEADS, mp[(0, g)], mp[(1, g)])

    def ab_body(g, carry):
        c = cc_s[g]
        b = bc_s[g]
        mprev = mp_s[g, 0:1, :]
        mloc = jnp.where(fwd_lane, _scan_rows(c, jnp.maximum, -jnp.inf, False, rowi),
                         _scan_rows(c, jnp.maximum, -jnp.inf, True, rowi))
        m = jnp.maximum(mloc, mprev)
        mlast = jnp.maximum(mprev, st_s[g, 0:1, :])
        m_s[g] = m
        en_s[g] = jnp.exp(-(b + m))
        cr_s[g] = c.T[0:HEAD_DIRS]
        wk_s[g] = jnp.exp((c - mlast).T[0:HEAD_DIRS])
        wc_s[g] = jnp.exp(jnp.broadcast_to(mprev - mlast, (L, GATE_W)).T[0:HEAD_DIRS])
        return carry

    lax.fori_loop(0, nc, ab_body, 0)

    colL = lax.broadcasted_iota(jnp.int32, (L, L), 1)
    rowL = lax.broadcasted_iota(jnp.int32, (L, L), 0)
    masks = (colL <= rowL, colL >= rowL)
    ones_blk = jnp.ones((L, DV), BF16)
    c_s[...] = jnp.zeros(c_s.shape, F32)

    def twice(x):
        return jnp.concatenate([x, x], axis=1)

    def step(g, z_ref, r_src, r_q, d, with_out, r_out):
        for h in range(N_HEADS):
            j = d * N_HEADS + h
            hs = slice(h * DV, (h + 1) * DV)
            kt = kt_s[h, :, pl.ds(r_q, L)]
            v = z_ref[0, pl.ds(r_src, L), QK_W + h * DV:QK_W + (h + 1) * DV]
            vext = jnp.concatenate([v, ones_blk], axis=1)
            state = c_s[j]
            if with_out:
                q = q_s[pl.ds(r_q, L), hs]
                mb = jnp.broadcast_to(m_s[g][:, j:j + 1], (L, L))
                w = jnp.where(masks[d], jnp.exp(cr_s[g, j:j + 1, :] - mb), 0.0)
                w_int = jnp.exp(mp_s[g][0:1, j:j + 1] - mb)
                p = jnp.concatenate([(_dot(q, kt) * w).astype(BF16), (q.astype(F32) * w_int).astype(BF16)], axis=1)
                nd = _dot(p, jnp.concatenate([vext, state.astype(BF16)], axis=0))
                hh = nd[:, :DV] / jnp.maximum(jnp.abs(nd[:, DV:]), en_s[g][:, j:j + 1])
                acc_s[d, pl.ds(r_out, L), hs] = hh
            ktw = (kt.astype(F32) * wk_s[g, j:j + 1, :]).astype(BF16)
            c_s[j] = twice(wc_s[g, j:j + 1, :]) * state + _dot(ktw, vext)

    def ctx_body(i, carry):
        for d in (0, 1):
            ii = i if d == 0 else nc_ctx - 1 - i
            r0 = pl.multiple_of(ii * L, L)
            step(ii, zc_ref, r0, r0, d, False, None)
        return carry

    def lat_body(i, carry):
        for d in (0, 1):
            ii = i if d == 0 else nc_lat - 1 - i
            r0 = pl.multiple_of(ii * L, L)
            step(nc_ctx + ii, zl_ref, r0, pl.multiple_of(t_ctx + ii * L, L), d, True, r0)
        return carry

    lax.fori_loop(0, nc_ctx, ctx_body, 0)
    lax.fori_loop(0, nc_lat, lat_body, 0)

    def out_body(i, carry):
        r0 = pl.multiple_of(i * L, L)
        for h in range(N_HEADS):
            hs = slice(h * DV, (h + 1) * DV)
            tot = acc_s[0, pl.ds(r0, L), hs] + acc_s[1, pl.ds(r0, L), hs]
            tot = tot * lax.rsqrt(jnp.mean(tot * tot, axis=-1, keepdims=True) + EPS)
            og = zl_ref[0, pl.ds(r0, L), QK_W + V_W + h * DV:QK_W + V_W + (h + 1) * DV].astype(F32)
            o_ref[0, pl.ds(r0, L), hs] = (tot * ng_ref[:, hs] * _sigmoid(og)).astype(BF16)
        return carry

    lax.fori_loop(0, nc_lat, out_body, 0)


def _mlstm(zm_lat, zm_ctx, gt_lat, gt_ctx, conv_w, conv_b, gate_b, norm_g):
    bsz, t_lat, _ = zm_lat.shape
    t_ctx = zm_ctx.shape[1]
    nc = (t_lat + t_ctx) // ML_CHUNK
    body = functools.partial(_mlstm_body, t_lat=t_lat, t_ctx=t_ctx)
    full = lambda b: (0, 0)
    return pl.pallas_call(
        body,
        grid=(bsz,),
        in_specs=[pl.BlockSpec((1, t_lat, GROUP_W), lambda b: (b, 0, 0)),
                  pl.BlockSpec((1, t_ctx, GROUP_W), lambda b: (b, 0, 0)),
                  pl.BlockSpec((1, t_lat, GATE_W), lambda b: (b, 0, 0)),
                  pl.BlockSpec((1, t_ctx, GATE_W), lambda b: (b, 0, 0)),
                  pl.BlockSpec((ML_CONV, QK_W), full),
                  pl.BlockSpec((1, QK_W), full),
                  pl.BlockSpec((1, GATE_W), full),
                  pl.BlockSpec((1, V_W), full)],
        out_specs=pl.BlockSpec((1, t_lat, V_W), lambda b: (b, 0, 0)),
        out_shape=jax.ShapeDtypeStruct((bsz, t_lat, V_W), BF16),
        scratch_shapes=[pltpu.VMEM((t_ctx + t_lat, QK_W), BF16),
                        pltpu.VMEM((N_HEADS, LANES, t_ctx + t_lat), BF16),
                        pltpu.VMEM((2, t_lat, V_W), F32),
                        pltpu.VMEM((nc, ML_CHUNK, GATE_W), F32),
                        pltpu.VMEM((nc, ML_CHUNK, GATE_W), F32),
                        pltpu.VMEM((nc, SUBLANES, GATE_W), F32),
                        pltpu.VMEM((nc, SUBLANES, GATE_W), F32),
                        pltpu.VMEM((nc, ML_CHUNK, GATE_W), F32),
                        pltpu.VMEM((nc, ML_CHUNK, GATE_W), F32),
                        pltpu.VMEM((nc, HEAD_DIRS, ML_CHUNK), F32),
                        pltpu.VMEM((nc, HEAD_DIRS, ML_CHUNK), F32),
                        pltpu.VMEM((nc, HEAD_DIRS, GATE_W), F32),
                        pltpu.VMEM((HEAD_DIRS, 2 * DK, 2 * DV), F32)],
        compiler_params=_params(("arbitrary",)),
        name="mlstm",
    )(zm_lat, zm_ctx, gt_lat, gt_ctx, conv_w, conv_b, gate_b, norm_g)


def _gla_body(zl_ref, zc_ref, gl_ref, gc_ref, w2_ref, b2_ref, ng_ref, o_ref,
              acc_s, s_s, qt_s, kt_s, ee_s, *, t_lat, t_ctx):
    L = GLA_CHUNK
    nc_ctx = t_ctx // L
    nc_lat = t_lat // L
    scale = DK ** -0.5
    half = N_HEADS * DK

    rowi = lax.broadcasted_iota(jnp.int32, (L, L), 0)
    coli = lax.broadcasted_iota(jnp.int32, (L, L), 1)
    masks = (coli <= rowi, coli >= rowi)
    tri_b = (masks[0].astype(BF16), masks[1].astype(BF16))
    low = lax.broadcasted_iota(jnp.int32, (L, LANES), 1) < DK
    w2_hi, w2_lo = _split2(w2_ref[...])
    bias2 = b2_ref[...]

    def prep(z_ref, g_ref, n_rows, c0, row0):
        grp = math.gcd(n_rows, GLA_PREP_ROWS)

        def body(gi, carry):
            rg = pl.multiple_of(gi * grp, grp)
            g_hi, g_lo = _split2(g_ref[0, pl.ds(rg, grp), :])
            pre = _dot(g_hi, w2_hi) + _dot(g_lo, w2_hi) + _dot(g_hi, w2_lo) + bias2
            la_hi, la_lo = _split2(_log_sigmoid(pre) * (1.0 / GLA_NORMALIZER))
            for cc in range(grp // L):
                rs = slice(cc * L, (cc + 1) * L)
                r0 = pl.multiple_of(rg + cc * L, L)
                rq = pl.multiple_of(row0 + rg + cc * L, L)
                c = c0 + gi * (grp // L) + cc
                for d in (0, 1):
                    ds_ = slice(d * half, (d + 1) * half)
                    b_all = _dot(tri_b[d], la_hi[rs, ds_]) + _dot(tri_b[d], la_lo[rs, ds_])
                    for hp in range(N_HEADS // 2):
                        pair = b_all[:, hp * LANES:(hp + 1) * LANES]
                        swapped = pltpu.roll(pair, DK, axis=1)
                        for hh in range(2):
                            h = 2 * hp + hh
                            hs = slice(h * LANES, (h + 1) * LANES)
                            b2 = jnp.where(low, pair, swapped) if hh == 0 else jnp.where(low, swapped, pair)
                            b_end = b2[L - 1:L] if d == 0 else b2[0:1]
                            qk = z_ref[0, pl.ds(r0, L), hs].astype(F32)
                            qkt = qk * jnp.exp(jnp.where(low, b2, -b2))
                            qt_s[d, pl.ds(rq, L), hs] = jnp.where(low, qkt * scale, 0.0).astype(BF16)
                            kt_s[d, pl.ds(rq, L), hs] = jnp.where(low, pltpu.roll(qkt, DK, axis=1), 0.0).astype(BF16)
                            ee_s[d, c, 0:1, hs] = jnp.exp(b_end)
            return carry

        lax.fori_loop(0, n_rows // grp, body, 0)

    prep(zc_ref, gc_ref, t_ctx, 0, 0)
    prep(zl_ref, gl_ref, t_lat, nc_ctx, t_ctx)

    s_s[...] = jnp.zeros(s_s.shape, F32)

    def step(z_ref, r0, rq, c, d, with_out):
        for h in range(N_HEADS):
            j = d * N_HEADS + h
            hs = slice(h * LANES, (h + 1) * LANES)
            qt = qt_s[d, pl.ds(rq, L), hs]
            kt = kt_s[d, pl.ds(rq, L), hs]
            v = z_ref[0, pl.ds(r0, L), QK_W + h * DV:QK_W + (h + 1) * DV]
            state = s_s[j]
            if with_out:
                a = jnp.where(masks[d], _dot_nt(qt, kt), 0.0).astype(BF16)
                acc_s[d, pl.ds(r0, L), hs] = _dot(a, v) + _dot_nt(qt, state.astype(BF16))
            e_end = ee_s[d, c, 0:1, hs]
            s_s[j] = state * e_end + _dot_tn(v, (kt.astype(F32) * e_end).astype(BF16))

    def loop(z_ref, n, c0, row0, with_out):
        def body(i, carry):
            for d in (0, 1):
                ii = i if d == 0 else n - 1 - i
                r0 = pl.multiple_of(ii * L, L)
                step(z_ref, r0, pl.multiple_of(row0 + ii * L, L), c0 + ii, d, with_out)
            return carry
        lax.fori_loop(0, n, body, 0, unroll=2)

    loop(zc_ref, nc_ctx, 0, 0, False)
    loop(zl_ref, nc_lat, nc_ctx, t_ctx, True)

    def out_body(i, carry):
        r0 = pl.multiple_of(i * L, L)
        for h in range(N_HEADS):
            hs = slice(h * LANES, (h + 1) * LANES)
            tot = acc_s[0, pl.ds(r0, L), hs] + acc_s[1, pl.ds(r0, L), hs]
            tot = tot * lax.rsqrt(jnp.mean(tot * tot, axis=-1, keepdims=True) + EPS)
            gg = zl_ref[0, pl.ds(r0, L), QK_W + V_W + h * DV:QK_W + V_W + (h + 1) * DV].astype(F32)
            o_ref[0, pl.ds(r0, L), hs] = (tot * ng_ref[:, hs] * (gg * _sigmoid(gg))).astype(BF16)
        return carry

    lax.fori_loop(0, nc_lat, out_body, 0)


def _gla(zg_lat, zg_ctx, gt_lat, gt_ctx, w2_ext, b2_ext, norm_g):
    bsz, t_lat, _ = zg_lat.shape
    t_ctx = zg_ctx.shape[1]
    body = functools.partial(_gla_body, t_lat=t_lat, t_ctx=t_ctx)
    full = lambda b: (0, 0)
    return pl.pallas_call(
        body,
        grid=(bsz,),
        in_specs=[pl.BlockSpec((1, t_lat, GROUP_W), lambda b: (b, 0, 0)),
                  pl.BlockSpec((1, t_ctx, GROUP_W), lambda b: (b, 0, 0)),
                  pl.BlockSpec((1, t_lat, GATE_W), lambda b: (b, 0, 0)),
                  pl.BlockSpec((1, t_ctx, GATE_W), lambda b: (b, 0, 0)),
                  pl.BlockSpec((GATE_W, 2 * N_HEADS * DK), full),
                  pl.BlockSpec((1, 2 * N_HEADS * DK), full),
                  pl.BlockSpec((1, V_W), full)],
        out_specs=pl.BlockSpec((1, t_lat, V_W), lambda b: (b, 0, 0)),
        out_shape=jax.ShapeDtypeStruct((bsz, t_lat, V_W), BF16),
        scratch_shapes=[pltpu.VMEM((2, t_lat, V_W), F32),
                        pltpu.VMEM((HEAD_DIRS, DV, LANES), F32),
                        pltpu.VMEM((2, t_ctx + t_lat, QK_W), BF16),
                        pltpu.VMEM((2, t_ctx + t_lat, QK_W), BF16),
                        pltpu.VMEM((2, (t_ctx + t_lat) // GLA_CHUNK, SUBLANES, QK_W), F32)],
        compiler_params=_params(("arbitrary",)),
        name="gla",
    )(zg_lat, zg_ctx, gt_lat, gt_ctx, w2_ext, b2_ext, norm_g)


def _outproj_body(x_ref, pe_ref, a_ref, g_ref, wo_ref, g1_ref, sc_ref, sh_ref, n2_ref, rw_ref, rb_ref,
                  x1_ref, u2_ref, route_ref, cnt_ref):
    x = x_ref[0] + pe_ref[...]
    y = _dot(a_ref[0], wo_ref[0:V_W, :]) + _dot(g_ref[0], wo_ref[V_W:2 * V_W, :])
    x1 = x + g1_ref[0] * y
    x1_ref[0] = x1
    u2 = _rms_mod(x1, n2_ref[...], sc_ref[0], sh_ref[0])
    u2_ref[0] = u2
    u_hi, u_lo = _split2(u2)
    w_hi, w_lo = _split2(rw_ref[...])
    hh = _dot(u_hi, jnp.concatenate([w_hi, w_lo], axis=1))
    logits = hh[:, :LANES] + _dot(u_lo, w_hi) + hh[:, LANES:] + rb_ref[...]
    lane = lax.broadcasted_iota(jnp.int32, logits.shape, 1).astype(F32)
    ids = []
    vals = []
    for _ in range(TOP_K):
        m = jnp.max(logits, axis=1, keepdims=True)
        idx = jnp.min(jnp.where(logits == m, lane, float(LANES)), axis=1, keepdims=True)
        ids.append(idx)
        vals.append(m)
        logits = jnp.where(lane == idx, -jnp.inf, logits)
    exps = [jnp.exp(v - vals[0]) for v in vals]
    denom = exps[0]
    for e in exps[1:]:
        denom = denom + e
    route = jnp.zeros(logits.shape, F32)
    for k in range(TOP_K):
        route = jnp.where(lane == float(k), ids[k], route)
        route = jnp.where(lane == float(TOP_K + k), exps[k] / denom, route)
    route_ref[0] = route

    @pl.when((pl.program_id(0) == 0) & (pl.program_id(1) == 0))
    def _():
        cnt_ref[...] = jnp.zeros(cnt_ref.shape, F32)

    cnt_ref[...] += _block_picks(route)[3]


def _outproj(x, pe, a, g, w_out, gate1, scale2, shift2, norm2_g, router_w, router_b, tm):
    bsz, t, d = x.shape
    tok = lambda i, b: (b, i, 0)
    mod = lambda i, b: (b, 0, 0)
    full = lambda i, b: (0, 0)
    return pl.pallas_call(
        _outproj_body,
        grid=(t // tm, bsz),
        in_specs=[pl.BlockSpec((1, tm, d), tok),
                  pl.BlockSpec((tm, d), lambda i, b: (i, 0)),
                  pl.BlockSpec((1, tm, V_W), tok),
                  pl.BlockSpec((1, tm, V_W), tok),
                  pl.BlockSpec((2 * V_W, d), full),
                  pl.BlockSpec((1, 1, d), mod),
                  pl.BlockSpec((1, 1, d), mod),
                  pl.BlockSpec((1, 1, d), mod),
                  pl.BlockSpec((1, d), full),
                  pl.BlockSpec((d, LANES), full),
                  pl.BlockSpec((1, LANES), full)],
        out_specs=[pl.BlockSpec((1, tm, d), tok),
                   pl.BlockSpec((1, tm, d), tok),
                   pl.BlockSpec((1, tm, LANES), tok),
                   pl.BlockSpec((1, LANES), full)],
        out_shape=[jax.ShapeDtypeStruct((bsz, t, d), F32),
                   jax.ShapeDtypeStruct((bsz, t, d), F32),
                   jax.ShapeDtypeStruct((bsz, t, LANES), F32),
                   jax.ShapeDtypeStruct((1, LANES), F32)],
        compiler_params=_params(("arbitrary", "arbitrary")),
        name="outproj_router",
    )(x, pe, a, g, w_out, gate1, scale2, shift2, norm2_g, router_w, router_b)


def _lane_excl_cumsum(x):
    x8 = jnp.broadcast_to(x, (SUBLANES, LANES))
    lane8 = lax.broadcasted_iota(jnp.int32, (SUBLANES, LANES), 1)
    inc = x8
    s = 1
    while s < LANES:
        inc = inc + jnp.where(lane8 >= s, pltpu.roll(inc, s, axis=1), 0.0)
        s *= 2
    return (inc - x8)[0:1]


def _block_picks(route):
    lane = lax.broadcasted_iota(jnp.int32, route.shape, 1).astype(F32)
    onehots = [lane == route[:, k:k + 1] for k in range(TOP_K)]
    oh = onehots[0].astype(F32)
    for o in onehots[1:]:
        oh = oh + o.astype(F32)
    run_len = jnp.ceil(jnp.sum(oh, axis=0, keepdims=True) * (1.0 / RUN_ALIGN)) * float(RUN_ALIGN)
    return lane, onehots, oh, run_len


def _route_pos_body(route_ref, cnt_ref, lic_ref, lit_ref, tab_ref, off_s, *, tile):
    tb = route_ref.shape[0]
    lane, onehots, oh, run_len = _block_picks(route_ref[...])

    @pl.when(pl.program_id(0) == 0)
    def _():
        cpad = jnp.floor((cnt_ref[...] + float(tile - 1)) * (1.0 / tile)) * float(tile)
        off_s[...] = _lane_excl_cumsum(cpad)

    r = lax.broadcasted_iota(jnp.int32, (tb, tb), 0)
    c = lax.broadcasted_iota(jnp.int32, (tb, tb), 1)
    before = _dot((c < r).astype(BF16), oh.astype(BF16))
    lstart = _lane_excl_cumsum(run_len)
    local = lstart + before
    li = jnp.zeros((tb, LANES), F32)
    for k in range(TOP_K):
        lk = jnp.sum(jnp.where(onehots[k], local, 0.0), axis=1, keepdims=True)
        li = jnp.where(lane == float(k), lk, li)
    lic_ref[...] = li
    lit_ref[0] = li.T[0:SUBLANES]
    row8 = lax.broadcasted_iota(jnp.int32, (SUBLANES, LANES), 0)
    total = jnp.sum(run_len, axis=1, keepdims=True)
    tab = jnp.where(row8 == 0, off_s[...],
                    jnp.where(row8 == 1, run_len, jnp.where(row8 == 2, lstart, jnp.where(row8 == 3, total, 0.0))))
    tab_ref[0] = tab.astype(jnp.int32)
    off_s[...] += run_len


def _route_pos(route, counts, tile):
    n = route.shape[0]
    tb = ROUTE_BLOCK
    nb = n // tb
    return pl.pallas_call(
        functools.partial(_route_pos_body, tile=tile),
        grid=(nb,),
        in_specs=[pl.BlockSpec((tb, LANES), lambda j: (j, 0)),
                  pl.BlockSpec((1, LANES), lambda j: (0, 0))],
        out_specs=[pl.BlockSpec((tb, LANES), lambda j: (j, 0)),
                   pl.BlockSpec((1, SUBLANES, tb), lambda j: (j, 0, 0)),
                   pl.BlockSpec((1, SUBLANES, LANES), lambda j: (j, 0, 0))],
        out_shape=[jax.ShapeDtypeStruct((n, LANES), F32),
                   jax.ShapeDtypeStruct((nb, SUBLANES, tb), F32),
                   jax.ShapeDtypeStruct((nb, SUBLANES, LANES), jnp.int32)],
        scratch_shapes=[pltpu.VMEM((1, LANES), F32)],
        compiler_params=_params(("arbitrary",)),
        name="route_pos",
    )(route, counts)


def _issue_runs(tab_ref, n_experts, make_copy):
    def per_expert(e, carry):
        g = tab_ref[0, 0, e]
        cnt = tab_ref[0, 1, e]
        l = tab_ref[0, 2, e]
        for size in RUN_SIZES:
            part = cnt & size

            @pl.when(part != 0)
            def _():
                make_copy(pl.multiple_of(l, RUN_ALIGN), pl.multiple_of(g, RUN_ALIGN), size).start()

            l = l + part
            g = g + part
        return carry

    lax.fori_loop(0, n_experts, per_expert, 0)


def _dispatch_body(zrow_ref, zflag_ref, nu_ref, tab_ref, lit_ref, u_ref, xs_ref, buf, zero_s, tot_s, sem, zsem,
                   *, n_experts, tile):
    i = pl.program_id(0)
    nb = pl.num_programs(0)
    tb = u_ref.shape[0]
    rows = buf.shape[1]
    n_tiles = xs_ref.shape[0] // tile
    slot = i % 2

    @pl.when(i == 0)
    def _():
        zero_s[...] = jnp.zeros(zero_s.shape, F32)

        def zero_tile(r0):
            cp = pltpu.make_async_copy(zero_s, xs_ref.at[pl.ds(pl.multiple_of(r0, tile), tile)], zsem)
            cp.start()
            cp.wait()

        for e in range(n_experts):
            @pl.when(zflag_ref[e] == 1)
            def _():
                zero_tile(zrow_ref[e])

        def tail(t, carry):
            zero_tile(t * tile)
            return carry

        lax.fori_loop(nu_ref[0], n_tiles, tail, 0)

    def wait_slot(s):
        nrows = pl.multiple_of(tot_s[s], RUN_ALIGN)
        pltpu.make_async_copy(buf.at[s, pl.ds(0, nrows)], xs_ref.at[pl.ds(0, nrows)], sem.at[s]).wait()

    @pl.when(i >= 2)
    def _():
        wait_slot(slot)

    tot_s[slot] = tab_ref[0, 3, 0]

    lit = lit_ref[0]
    ub = u_ref[...].astype(BF16)
    ri0 = lax.broadcasted_iota(jnp.int32, (PERM_ROWS, tb), 0).astype(F32)
    for c in range(rows // PERM_ROWS):
        ri = ri0 + float(c * PERM_ROWS)
        hit = jnp.zeros((PERM_ROWS, tb), F32)
        for k in range(TOP_K):
            hit = jnp.where(ri == lit[k:k + 1], 1.0, hit)
        buf[slot, c * PERM_ROWS:(c + 1) * PERM_ROWS] = _dot(hit.astype(BF16), ub)
    _issue_runs(tab_ref, n_experts,
                lambda l, g, size: pltpu.make_async_copy(buf.at[slot, pl.ds(l, size)], xs_ref.at[pl.ds(g, size)],
                                                         sem.at[slot]))

    @pl.when(i == nb - 1)
    def _():
        wait_slot(slot)

    @pl.when((i == nb - 1) & (nb > 1))
    def _():
        wait_slot(1 - slot)


def _dispatch(u2, tab, lit, zrow, zflag, n_used, n_rows, tile):
    n, d = u2.shape
    tb = ROUTE_BLOCK
    nb = n // tb
    n_experts = zrow.shape[0]
    grid_spec = pltpu.PrefetchScalarGridSpec(
        num_scalar_prefetch=3,
        grid=(nb,),
        in_specs=[pl.BlockSpec((1, SUBLANES, LANES), lambda i, zr, zf, nu: (i, 0, 0), memory_space=pltpu.SMEM),
                  pl.BlockSpec((1, SUBLANES, tb), lambda i, zr, zf, nu: (i, 0, 0)),
                  pl.BlockSpec((tb, d), lambda i, zr, zf, nu: (i, 0))],
        out_specs=pl.BlockSpec(memory_space=pl.ANY),
        scratch_shapes=[pltpu.VMEM((2, tb * TOP_K + n_experts * RUN_ALIGN, d), F32), pltpu.VMEM((tile, d), F32),
                        pltpu.SMEM((2,), jnp.int32), pltpu.SemaphoreType.DMA((2,)), pltpu.SemaphoreType.DMA(())],
    )
    return pl.pallas_call(
        functools.partial(_dispatch_body, n_experts=n_experts, tile=tile),
        grid_spec=grid_spec,
        out_shape=jax.ShapeDtypeStruct((n_rows, d), F32),
        compiler_params=_params(("arbitrary",)),
        name="dispatch",
    )(zrow, zflag, n_used, tab, lit, u2)


def _experts_body(te_ref, nu_ref, x_ref, wgu_ref, bgu_ref, wd_ref, bd_ref, y_ref, wgu_s, wd_s, *, d_expert):
    t = pl.program_id(0)

    @pl.when((t == 0) | (te_ref[t] != te_ref[jnp.maximum(t - 1, 0)]))
    def _():
        wgu_s[...] = wgu_ref[0].astype(BF16)
        wd_s[...] = wd_ref[0].astype(BF16)

    @pl.when(t < nu_ref[0])
    def _():
        gu = _dot(x_ref[...].astype(BF16), wgu_s[...]) + bgu_ref[0]
        gate = jnp.minimum(gu[:, :d_expert], SWIGLU_LIMIT)
        up = jnp.clip(gu[:, d_expert:], -SWIGLU_LIMIT, SWIGLU_LIMIT)
        act = ((up + 1.0) * (gate * _sigmoid(SWIGLU_ALPHA * gate))).astype(BF16)
        y_ref[...] = _dot(act, wd_s[...]) + bd_ref[0]

    @pl.when(t >= nu_ref[0])
    def _():
        y_ref[...] = jnp.zeros(y_ref.shape, F32)


def _experts(xs, tile_expert, n_used, w_gu, b_gu, w_down, b_down, tile):
    n_rows, d = xs.shape
    n_experts, _, two_de = w_gu.shape
    d_expert = two_de // 2
    grid_spec = pltpu.PrefetchScalarGridSpec(
        num_scalar_prefetch=2,
        grid=(n_rows // tile,),
        in_specs=[pl.BlockSpec((tile, d), lambda t, te, nu: (jnp.minimum(t, nu[0] - 1), 0)),
                  pl.BlockSpec((1, d, two_de), lambda t, te, nu: (te[t], 0, 0)),
                  pl.BlockSpec((1, 1, two_de), lambda t, te, nu: (te[t], 0, 0)),
                  pl.BlockSpec((1, d_expert, d), lambda t, te, nu: (te[t], 0, 0)),
                  pl.BlockSpec((1, 1, d), lambda t, te, nu: (te[t], 0, 0))],
        out_specs=pl.BlockSpec((tile, d), lambda t, te, nu: (t, 0)),
        scratch_shapes=[pltpu.VMEM((d, two_de), BF16), pltpu.VMEM((d_expert, d), BF16)],
    )
    return pl.pallas_call(
        functools.partial(_experts_body, d_expert=d_expert),
        grid_spec=grid_spec,
        out_shape=jax.ShapeDtypeStruct((n_rows, d), F32),
        compiler_params=_params(("arbitrary",)),
        name="experts",
    )(tile_expert, n_used, xs, w_gu, b_gu, w_down, b_down)


def _combine_body(tab_ref, tabn_ref, lic_ref, route_ref, x1_ref, g5_ref, fg_ref, ys_ref, o_ref, buf, sem,
                  *, n_experts):
    i = pl.program_id(0)
    n = pl.num_programs(0)
    tb = x1_ref.shape[0]
    rows = buf.shape[1]
    slot = i % 2

    def issue(t_ref, s):
        _issue_runs(t_ref, n_experts,
                    lambda l, g, size: pltpu.make_async_copy(ys_ref.at[pl.ds(g, size)], buf.at[s, pl.ds(l, size)],
                                                             sem.at[s]))

    @pl.when(i == 0)
    def _():
        buf[...] = jnp.zeros(buf.shape, F32)
        issue(tab_ref, 0)

    @pl.when(i + 1 < n)
    def _():
        issue(tabn_ref, 1 - slot)

    nrows = pl.multiple_of(tab_ref[0, 3, 0], RUN_ALIGN)
    pltpu.make_async_copy(ys_ref.at[pl.ds(0, nrows)], buf.at[slot, pl.ds(0, nrows)], sem.at[slot]).wait()

    lane = lax.broadcasted_iota(jnp.int32, (tb, LANES), 1)
    li_hi, li_lo = _split2(lic_ref[...])
    a = jnp.where(lane < TOP_K, li_hi.astype(F32),
                  jnp.where(lane < 2 * TOP_K, pltpu.roll(li_lo.astype(F32), TOP_K, axis=1),
                            jnp.where(lane < 3 * TOP_K, pltpu.roll(route_ref[...], TOP_K, axis=1), 0.0))).astype(BF16)
    rsel = lax.broadcasted_iota(jnp.int32, (LANES, LANES), 0)
    lane_f = lane.astype(F32)
    idx, gate = [], []
    for k in range(TOP_K):
        sel = jnp.concatenate([((rsel == k) | (rsel == TOP_K + k)).astype(BF16),
                               (rsel == 2 * TOP_K + k).astype(BF16)], axis=1)
        bk = _dot(a, sel)
        idx.append(bk[:, :LANES])
        gate.append(bk[:, LANES:])
    moe = jnp.zeros(x1_ref.shape, F32)
    for c0 in range(0, rows, PERM_ROWS):
        blocks = []
        for c in range(c0, c0 + PERM_ROWS, LANES):
            col = lane_f + float(c)
            w = jnp.zeros((tb, LANES), F32)
            for k in range(TOP_K):
                w = jnp.where(idx[k] == col, gate[k], w)
            blocks.append(w.astype(BF16))
        moe = moe + _dot(jnp.concatenate(blocks, axis=1), buf[slot, c0:c0 + PERM_ROWS].astype(BF16))
    xo = x1_ref[...] + g5_ref[0] * moe
    o_ref[...] = xo * lax.rsqrt(jnp.mean(xo * xo, axis=-1, keepdims=True) + EPS) * fg_ref[...]


def _combine(ys, tab, lic, route, x1, gate5, final_g, t_lat, n_experts):
    n, d = x1.shape
    tb = ROUTE_BLOCK
    nb = n // tb
    per_b = t_lat // tb
    return pl.pallas_call(
        functools.partial(_combine_body, n_experts=n_experts),
        grid=(nb,),
        in_specs=[pl.BlockSpec((1, SUBLANES, LANES), lambda i: (i, 0, 0), memory_space=pltpu.SMEM),
                  pl.BlockSpec((1, SUBLANES, LANES), lambda i: (jnp.minimum(i + 1, nb - 1), 0, 0),
                               memory_space=pltpu.SMEM),
                  pl.BlockSpec((tb, LANES), lambda i: (i, 0)),
                  pl.BlockSpec((tb, LANES), lambda i: (i, 0)),
                  pl.BlockSpec((tb, d), lambda i: (i, 0)),
                  pl.BlockSpec((1, 1, d), lambda i: (i // per_b, 0, 0)),
                  pl.BlockSpec((1, d), lambda i: (0, 0)),
                  pl.BlockSpec(memory_space=pl.ANY)],
        out_specs=pl.BlockSpec((tb, d), lambda i: (i, 0)),
        out_shape=jax.ShapeDtypeStruct((n, d), F32),
        scratch_shapes=[pltpu.VMEM((2, tb * TOP_K + n_experts * RUN_ALIGN, d), F32), pltpu.SemaphoreType.DMA((2,))],
        compiler_params=_params(("arbitrary",)),
        name="combine",
    )(tab, tab, lic, route, x1, gate5, final_g, ys)


def _moe(u2, route, counts, x1, w_gu, b_gu, w_down, b_down, gate5, final_g):
    bsz, t_lat, d = x1.shape
    n = bsz * t_lat
    n_experts = w_gu.shape[0]
    tile = EXPERT_TILE
    route2 = route.reshape(n, LANES)
    lic, lit, tab = _route_pos(route2, counts, tile)
    cnt = counts[0, :n_experts].astype(jnp.int32)
    cpad = ((cnt + tile - 1) // tile) * tile
    ends = jnp.cumsum(cpad)
    n_tiles = -(-(n * TOP_K + (n // ROUTE_BLOCK) * n_experts * RUN_ALIGN) // tile) + n_experts
    n_rows = n_tiles * tile
    tile_expert = jnp.minimum(
        jnp.sum(jnp.arange(n_tiles, dtype=jnp.int32)[:, None] * tile >= ends[None, :], axis=1), n_experts - 1
    ).astype(jnp.int32)
    n_used = (ends[-1:] // tile).astype(jnp.int32)
    zrow = jnp.maximum(ends - tile, 0).astype(jnp.int32)
    zflag = (cnt % tile != 0).astype(jnp.int32)
    xs = _dispatch(u2.reshape(n, d), tab, lit, zrow, zflag, n_used, n_rows, tile)
    ys = _experts(xs, tile_expert, n_used, w_gu, b_gu, w_down, b_down, tile)
    out = _combine(ys, tab, lic, route2, x1.reshape(n, d), gate5, final_g, t_lat, n_experts)
    return out.reshape(bsz, t_lat, d)


def _grid_sincos(rows, d):
    nf = d // 4
    omega = 1.0 / (POS_BASE ** (jnp.arange(nf, dtype=F32) / nf))
    r = jnp.broadcast_to(jnp.arange(rows, dtype=F32)[:, None, None] * omega, (rows, GRID_W, nf))
    cl = jnp.broadcast_to(jnp.arange(GRID_W, dtype=F32)[None, :, None] * omega, (rows, GRID_W, nf))
    pe = jnp.concatenate([jnp.sin(r), jnp.cos(r), jnp.sin(cl), jnp.cos(cl)], axis=-1)
    return pe.reshape(rows * GRID_W, d)


def _pack_qk(wq, wk):
    lead = wq.shape[:-1]
    q = wq.reshape(lead + (N_HEADS, DK))
    k = wk.reshape(lead + (N_HEADS, DK))
    return jnp.concatenate([q, k], axis=-1).reshape(lead + (QK_W,))


def _row_tile(t):
    for tm in (512, 256, 128):
        if t % tm == 0:
            return tm
    raise ValueError(f"sequence length {t} must be a multiple of 128")


def kernel(x, c, ctx, c_ctx, ada_w, ada_b, norm1_g, w_in, ml_conv_w, ml_conv_b, ml_gate_b, ml_norm_g,
           gla_gate_w2, gla_gate_b, gla_norm_g, w_out, norm2_g, router_w, router_b, moe_w_gu, moe_b_gu,
           moe_w_down, moe_b_down, final_norm_g):
    bsz, t_lat, d = x.shape
    t_ctx = ctx.shape[1]
    assert ada_w.shape[0] == 1, "single-layer block"
    assert t_lat % ML_CHUNK == 0 and t_ctx % ML_CHUNK == 0 and t_lat % GRID_W == 0 and t_lat % ROUTE_BLOCK == 0
    n_experts = router_w.shape[-1]
    assert n_experts <= LANES

    ml_qk, ml_v = N_HEADS * DK, N_HEADS * DV
    sizes = (ml_qk, ml_qk, ml_v, ml_v, 4 * N_HEADS, ml_qk, ml_qk, ml_v, ml_v, 2 * GLA_RANK)
    offs = [int(o) for o in np.cumsum(sizes)[:-1]]
    w_mq, w_mk, w_mv, w_mo, w_mg, w_gq, w_gk, w_gv, w_gg, w_glr = jnp.split(w_in[0], offs, axis=-1)
    w_main = jnp.concatenate([_pack_qk(w_mq, w_mk), w_mv, w_mo, _pack_qk(w_gq, w_gk), w_gv, w_gg],
                             axis=-1).astype(BF16)
    gate_perm = np.concatenate([np.arange(N_HEADS) + g * N_HEADS for g in (0, 2, 1, 3)])
    w_gate = jnp.concatenate([w_mg[:, gate_perm], w_glr,
                              jnp.zeros((d, GATE_W - 4 * N_HEADS - 2 * GLA_RANK), F32)], axis=-1).astype(BF16)
    gate_b = jnp.concatenate([ml_gate_b[0].reshape(-1)[gate_perm],
                              jnp.zeros((GATE_W - 4 * N_HEADS,), F32)]).reshape(1, GATE_W)
    conv_w = _pack_qk(ml_conv_w[0][:, :ml_qk], ml_conv_w[0][:, ml_qk:])
    conv_b = _pack_qk(ml_conv_b[0][:ml_qk], ml_conv_b[0][ml_qk:]).reshape(1, QK_W)
    gla_qk = N_HEADS * DK
    w2_ext = jnp.zeros((GATE_W, 2 * gla_qk), F32)
    for dd in range(2):
        r0 = 4 * N_HEADS + dd * GLA_RANK
        w2_ext = w2_ext.at[r0:r0 + GLA_RANK, dd * gla_qk:(dd + 1) * gla_qk].set(gla_gate_w2[0, dd])
    b2_ext = gla_gate_b[0].reshape(1, 2 * gla_qk)
    router_w_p = jnp.concatenate([router_w[0], jnp.zeros((d, LANES - n_experts), F32)], axis=-1)
    router_b_p = jnp.concatenate([router_b[0], jnp.full((LANES - n_experts,), -jnp.inf, F32)]).reshape(1, LANES)

    cond = jnp.concatenate([c, c_ctx[None, :], jnp.zeros(((-bsz - 1) % SUBLANES, d), F32)], axis=0)
    mod = _adaln(cond, ada_w[0], ada_b[0]).reshape(cond.shape[0], N_MOD, 1, d)
    m_lat = mod[:bsz]
    m_ctx = mod[bsz:bsz + 1]

    pe = _grid_sincos(t_lat // GRID_W, d)
    g1 = norm1_g[0].reshape(1, d)
    tm = _row_tile(t_lat)
    zm_lat, zg_lat, gt_lat = _inproj(x, pe, m_lat[:, 1], m_lat[:, 0], g1, w_main, w_gate, tm)
    zm_ctx, zg_ctx, gt_ctx = _inproj(ctx, jnp.zeros((t_ctx, d), F32), m_ctx[:, 1], m_ctx[:, 0], g1,
                                     w_main, w_gate, _row_tile(t_ctx))

    a = _mlstm(zm_lat, zm_ctx, gt_lat, gt_ctx, conv_w, conv_b, gate_b, ml_norm_g[0].reshape(1, V_W))
    g = _gla(zg_lat, zg_ctx, gt_lat, gt_ctx, w2_ext, b2_ext, gla_norm_g[0].reshape(1, V_W))

    assert tm == ROUTE_BLOCK
    x1, u2, route, counts = _outproj(x, pe, a, g, w_out[0].astype(BF16), m_lat[:, 2], m_lat[:, 4], m_lat[:, 3],
                                     norm2_g[0].reshape(1, d), router_w_p, router_b_p, tm)

    return _moe(u2, route, counts, x1, moe_w_gu[0], moe_b_gu[0][:, None, :], moe_w_down[0],
                moe_b_down[0][:, None, :], m_lat[:, 5], final_norm_g.reshape(1, d))
```

```python
import functools
import math

import numpy as np
import jax
import jax.numpy as jnp
from jax import lax
from jax.experimental import pallas as pl
from jax.experimental.pallas import tpu as pltpu

F32 = jnp.float32
BF16 = jnp.bfloat16

EPS = 1e-6
GRID_W = 64
POS_BASE = 10000.0
N_MOD = 6
N_HEADS = 4
DK = 64
DV = 128
QK_W = N_HEADS * 2 * DK
V_W = N_HEADS * DV
GROUP_W = QK_W + 2 * V_W
GATE_W = 128
ML_CONV = 3
GLA_RANK = 16
GLA_NORMALIZER = 16.0
TOP_K = 4
SWIGLU_LIMIT = 7.0
SWIGLU_ALPHA = 1.702
LANES = 128
SUBLANES = 8
BF16_ROWS = 16
ML_CHUNK = 128
GLA_CHUNK = 64
GLA_PREP_ROWS = 256
CONV_ROWS = 128
EXPERT_TILE = 512
ROUTE_BLOCK = 512
RUN_ALIGN = SUBLANES
PERM_ROWS = 256
RUN_SIZES = tuple(s for s in (ROUTE_BLOCK >> i for i in range(ROUTE_BLOCK.bit_length())) if s >= RUN_ALIGN)
VMEM_LIMIT = 56 * 1024 * 1024


def _params(sem):
    return pltpu.CompilerParams(dimension_semantics=sem, vmem_limit_bytes=VMEM_LIMIT)


def _sigmoid(x):
    return 1.0 / (1.0 + jnp.exp(-x))


def _log_sigmoid(x):
    return jnp.minimum(x, 0.0) - jnp.log(1.0 + jnp.exp(-jnp.abs(x)))


def _dot(a, b):
    return jnp.dot(a, b, preferred_element_type=F32)


def _split2(x):
    hi = x.astype(BF16)
    lo = (x - hi.astype(F32)).astype(BF16)
    return hi, lo


def _dot3(a, b):
    a_hi, a_lo = _split2(a)
    b_hi, b_lo = _split2(b)
    return _dot(a_hi, b_hi) + _dot(a_lo, b_hi) + _dot(a_hi, b_lo)


def _dot_nt(a, b):
    return lax.dot_general(a, b, (((1,), (1,)), ((), ())), preferred_element_type=F32)


def _dot_tn(a, b):
    return lax.dot_general(a, b, (((0,), (0,)), ((), ())), preferred_element_type=F32)


def _adaln_body(c_ref, w_ref, b_ref, o_ref):
    c = c_ref[...]
    s = c * _sigmoid(c)
    o_ref[...] = _dot3(s, w_ref[...]) + b_ref[...]


def _adaln(cond, w, b):
    rows, d = cond.shape
    n = w.shape[1]
    tn = 512
    return pl.pallas_call(
        _adaln_body,
        grid=(n // tn,),
        in_specs=[pl.BlockSpec((rows, d), lambda j: (0, 0)),
                  pl.BlockSpec((d, tn), lambda j: (0, j)),
                  pl.BlockSpec((1, tn), lambda j: (0, j))],
        out_specs=pl.BlockSpec((rows, tn), lambda j: (0, j)),
        out_shape=jax.ShapeDtypeStruct((rows, n), F32),
        compiler_params=_params(("arbitrary",)),
        name="adaln",
    )(cond, w, b.reshape(1, n))


def _rms_mod(x, g, scale, shift):
    y = x * lax.rsqrt(jnp.mean(x * x, axis=-1, keepdims=True) + EPS)
    return (y * g) * (1.0 + scale) + shift


def _inproj_body(x_ref, pe_ref, sc_ref, sh_ref, g_ref, wm_ref, wg_ref, zm_ref, zg_ref, gt_ref):
    x = x_ref[0] + pe_ref[...]
    u = _rms_mod(x, g_ref[...], sc_ref[0], sh_ref[0]).astype(BF16)
    z = _dot(u, wm_ref[...])
    zm_ref[0] = z[:, :GROUP_W].astype(BF16)
    zg_ref[0] = z[:, GROUP_W:].astype(BF16)
    gt_ref[0] = _dot(u, wg_ref[...])


def _inproj(x, pe, scale, shift, g, w_main, w_gate, tm):
    bsz, t, d = x.shape
    per_batch = scale.shape[0] == bsz and bsz > 1
    mod_map = (lambda i, b: (b, 0, 0)) if per_batch else (lambda i, b: (0, 0, 0))
    return pl.pallas_call(
        _inproj_body,
        grid=(t // tm, bsz),
        in_specs=[pl.BlockSpec((1, tm, d), lambda i, b: (b, i, 0)),
                  pl.BlockSpec((tm, d), lambda i, b: (i, 0)),
                  pl.BlockSpec((1, 1, d), mod_map),
                  pl.BlockSpec((1, 1, d), mod_map),
                  pl.BlockSpec((1, d), lambda i, b: (0, 0)),
                  pl.BlockSpec((d, 2 * GROUP_W), lambda i, b: (0, 0)),
                  pl.BlockSpec((d, GATE_W), lambda i, b: (0, 0))],
        out_specs=[pl.BlockSpec((1, tm, GROUP_W), lambda i, b: (b, i, 0)),
                   pl.BlockSpec((1, tm, GROUP_W), lambda i, b: (b, i, 0)),
                   pl.BlockSpec((1, tm, GATE_W), lambda i, b: (b, i, 0))],
        out_shape=[jax.ShapeDtypeStruct((bsz, t, GROUP_W), BF16),
                   jax.ShapeDtypeStruct((bsz, t, GROUP_W), BF16),
                   jax.ShapeDtypeStruct((bsz, t, GATE_W), F32)],
        compiler_params=_params(("arbitrary", "arbitrary")),
        name="inproj",
    )(x, pe, scale, shift, g, w_main, w_gate)


def _scan_rows(x, op, ident, reverse, rowi):
    n = x.shape[0]
    k = 1
    while k < n:
        if reverse:
            shifted = jnp.where(rowi < n - k, pltpu.roll(x, n - k, axis=0), ident)
        else:
            shifted = jnp.where(rowi >= k, pltpu.roll(x, k, axis=0), ident)
        x = op(x, shifted)
        k *= 2
    return x


HEAD_DIRS = 2 * N_HEADS


def _mlstm_body(zl_ref, zc_ref, gl_ref, gc_ref, cw_ref, cb_ref, gb_ref, ng_ref, o_ref,
                q_s, kt_s, acc_s, cc_s, bc_s, st_s, mp_s, m_s, en_s, cr_s, wk_s, wc_s, c_s, *, t_lat, t_ctx):
    L = ML_CHUNK
    nc_ctx = t_ctx // L
    nc_lat = t_lat // L
    nc = nc_ctx + nc_lat
    scale = DK ** -0.5

    cw = cw_ref[...]
    cb = cb_ref[...]

    def conv_pass(z_ref, n, dst0):
        nb = n // CONV_ROWS
        row = lax.broadcasted_iota(jnp.int32, (CONV_ROWS, QK_W), 0)
        lane = lax.broadcasted_iota(jnp.int32, (CONV_ROWS, QK_W), 1)
        low = (lane % LANES) < DK

        def body(r, carry):
            r0 = pl.multiple_of(r * CONV_ROWS, CONV_ROWS)
            zc = z_ref[0, pl.ds(r0, CONV_ROWS), 0:QK_W].astype(F32)
            p0 = pl.multiple_of(jnp.maximum(r0 - BF16_ROWS, 0), BF16_ROWS)
            prev = z_ref[0, pl.ds(p0, BF16_ROWS), 0:QK_W].astype(F32)[BF16_ROWS - 1:BF16_ROWS]
            prev = jnp.where(r > 0, prev, 0.0)
            n0 = pl.multiple_of(jnp.minimum(r0 + CONV_ROWS, n - BF16_ROWS), BF16_ROWS)
            nxt = z_ref[0, pl.ds(n0, BF16_ROWS), 0:QK_W].astype(F32)[0:1]
            nxt = jnp.where(r < nb - 1, nxt, 0.0)
            up = jnp.where(row == 0, prev, pltpu.roll(zc, 1, axis=0))
            dn = jnp.where(row == CONV_ROWS - 1, nxt, pltpu.roll(zc, CONV_ROWS - 1, axis=0))
            y = cw[0:1] * up + cw[1:2] * zc + cw[2:3] * dn + cb
            y = y * _sigmoid(y)
            q = jnp.where(low, y * scale, 0.0)
            k = jnp.where(low, pltpu.roll(y, QK_W - DK, axis=1), 0.0)
            d0 = pl.multiple_of(dst0 + r0, CONV_ROWS)
            q_s[pl.ds(d0, CONV_ROWS), :] = q.astype(BF16)
            for h in range(N_HEADS):
                kt_s[h, :, pl.ds(d0, CONV_ROWS)] = k[:, h * LANES:(h + 1) * LANES].T[0:DK].astype(BF16)
            return carry

        lax.fori_loop(0, nb, body, 0)

    conv_pass(zc_ref, t_ctx, 0)
    conv_pass(zl_ref, t_lat, t_ctx)

    rowi = lax.broadcasted_iota(jnp.int32, (L, LANES), 0)
    lane_g = lax.broadcasted_iota(jnp.int32, (L, GATE_W), 1)
    fwd_lane = lane_g < N_HEADS
    gb = gb_ref[...]

    def gate_pass(g_ref, nchunks, g0):
        def body(i, carry):
            r0 = pl.multiple_of(i * L, L)
            gates = g_ref[0, pl.ds(r0, L), :] + gb
            lf = pltpu.roll(_log_sigmoid(gates), GATE_W - HEAD_DIRS, axis=1)
            b = jnp.where(fwd_lane, _scan_rows(lf, jnp.add, 0.0, False, rowi),
                          _scan_rows(lf, jnp.add, 0.0, True, rowi))
            c = gates - b
            cc_s[g0 + i] = c
            bc_s[g0 + i] = b
            st_s[g0 + i, 0:1, :] = jnp.max(c, axis=0, keepdims=True)
            st_s[g0 + i, 1:2, :] = jnp.sum(lf, axis=0, keepdims=True)
            return carry

        lax.fori_loop(0, nchunks, body, 0)

    gate_pass(gc_ref, nc_ctx, 0)
    gate_pass(gl_ref, nc_lat, nc_ctx)

    lane1 = lax.broadcasted_iota(jnp.int32, (1, GATE_W), 1)
    fwd_order = list(range(nc))
    bwd_order = list(range(nc_ctx - 1, -1, -1)) + list(range(nc - 1, nc_ctx - 1, -1))
    mp = {}
    for d, order in enumerate((fwd_order, bwd_order)):
        m = jnp.zeros((1, GATE_W), F32)
        for g in order:
            mp[(d, g)] = m
            m = st_s[g, 1:2, :] + jnp.maximum(m, st_s[g, 0:1, :])
    for g in range(nc):
        mp_s[g, 0:1, :] = jnp.where(lane1 < N_HEADS, mp[(0, g)], mp[(1, g)])

    def ab_body(g, carry):
        c = cc_s[g]
        b = bc_s[g]
        mprev = mp_s[g, 0:1, :]
        mloc = jnp.where(fwd_lane, _scan_rows(c, jnp.maximum, -jnp.inf, False, rowi),
                         _scan_rows(c, jnp.maximum, -jnp.inf, True, rowi))
        m = jnp.maximum(mloc, mprev)
        mlast = jnp.maximum(mprev, st_s[g, 0:1, :])
        m_s[g] = m
        en_s[g] = jnp.exp(-(b + m))
        cr_s[g] = c.T[0:HEAD_DIRS]
        wk_s[g] = jnp.exp((c - mlast).T[0:HEAD_DIRS])
        wc_s[g] = jnp.exp(jnp.broadcast_to(mprev - mlast, (L, GATE_W)).T[0:HEAD_DIRS])
        return carry

    lax.fori_loop(0, nc, ab_body, 0)

    colL = lax.broadcasted_iota(jnp.int32, (L, L), 1)
    rowL = lax.broadcasted_iota(jnp.int32, (L, L), 0)
    masks = (colL <= rowL, colL >= rowL)
    ones_blk = jnp.ones((L, DV), BF16)
    c_s[...] = jnp.zeros(c_s.shape, F32)

    def twice(x):
        return jnp.concatenate([x, x], axis=1)

    def step(g, z_ref, r_src, r_q, d, with_out, r_out):
        for h in range(N_HEADS):
            j = d * N_HEADS + h
            hs = slice(h * DV, (h + 1) * DV)
            kt = kt_s[h, :, pl.ds(r_q, L)]
            v = z_ref[0, pl.ds(r_src, L), QK_W + h * DV:QK_W + (h + 1) * DV]
            vext = jnp.concatenate([v, ones_blk], axis=1)
            state = c_s[j]
            if with_out:
                q = q_s[pl.ds(r_q, L), hs][:, 0:DK]
                mb = jnp.broadcast_to(m_s[g][:, j:j + 1], (L, L))
                w = jnp.where(masks[d], jnp.exp(cr_s[g, j:j + 1, :] - mb), 0.0)
                w_int = jnp.exp(mp_s[g][0:1, j:j + 1] - mb[:, 0:DK])
                p = jnp.concatenate([(_dot(q, kt) * w).astype(BF16), (q.astype(F32) * w_int).astype(BF16)], axis=1)
                nd = _dot(p, jnp.concatenate([vext, state.astype(BF16)], axis=0))
                hh = nd[:, :DV] / jnp.maximum(jnp.abs(nd[:, DV:]), en_s[g][:, j:j + 1])
                acc_s[d, pl.ds(r_out, L), hs] = hh
            ktw = (kt.astype(F32) * wk_s[g, j:j + 1, :]).astype(BF16)
            c_s[j] = twice(wc_s[g, j:j + 1, :]) * state + _dot(ktw, vext)

    def ctx_body(i, carry):
        for d in (0, 1):
            ii = i if d == 0 else nc_ctx - 1 - i
            r0 = pl.multiple_of(ii * L, L)
            step(ii, zc_ref, r0, r0, d, False, None)
        return carry

    def lat_body(i, carry):
        for d in (0, 1):
            ii = i if d == 0 else nc_lat - 1 - i
            r0 = pl.multiple_of(ii * L, L)
            step(nc_ctx + ii, zl_ref, r0, pl.multiple_of(t_ctx + ii * L, L), d, True, r0)
        return carry

    lax.fori_loop(0, nc_ctx, ctx_body, 0)
    lax.fori_loop(0, nc_lat, lat_body, 0)

    def out_body(i, carry):
        r0 = pl.multiple_of(i * L, L)
        for h in range(N_HEADS):
            hs = slice(h * DV, (h + 1) * DV)
            tot = acc_s[0, pl.ds(r0, L), hs] + acc_s[1, pl.ds(r0, L), hs]
            tot = tot * lax.rsqrt(jnp.mean(tot * tot, axis=-1, keepdims=True) + EPS)
            og = zl_ref[0, pl.ds(r0, L), QK_W + V_W + h * DV:QK_W + V_W + (h + 1) * DV].astype(F32)
            o_ref[0, pl.ds(r0, L), hs] = (tot * ng_ref[:, hs] * _sigmoid(og)).astype(BF16)
        return carry

    lax.fori_loop(0, nc_lat, out_body, 0)


def _mlstm(zm_lat, zm_ctx, gt_lat, gt_ctx, conv_w, conv_b, gate_b, norm_g):
    bsz, t_lat, _ = zm_lat.shape
    t_ctx = zm_ctx.shape[1]
    nc = (t_lat + t_ctx) // ML_CHUNK
    body = functools.partial(_mlstm_body, t_lat=t_lat, t_ctx=t_ctx)
    full = lambda b: (0, 0)
    return pl.pallas_call(
        body,
        grid=(bsz,),
        in_specs=[pl.BlockSpec((1, t_lat, GROUP_W), lambda b: (b, 0, 0)),
                  pl.BlockSpec((1, t_ctx, GROUP_W), lambda b: (b, 0, 0)),
                  pl.BlockSpec((1, t_lat, GATE_W), lambda b: (b, 0, 0)),
                  pl.BlockSpec((1, t_ctx, GATE_W), lambda b: (b, 0, 0)),
                  pl.BlockSpec((ML_CONV, QK_W), full),
                  pl.BlockSpec((1, QK_W), full),
                  pl.BlockSpec((1, GATE_W), full),
                  pl.BlockSpec((1, V_W), full)],
        out_specs=pl.BlockSpec((1, t_lat, V_W), lambda b: (b, 0, 0)),
        out_shape=jax.ShapeDtypeStruct((bsz, t_lat, V_W), BF16),
        scratch_shapes=[pltpu.VMEM((t_ctx + t_lat, QK_W), BF16),
                        pltpu.VMEM((N_HEADS, DK, t_ctx + t_lat), BF16),
                        pltpu.VMEM((2, t_lat, V_W), F32),
                        pltpu.VMEM((nc, ML_CHUNK, GATE_W), F32),
                        pltpu.VMEM((nc, ML_CHUNK, GATE_W), F32),
                        pltpu.VMEM((nc, SUBLANES, GATE_W), F32),
                        pltpu.VMEM((nc, SUBLANES, GATE_W), F32),
                        pltpu.VMEM((nc, ML_CHUNK, GATE_W), F32),
                        pltpu.VMEM((nc, ML_CHUNK, GATE_W), F32),
                        pltpu.VMEM((nc, HEAD_DIRS, ML_CHUNK), F32),
                        pltpu.VMEM((nc, HEAD_DIRS, ML_CHUNK), F32),
                        pltpu.VMEM((nc, HEAD_DIRS, GATE_W), F32),
                        pltpu.VMEM((HEAD_DIRS, DK, 2 * DV), F32)],
        compiler_params=_params(("arbitrary",)),
        name="mlstm",
    )(zm_lat, zm_ctx, gt_lat, gt_ctx, conv_w, conv_b, gate_b, norm_g)


def _gla_body(zl_ref, zc_ref, gl_ref, gc_ref, w2_ref, b2_ref, ng_ref, o_ref,
              acc_s, s_s, qt_s, kt_s, ee_s, *, t_lat, t_ctx):
    L = GLA_CHUNK
    nc_ctx = t_ctx // L
    nc_lat = t_lat // L
    scale = DK ** -0.5
    half = N_HEADS * DK

    rowi = lax.broadcasted_iota(jnp.int32, (L, L), 0)
    coli = lax.broadcasted_iota(jnp.int32, (L, L), 1)
    masks = (coli <= rowi, coli >= rowi)
    tri_b = (masks[0].astype(BF16), masks[1].astype(BF16))
    low = lax.broadcasted_iota(jnp.int32, (L, LANES), 1) < DK
    w2_hi, w2_lo = _split2(w2_ref[...])
    bias2 = b2_ref[...]

    def prep(z_ref, g_ref, n_rows, c0, row0):
        grp = math.gcd(n_rows, GLA_PREP_ROWS)

        def body(gi, carry):
            rg = pl.multiple_of(gi * grp, grp)
            g_hi, g_lo = _split2(g_ref[0, pl.ds(rg, grp), :])
            pre = _dot(g_hi, w2_hi) + _dot(g_lo, w2_hi) + _dot(g_hi, w2_lo) + bias2
            la_hi, la_lo = _split2(_log_sigmoid(pre) * (1.0 / GLA_NORMALIZER))
            for cc in range(grp // L):
                rs = slice(cc * L, (cc + 1) * L)
                r0 = pl.multiple_of(rg + cc * L, L)
                rq = pl.multiple_of(row0 + rg + cc * L, L)
                c = c0 + gi * (grp // L) + cc
                for d in (0, 1):
                    ds_ = slice(d * half, (d + 1) * half)
                    b_all = _dot(tri_b[d], la_hi[rs, ds_]) + _dot(tri_b[d], la_lo[rs, ds_])
                    for hp in range(N_HEADS // 2):
                        pair = b_all[:, hp * LANES:(hp + 1) * LANES]
                        swapped = pltpu.roll(pair, DK, axis=1)
                        for hh in range(2):
                            h = 2 * hp + hh
                            hs = slice(h * LANES, (h + 1) * LANES)
                            b2 = jnp.where(low, pair, swapped) if hh == 0 else jnp.where(low, swapped, pair)
                            b_end = b2[L - 1:L] if d == 0 else b2[0:1]
                            qk = z_ref[0, pl.ds(r0, L), hs].astype(F32)
                            qkt = qk * jnp.exp(jnp.where(low, b2, -b2))
                            qt_s[d, pl.ds(rq, L), hs] = jnp.where(low, qkt * scale, 0.0).astype(BF16)
                            kt_s[d, pl.ds(rq, L), hs] = jnp.where(low, pltpu.roll(qkt, DK, axis=1), 0.0).astype(BF16)
                            ee_s[d, c, 0:1, hs] = jnp.exp(b_end)
            return carry

        lax.fori_loop(0, n_rows // grp, body, 0)

    prep(zc_ref, gc_ref, t_ctx, 0, 0)
    prep(zl_ref, gl_ref, t_lat, nc_ctx, t_ctx)

    s_s[...] = jnp.zeros(s_s.shape, F32)

    def step(z_ref, r0, rq, c, d, with_out):
        for h in range(N_HEADS):
            j = d * N_HEADS + h
            hs = slice(h * LANES, (h + 1) * LANES)
            qt = qt_s[d, pl.ds(rq, L), hs]
            kt = kt_s[d, pl.ds(rq, L), hs]
            v = z_ref[0, pl.ds(r0, L), QK_W + h * DV:QK_W + (h + 1) * DV]
            state = s_s[j]
            if with_out:
                a = jnp.where(masks[d], _dot_nt(qt, kt), 0.0).astype(BF16)
                acc_s[d, pl.ds(r0, L), hs] = _dot(a, v) + _dot_nt(qt, state.astype(BF16))
            e_end = ee_s[d, c, 0:1, hs]
            s_s[j] = state * e_end + _dot_tn(v, (kt.astype(F32) * e_end).astype(BF16))

    def loop(z_ref, n, c0, row0, with_out):
        def body(i, carry):
            for d in (0, 1):
                ii = i if d == 0 else n - 1 - i
                r0 = pl.multiple_of(ii * L, L)
                step(z_ref, r0, pl.multiple_of(row0 + ii * L, L), c0 + ii, d, with_out)
            return carry
        lax.fori_loop(0, n, body, 0, unroll=2)

    loop(zc_ref, nc_ctx, 0, 0, False)
    loop(zl_ref, nc_lat, nc_ctx, t_ctx, True)

    def out_body(i, carry):
        r0 = pl.multiple_of(i * L, L)
        for h in range(N_HEADS):
            hs = slice(h * LANES, (h + 1) * LANES)
            tot = acc_s[0, pl.ds(r0, L), hs] + acc_s[1, pl.ds(r0, L), hs]
            tot = tot * lax.rsqrt(jnp.mean(tot * tot, axis=-1, keepdims=True) + EPS)
            gg = zl_ref[0, pl.ds(r0, L), QK_W + V_W + h * DV:QK_W + V_W + (h + 1) * DV].astype(F32)
            o_ref[0, pl.ds(r0, L), hs] = (tot * ng_ref[:, hs] * (gg * _sigmoid(gg))).astype(BF16)
        return carry

    lax.fori_loop(0, nc_lat, out_body, 0)


def _gla(zg_lat, zg_ctx, gt_lat, gt_ctx, w2_ext, b2_ext, norm_g):
    bsz, t_lat, _ = zg_lat.shape
    t_ctx = zg_ctx.shape[1]
    body = functools.partial(_gla_body, t_lat=t_lat, t_ctx=t_ctx)
    full = lambda b: (0, 0)
    return pl.pallas_call(
        body,
        grid=(bsz,),
        in_specs=[pl.BlockSpec((1, t_lat, GROUP_W), lambda b: (b, 0, 0)),
                  pl.BlockSpec((1, t_ctx, GROUP_W), lambda b: (b, 0, 0)),
                  pl.BlockSpec((1, t_lat, GATE_W), lambda b: (b, 0, 0)),
                  pl.BlockSpec((1, t_ctx, GATE_W), lambda b: (b, 0, 0)),
                  pl.BlockSpec((GATE_W, 2 * N_HEADS * DK), full),
                  pl.BlockSpec((1, 2 * N_HEADS * DK), full),
                  pl.BlockSpec((1, V_W), full)],
        out_specs=pl.BlockSpec((1, t_lat, V_W), lambda b: (b, 0, 0)),
        out_shape=jax.ShapeDtypeStruct((bsz, t_lat, V_W), BF16),
        scratch_shapes=[pltpu.VMEM((2, t_lat, V_W), F32),
                        pltpu.VMEM((HEAD_DIRS, DV, LANES), F32),
                        pltpu.VMEM((2, t_ctx + t_lat, QK_W), BF16),
                        pltpu.VMEM((2, t_ctx + t_lat, QK_W), BF16),
                        pltpu.VMEM((2, (t_ctx + t_lat) // GLA_CHUNK, SUBLANES, QK_W), F32)],
        compiler_params=_params(("arbitrary",)),
        name="gla",
    )(zg_lat, zg_ctx, gt_lat, gt_ctx, w2_ext, b2_ext, norm_g)


def _outproj_body(x_ref, pe_ref, a_ref, g_ref, wo_ref, g1_ref, sc_ref, sh_ref, n2_ref, rw_ref, rb_ref,
                  x1_ref, u2_ref, route_ref, cnt_ref):
    x = x_ref[0] + pe_ref[...]
    y = _dot(a_ref[0], wo_ref[0:V_W, :]) + _dot(g_ref[0], wo_ref[V_W:2 * V_W, :])
    x1 = x + g1_ref[0] * y
    x1_ref[0] = x1
    u2 = _rms_mod(x1, n2_ref[...], sc_ref[0], sh_ref[0])
    u2_ref[0] = u2
    u_hi, u_lo = _split2(u2)
    w_hi, w_lo = _split2(rw_ref[...])
    hh = _dot(u_hi, jnp.concatenate([w_hi, w_lo], axis=1))
    logits = hh[:, :LANES] + _dot(u_lo, w_hi) + hh[:, LANES:] + rb_ref[...]
    lane = lax.broadcasted_iota(jnp.int32, logits.shape, 1).astype(F32)
    ids = []
    vals = []
    for _ in range(TOP_K):
        m = jnp.max(logits, axis=1, keepdims=True)
        idx = jnp.min(jnp.where(logits == m, lane, float(LANES)), axis=1, keepdims=True)
        ids.append(idx)
        vals.append(m)
        logits = jnp.where(lane == idx, -jnp.inf, logits)
    exps = [jnp.exp(v - vals[0]) for v in vals]
    denom = exps[0]
    for e in exps[1:]:
        denom = denom + e
    route = jnp.zeros(logits.shape, F32)
    for k in range(TOP_K):
        route = jnp.where(lane == float(k), ids[k], route)
        route = jnp.where(lane == float(TOP_K + k), exps[k] / denom, route)
    route_ref[0] = route

    @pl.when((pl.program_id(0) == 0) & (pl.program_id(1) == 0))
    def _():
        cnt_ref[...] = jnp.zeros(cnt_ref.shape, F32)

    cnt_ref[...] += _block_picks(route)[3]


def _outproj(x, pe, a, g, w_out, gate1, scale2, shift2, norm2_g, router_w, router_b, tm):
    bsz, t, d = x.shape
    tok = lambda i, b: (b, i, 0)
    mod = lambda i, b: (b, 0, 0)
    full = lambda i, b: (0, 0)
    return pl.pallas_call(
        _outproj_body,
        grid=(t // tm, bsz),
        in_specs=[pl.BlockSpec((1, tm, d), tok),
                  pl.BlockSpec((tm, d), lambda i, b: (i, 0)),
                  pl.BlockSpec((1, tm, V_W), tok),
                  pl.BlockSpec((1, tm, V_W), tok),
                  pl.BlockSpec((2 * V_W, d), full),
                  pl.BlockSpec((1, 1, d), mod),
                  pl.BlockSpec((1, 1, d), mod),
                  pl.BlockSpec((1, 1, d), mod),
                  pl.BlockSpec((1, d), full),
                  pl.BlockSpec((d, LANES), full),
                  pl.BlockSpec((1, LANES), full)],
        out_specs=[pl.BlockSpec((1, tm, d), tok),
                   pl.BlockSpec((1, tm, d), tok),
                   pl.BlockSpec((1, tm, LANES), tok),
                   pl.BlockSpec((1, LANES), full)],
        out_shape=[jax.ShapeDtypeStruct((bsz, t, d), F32),
                   jax.ShapeDtypeStruct((bsz, t, d), F32),
                   jax.ShapeDtypeStruct((bsz, t, LANES), F32),
                   jax.ShapeDtypeStruct((1, LANES), F32)],
        compiler_params=_params(("arbitrary", "arbitrary")),
        name="outproj_router",
    )(x, pe, a, g, w_out, gate1, scale2, shift2, norm2_g, router_w, router_b)


def _lane_excl_cumsum(x):
    x8 = jnp.broadcast_to(x, (SUBLANES, LANES))
    lane8 = lax.broadcasted_iota(jnp.int32, (SUBLANES, LANES), 1)
    inc = x8
    s = 1
    while s < LANES:
        inc = inc + jnp.where(lane8 >= s, pltpu.roll(inc, s, axis=1), 0.0)
        s *= 2
    return (inc - x8)[0:1]


def _block_picks(route):
    lane = lax.broadcasted_iota(jnp.int32, route.shape, 1).astype(F32)
    onehots = [lane == route[:, k:k + 1] for k in range(TOP_K)]
    oh = onehots[0].astype(F32)
    for o in onehots[1:]:
        oh = oh + o.astype(F32)
    run_len = jnp.ceil(jnp.sum(oh, axis=0, keepdims=True) * (1.0 / RUN_ALIGN)) * float(RUN_ALIGN)
    return lane, onehots, oh, run_len


def _route_pos_body(route_ref, cnt_ref, lic_ref, lit_ref, tab_ref, off_s, *, tile):
    tb = route_ref.shape[0]
    lane, onehots, oh, run_len = _block_picks(route_ref[...])

    @pl.when(pl.program_id(0) == 0)
    def _():
        cpad = jnp.floor((cnt_ref[...] + float(tile - 1)) * (1.0 / tile)) * float(tile)
        off_s[...] = _lane_excl_cumsum(cpad)

    r = lax.broadcasted_iota(jnp.int32, (tb, tb), 0)
    c = lax.broadcasted_iota(jnp.int32, (tb, tb), 1)
    before = _dot((c < r).astype(BF16), oh.astype(BF16))
    lstart = _lane_excl_cumsum(run_len)
    local = lstart + before
    li = jnp.zeros((tb, LANES), F32)
    for k in range(TOP_K):
        lk = jnp.sum(jnp.where(onehots[k], local, 0.0), axis=1, keepdims=True)
        li = jnp.where(lane == float(k), lk, li)
    lic_ref[...] = li
    lit_ref[0] = li.T[0:SUBLANES]
    row8 = lax.broadcasted_iota(jnp.int32, (SUBLANES, LANES), 0)
    total = jnp.sum(run_len, axis=1, keepdims=True)
    tab = jnp.where(row8 == 0, off_s[...],
                    jnp.where(row8 == 1, run_len, jnp.where(row8 == 2, lstart, jnp.where(row8 == 3, total, 0.0))))
    tab_ref[0] = tab.astype(jnp.int32)
    off_s[...] += run_len


def _route_pos(route, counts, tile):
    n = route.shape[0]
    tb = ROUTE_BLOCK
    nb = n // tb
    return pl.pallas_call(
        functools.partial(_route_pos_body, tile=tile),
        grid=(nb,),
        in_specs=[pl.BlockSpec((tb, LANES), lambda j: (j, 0)),
                  pl.BlockSpec((1, LANES), lambda j: (0, 0))],
        out_specs=[pl.BlockSpec((tb, LANES), lambda j: (j, 0)),
                   pl.BlockSpec((1, SUBLANES, tb), lambda j: (j, 0, 0)),
                   pl.BlockSpec((1, SUBLANES, LANES), lambda j: (j, 0, 0))],
        out_shape=[jax.ShapeDtypeStruct((n, LANES), F32),
                   jax.ShapeDtypeStruct((nb, SUBLANES, tb), F32),
                   jax.ShapeDtypeStruct((nb, SUBLANES, LANES), jnp.int32)],
        scratch_shapes=[pltpu.VMEM((1, LANES), F32)],
        compiler_params=_params(("arbitrary",)),
        name="route_pos",
    )(route, counts)


def _issue_runs(tab_ref, n_experts, make_copy):
    def per_expert(e, carry):
        g = tab_ref[0, 0, e]
        cnt = tab_ref[0, 1, e]
        l = tab_ref[0, 2, e]
        for size in RUN_SIZES:
            part = cnt & size

            @pl.when(part != 0)
            def _():
                make_copy(pl.multiple_of(l, RUN_ALIGN), pl.multiple_of(g, RUN_ALIGN), size).start()

            l = l + part
            g = g + part
        return carry

    lax.fori_loop(0, n_experts, per_expert, 0)


def _dispatch_body(zrow_ref, zflag_ref, nu_ref, tab_ref, lit_ref, u_ref, xs_ref, buf, zero_s, tot_s, sem, zsem,
                   *, n_experts, tile):
    i = pl.program_id(0)
    nb = pl.num_programs(0)
    tb = u_ref.shape[0]
    rows = buf.shape[1]
    n_tiles = xs_ref.shape[0] // tile
    slot = i % 2

    @pl.when(i == 0)
    def _():
        zero_s[...] = jnp.zeros(zero_s.shape, F32)

        def zero_tile(r0):
            cp = pltpu.make_async_copy(zero_s, xs_ref.at[pl.ds(pl.multiple_of(r0, tile), tile)], zsem)
            cp.start()
            cp.wait()

        for e in range(n_experts):
            @pl.when(zflag_ref[e] == 1)
            def _():
                zero_tile(zrow_ref[e])

        def tail(t, carry):
            zero_tile(t * tile)
            return carry

        lax.fori_loop(nu_ref[0], n_tiles, tail, 0)

    def wait_slot(s):
        nrows = pl.multiple_of(tot_s[s], RUN_ALIGN)
        pltpu.make_async_copy(buf.at[s, pl.ds(0, nrows)], xs_ref.at[pl.ds(0, nrows)], sem.at[s]).wait()

    @pl.when(i >= 2)
    def _():
        wait_slot(slot)

    tot_s[slot] = tab_ref[0, 3, 0]

    lit = lit_ref[0]
    ub = u_ref[...].astype(BF16)
    ri0 = lax.broadcasted_iota(jnp.int32, (PERM_ROWS, tb), 0).astype(F32)
    for c in range(rows // PERM_ROWS):
        ri = ri0 + float(c * PERM_ROWS)
        hit = jnp.zeros((PERM_ROWS, tb), F32)
        for k in range(TOP_K):
            hit = jnp.where(ri == lit[k:k + 1], 1.0, hit)
        buf[slot, c * PERM_ROWS:(c + 1) * PERM_ROWS] = _dot(hit.astype(BF16), ub)
    _issue_runs(tab_ref, n_experts,
                lambda l, g, size: pltpu.make_async_copy(buf.at[slot, pl.ds(l, size)], xs_ref.at[pl.ds(g, size)],
                                                         sem.at[slot]))

    @pl.when(i == nb - 1)
    def _():
        wait_slot(slot)

    @pl.when((i == nb - 1) & (nb > 1))
    def _():
        wait_slot(1 - slot)


def _dispatch(u2, tab, lit, zrow, zflag, n_used, n_rows, tile):
    n, d = u2.shape
    tb = ROUTE_BLOCK
    nb = n // tb
    n_experts = zrow.shape[0]
    grid_spec = pltpu.PrefetchScalarGridSpec(
        num_scalar_prefetch=3,
        grid=(nb,),
        in_specs=[pl.BlockSpec((1, SUBLANES, LANES), lambda i, zr, zf, nu: (i, 0, 0), memory_space=pltpu.SMEM),
                  pl.BlockSpec((1, SUBLANES, tb), lambda i, zr, zf, nu: (i, 0, 0)),
                  pl.BlockSpec((tb, d), lambda i, zr, zf, nu: (i, 0))],
        out_specs=pl.BlockSpec(memory_space=pl.ANY),
        scratch_shapes=[pltpu.VMEM((2, tb * TOP_K + n_experts * RUN_ALIGN, d), F32), pltpu.VMEM((tile, d), F32),
                        pltpu.SMEM((2,), jnp.int32), pltpu.SemaphoreType.DMA((2,)), pltpu.SemaphoreType.DMA(())],
    )
    return pl.pallas_call(
        functools.partial(_dispatch_body, n_experts=n_experts, tile=tile),
        grid_spec=grid_spec,
        out_shape=jax.ShapeDtypeStruct((n_rows, d), F32),
        compiler_params=_params(("arbitrary",)),
        name="dispatch",
    )(zrow, zflag, n_used, tab, lit, u2)


def _experts_body(te_ref, nu_ref, x_ref, wgu_ref, bgu_ref, wd_ref, bd_ref, y_ref, wgu_s, wd_s, *, d_expert):
    t = pl.program_id(0)

    @pl.when((t == 0) | (te_ref[t] != te_ref[jnp.maximum(t - 1, 0)]))
    def _():
        wgu_s[...] = wgu_ref[0].astype(BF16)
        wd_s[...] = wd_ref[0].astype(BF16)

    @pl.when(t < nu_ref[0])
    def _():
        gu = _dot(x_ref[...].astype(BF16), wgu_s[...]) + bgu_ref[0]
        gate = jnp.minimum(gu[:, :d_expert], SWIGLU_LIMIT)
        up = jnp.clip(gu[:, d_expert:], -SWIGLU_LIMIT, SWIGLU_LIMIT)
        act = ((up + 1.0) * (gate * _sigmoid(SWIGLU_ALPHA * gate))).astype(BF16)
        y_ref[...] = _dot(act, wd_s[...]) + bd_ref[0]

    @pl.when(t >= nu_ref[0])
    def _():
        y_ref[...] = jnp.zeros(y_ref.shape, F32)


def _experts(xs, tile_expert, n_used, w_gu, b_gu, w_down, b_down, tile):
    n_rows, d = xs.shape
    n_experts, _, two_de = w_gu.shape
    d_expert = two_de // 2
    grid_spec = pltpu.PrefetchScalarGridSpec(
        num_scalar_prefetch=2,
        grid=(n_rows // tile,),
        in_specs=[pl.BlockSpec((tile, d), lambda t, te, nu: (jnp.minimum(t, nu[0] - 1), 0)),
                  pl.BlockSpec((1, d, two_de), lambda t, te, nu: (te[t], 0, 0)),
                  pl.BlockSpec((1, 1, two_de), lambda t, te, nu: (te[t], 0, 0)),
                  pl.BlockSpec((1, d_expert, d), lambda t, te, nu: (te[t], 0, 0)),
                  pl.BlockSpec((1, 1, d), lambda t, te, nu: (te[t], 0, 0))],
        out_specs=pl.BlockSpec((tile, d), lambda t, te, nu: (t, 0)),
        scratch_shapes=[pltpu.VMEM((d, two_de), BF16), pltpu.VMEM((d_expert, d), BF16)],
    )
    return pl.pallas_call(
        functools.partial(_experts_body, d_expert=d_expert),
        grid_spec=grid_spec,
        out_shape=jax.ShapeDtypeStruct((n_rows, d), F32),
        compiler_params=_params(("arbitrary",)),
        name="experts",
    )(tile_expert, n_used, xs, w_gu, b_gu, w_down, b_down)


def _combine_body(tab_ref, tabn_ref, lic_ref, route_ref, x1_ref, g5_ref, fg_ref, ys_ref, o_ref, buf, sem,
                  *, n_experts):
    i = pl.program_id(0)
    n = pl.num_programs(0)
    tb = x1_ref.shape[0]
    rows = buf.shape[1]
    slot = i % 2

    def issue(t_ref, s):
        _issue_runs(t_ref, n_experts,
                    lambda l, g, size: pltpu.make_async_copy(ys_ref.at[pl.ds(g, size)], buf.at[s, pl.ds(l, size)],
                                                             sem.at[s]))

    @pl.when(i == 0)
    def _():
        buf[...] = jnp.zeros(buf.shape, F32)
        issue(tab_ref, 0)

    @pl.when(i + 1 < n)
    def _():
        issue(tabn_ref, 1 - slot)

    nrows = pl.multiple_of(tab_ref[0, 3, 0], RUN_ALIGN)
    pltpu.make_async_copy(ys_ref.at[pl.ds(0, nrows)], buf.at[slot, pl.ds(0, nrows)], sem.at[slot]).wait()

    lane = lax.broadcasted_iota(jnp.int32, (tb, LANES), 1)
    li_hi, li_lo = _split2(lic_ref[...])
    a = jnp.where(lane < TOP_K, li_hi.astype(F32),
                  jnp.where(lane < 2 * TOP_K, pltpu.roll(li_lo.astype(F32), TOP_K, axis=1),
                            jnp.where(lane < 3 * TOP_K, pltpu.roll(route_ref[...], TOP_K, axis=1), 0.0))).astype(BF16)
    rsel = lax.broadcasted_iota(jnp.int32, (LANES, LANES), 0)
    lane_f = lane.astype(F32)
    idx, gate = [], []
    for k in range(TOP_K):
        sel = jnp.concatenate([((rsel == k) | (rsel == TOP_K + k)).astype(BF16),
                               (rsel == 2 * TOP_K + k).astype(BF16)], axis=1)
        bk = _dot(a, sel)
        idx.append(bk[:, :LANES])
        gate.append(bk[:, LANES:])
    moe = jnp.zeros(x1_ref.shape, F32)
    for c0 in range(0, rows, PERM_ROWS):
        blocks = []
        for c in range(c0, c0 + PERM_ROWS, LANES):
            col = lane_f + float(c)
            w = jnp.zeros((tb, LANES), F32)
            for k in range(TOP_K):
                w = jnp.where(idx[k] == col, gate[k], w)
            blocks.append(w.astype(BF16))
        moe = moe + _dot(jnp.concatenate(blocks, axis=1), buf[slot, c0:c0 + PERM_ROWS].astype(BF16))
    xo = x1_ref[...] + g5_ref[0] * moe
    o_ref[...] = xo * lax.rsqrt(jnp.mean(xo * xo, axis=-1, keepdims=True) + EPS) * fg_ref[...]


def _combine(ys, tab, lic, route, x1, gate5, final_g, t_lat, n_experts):
    n, d = x1.shape
    tb = ROUTE_BLOCK
    nb = n // tb
    per_b = t_lat // tb
    return pl.pallas_call(
        functools.partial(_combine_body, n_experts=n_experts),
        grid=(nb,),
        in_specs=[pl.BlockSpec((1, SUBLANES, LANES), lambda i: (i, 0, 0), memory_space=pltpu.SMEM),
                  pl.BlockSpec((1, SUBLANES, LANES), lambda i: (jnp.minimum(i + 1, nb - 1), 0, 0),
                               memory_space=pltpu.SMEM),
                  pl.BlockSpec((tb, LANES), lambda i: (i, 0)),
                  pl.BlockSpec((tb, LANES), lambda i: (i, 0)),
                  pl.BlockSpec((tb, d), lambda i: (i, 0)),
                  pl.BlockSpec((1, 1, d), lambda i: (i // per_b, 0, 0)),
                  pl.BlockSpec((1, d), lambda i: (0, 0)),
                  pl.BlockSpec(memory_space=pl.ANY)],
        out_specs=pl.BlockSpec((tb, d), lambda i: (i, 0)),
        out_shape=jax.ShapeDtypeStruct((n, d), F32),
        scratch_shapes=[pltpu.VMEM((2, tb * TOP_K + n_experts * RUN_ALIGN, d), F32), pltpu.SemaphoreType.DMA((2,))],
        compiler_params=_params(("arbitrary",)),
        name="combine",
    )(tab, tab, lic, route, x1, gate5, final_g, ys)


def _moe(u2, route, counts, x1, w_gu, b_gu, w_down, b_down, gate5, final_g):
    bsz, t_lat, d = x1.shape
    n = bsz * t_lat
    n_experts = w_gu.shape[0]
    tile = EXPERT_TILE
    route2 = route.reshape(n, LANES)
    lic, lit, tab = _route_pos(route2, counts, tile)
    cnt = counts[0, :n_experts].astype(jnp.int32)
    cpad = ((cnt + tile - 1) // tile) * tile
    ends = jnp.cumsum(cpad)
    n_tiles = -(-(n * TOP_K + (n // ROUTE_BLOCK) * n_experts * RUN_ALIGN) // tile) + n_experts
    n_rows = n_tiles * tile
    tile_expert = jnp.minimum(
        jnp.sum(jnp.arange(n_tiles, dtype=jnp.int32)[:, None] * tile >= ends[None, :], axis=1), n_experts - 1
    ).astype(jnp.int32)
    n_used = (ends[-1:] // tile).astype(jnp.int32)
    zrow = jnp.maximum(ends - tile, 0).astype(jnp.int32)
    zflag = (cnt % tile != 0).astype(jnp.int32)
    xs = _dispatch(u2.reshape(n, d), tab, lit, zrow, zflag, n_used, n_rows, tile)
    ys = _experts(xs, tile_expert, n_used, w_gu, b_gu, w_down, b_down, tile)
    out = _combine(ys, tab, lic, route2, x1.reshape(n, d), gate5, final_g, t_lat, n_experts)
    return out.reshape(bsz, t_lat, d)


def _grid_sincos(rows, d):
    nf = d // 4
    omega = 1.0 / (POS_BASE ** (jnp.arange(nf, dtype=F32) / nf))
    r = jnp.broadcast_to(jnp.arange(rows, dtype=F32)[:, None, None] * omega, (rows, GRID_W, nf))
    cl = jnp.broadcast_to(jnp.arange(GRID_W, dtype=F32)[None, :, None] * omega, (rows, GRID_W, nf))
    pe = jnp.concatenate([jnp.sin(r), jnp.cos(r), jnp.sin(cl), jnp.cos(cl)], axis=-1)
    return pe.reshape(rows * GRID_W, d)


def _pack_qk(wq, wk):
    lead = wq.shape[:-1]
    q = wq.reshape(lead + (N_HEADS, DK))
    k = wk.reshape(lead + (N_HEADS, DK))
    return jnp.concatenate([q, k], axis=-1).reshape(lead + (QK_W,))


def _row_tile(t):
    for tm in (512, 256, 128):
        if t % tm == 0:
            return tm
    raise ValueError(f"sequence length {t} must be a multiple of 128")


def kernel(x, c, ctx, c_ctx, ada_w, ada_b, norm1_g, w_in, ml_conv_w, ml_conv_b, ml_gate_b, ml_norm_g,
           gla_gate_w2, gla_gate_b, gla_norm_g, w_out, norm2_g, router_w, router_b, moe_w_gu, moe_b_gu,
           moe_w_down, moe_b_down, final_norm_g):
    bsz, t_lat, d = x.shape
    t_ctx = ctx.shape[1]
    assert ada_w.shape[0] == 1, "single-layer block"
    assert t_lat % ML_CHUNK == 0 and t_ctx % ML_CHUNK == 0 and t_lat % GRID_W == 0 and t_lat % ROUTE_BLOCK == 0
    n_experts = router_w.shape[-1]
    assert n_experts <= LANES

    ml_qk, ml_v = N_HEADS * DK, N_HEADS * DV
    sizes = (ml_qk, ml_qk, ml_v, ml_v, 4 * N_HEADS, ml_qk, ml_qk, ml_v, ml_v, 2 * GLA_RANK)
    offs = [int(o) for o in np.cumsum(sizes)[:-1]]
    w_mq, w_mk, w_mv, w_mo, w_mg, w_gq, w_gk, w_gv, w_gg, w_glr = jnp.split(w_in[0], offs, axis=-1)
    w_main = jnp.concatenate([_pack_qk(w_mq, w_mk), w_mv, w_mo, _pack_qk(w_gq, w_gk), w_gv, w_gg],
                             axis=-1).astype(BF16)
    gate_perm = np.concatenate([np.arange(N_HEADS) + g * N_HEADS for g in (0, 2, 1, 3)])
    w_gate = jnp.concatenate([w_mg[:, gate_perm], w_glr,
                              jnp.zeros((d, GATE_W - 4 * N_HEADS - 2 * GLA_RANK), F32)], axis=-1).astype(BF16)
    gate_b = jnp.concatenate([ml_gate_b[0].reshape(-1)[gate_perm],
                              jnp.zeros((GATE_W - 4 * N_HEADS,), F32)]).reshape(1, GATE_W)
    conv_w = _pack_qk(ml_conv_w[0][:, :ml_qk], ml_conv_w[0][:, ml_qk:])
    conv_b = _pack_qk(ml_conv_b[0][:ml_qk], ml_conv_b[0][ml_qk:]).reshape(1, QK_W)
    gla_qk = N_HEADS * DK
    w2_ext = jnp.zeros((GATE_W, 2 * gla_qk), F32)
    for dd in range(2):
        r0 = 4 * N_HEADS + dd * GLA_RANK
        w2_ext = w2_ext.at[r0:r0 + GLA_RANK, dd * gla_qk:(dd + 1) * gla_qk].set(gla_gate_w2[0, dd])
    b2_ext = gla_gate_b[0].reshape(1, 2 * gla_qk)
    router_w_p = jnp.concatenate([router_w[0], jnp.zeros((d, LANES - n_experts), F32)], axis=-1)
    router_b_p = jnp.concatenate([router_b[0], jnp.full((LANES - n_experts,), -jnp.inf, F32)]).reshape(1, LANES)

    cond = jnp.concatenate([c, c_ctx[None, :], jnp.zeros(((-bsz - 1) % SUBLANES, d), F32)], axis=0)
    mod = _adaln(cond, ada_w[0], ada_b[0]).reshape(cond.shape[0], N_MOD, 1, d)
    m_lat = mod[:bsz]
    m_ctx = mod[bsz:bsz + 1]

    pe = _grid_sincos(t_lat // GRID_W, d)
    g1 = norm1_g[0].reshape(1, d)
    tm = _row_tile(t_lat)
    zm_lat, zg_lat, gt_lat = _inproj(x, pe, m_lat[:, 1], m_lat[:, 0], g1, w_main, w_gate, tm)
    zm_ctx, zg_ctx, gt_ctx = _inproj(ctx, jnp.zeros((t_ctx, d), F32), m_ctx[:, 1], m_ctx[:, 0], g1,
                                     w_main, w_gate, _row_tile(t_ctx))

    a = _mlstm(zm_lat, zm_ctx, gt_lat, gt_ctx, conv_w, conv_b, gate_b, ml_norm_g[0].reshape(1, V_W))
    g = _gla(zg_lat, zg_ctx, gt_lat, gt_ctx, w2_ext, b2_ext, gla_norm_g[0].reshape(1, V_W))

    assert tm == ROUTE_BLOCK
    x1, u2, route, counts = _outproj(x, pe, a, g, w_out[0].astype(BF16), m_lat[:, 2], m_lat[:, 4], m_lat[:, 3],
                                     norm2_g[0].reshape(1, d), router_w_p, router_b_p, tm)

    return _moe(u2, route, counts, x1, moe_w_gu[0], moe_b_gu[0][:, None, :], moe_w_down[0],
                moe_b_down[0][:, None, :], m_lat[:, 5], final_norm_g.reshape(1, d))
```

```python
import functools
import math

import numpy as np
import jax
import jax.numpy as jnp
from jax import lax
from jax.experimental import pallas as pl
from jax.experimental.pallas import tpu as pltpu

F32 = jnp.float32
BF16 = jnp.bfloat16

EPS = 1e-6
GRID_W = 64
POS_BASE = 10000.0
N_MOD = 6
N_HEADS = 4
DK = 64
DV = 128
QK_W = N_HEADS * 2 * DK
V_W = N_HEADS * DV
GROUP_W = QK_W + 2 * V_W
GATE_W = 128
ML_CONV = 3
GLA_RANK = 16
GLA_NORMALIZER = 16.0
TOP_K = 4
SWIGLU_LIMIT = 7.0
SWIGLU_ALPHA = 1.702
LANES = 128
SUBLANES = 8
BF16_ROWS = 16
ML_CHUNK = 128
GLA_CHUNK = 64
GLA_PREP_ROWS = 256
CONV_ROWS = 128
EXPERT_TILE = 512
ROUTE_BLOCK = 512
RUN_ALIGN = SUBLANES
PERM_ROWS = 256
RUN_BIG = 128
RUN_SIZES = tuple(s for s in (ROUTE_BLOCK >> i for i in range(ROUTE_BLOCK.bit_length())) if s >= RUN_ALIGN)
VMEM_LIMIT = 56 * 1024 * 1024


def _params(sem):
    return pltpu.CompilerParams(dimension_semantics=sem, vmem_limit_bytes=VMEM_LIMIT)


def _sigmoid(x):
    return 1.0 / (1.0 + jnp.exp(-x))


def _log_sigmoid(x):
    return jnp.minimum(x, 0.0) - jnp.log(1.0 + jnp.exp(-jnp.abs(x)))


def _dot(a, b):
    return jnp.dot(a, b, preferred_element_type=F32)


def _split2(x):
    hi = x.astype(BF16)
    lo = (x - hi.astype(F32)).astype(BF16)
    return hi, lo


def _dot3(a, b):
    a_hi, a_lo = _split2(a)
    b_hi, b_lo = _split2(b)
    return _dot(a_hi, b_hi) + _dot(a_lo, b_hi) + _dot(a_hi, b_lo)


def _dot_nt(a, b):
    return lax.dot_general(a, b, (((1,), (1,)), ((), ())), preferred_element_type=F32)


def _dot_tn(a, b):
    return lax.dot_general(a, b, (((0,), (0,)), ((), ())), preferred_element_type=F32)


def _adaln_body(c_ref, w_ref, b_ref, o_ref):
    c = c_ref[...]
    s = c * _sigmoid(c)
    o_ref[...] = _dot3(s, w_ref[...]) + b_ref[...]


def _adaln(cond, w, b):
    rows, d = cond.shape
    n = w.shape[1]
    tn = 512
    return pl.pallas_call(
        _adaln_body,
        grid=(n // tn,),
        in_specs=[pl.BlockSpec((rows, d), lambda j: (0, 0)),
                  pl.BlockSpec((d, tn), lambda j: (0, j)),
                  pl.BlockSpec((1, tn), lambda j: (0, j))],
        out_specs=pl.BlockSpec((rows, tn), lambda j: (0, j)),
        out_shape=jax.ShapeDtypeStruct((rows, n), F32),
        compiler_params=_params(("arbitrary",)),
        name="adaln",
    )(cond, w, b.reshape(1, n))


def _rms_mod(x, g, scale, shift):
    y = x * lax.rsqrt(jnp.mean(x * x, axis=-1, keepdims=True) + EPS)
    return (y * g) * (1.0 + scale) + shift


def _inproj_body(x_ref, pe_ref, sc_ref, sh_ref, g_ref, wm_ref, wg_ref, zm_ref, zg_ref, gt_ref):
    x = x_ref[0] + pe_ref[...]
    u = _rms_mod(x, g_ref[...], sc_ref[0], sh_ref[0]).astype(BF16)
    z = _dot(u, wm_ref[...])
    zm_ref[0] = z[:, :GROUP_W].astype(BF16)
    zg_ref[0] = z[:, GROUP_W:].astype(BF16)
    gt_ref[0] = _dot(u, wg_ref[...])


def _inproj(x, pe, scale, shift, g, w_main, w_gate, tm):
    bsz, t, d = x.shape
    per_batch = scale.shape[0] == bsz and bsz > 1
    mod_map = (lambda i, b: (b, 0, 0)) if per_batch else (lambda i, b: (0, 0, 0))
    return pl.pallas_call(
        _inproj_body,
        grid=(t // tm, bsz),
        in_specs=[pl.BlockSpec((1, tm, d), lambda i, b: (b, i, 0)),
                  pl.BlockSpec((tm, d), lambda i, b: (i, 0)),
                  pl.BlockSpec((1, 1, d), mod_map),
                  pl.BlockSpec((1, 1, d), mod_map),
                  pl.BlockSpec((1, d), lambda i, b: (0, 0)),
                  pl.BlockSpec((d, 2 * GROUP_W), lambda i, b: (0, 0)),
                  pl.BlockSpec((d, GATE_W), lambda i, b: (0, 0))],
        out_specs=[pl.BlockSpec((1, tm, GROUP_W), lambda i, b: (b, i, 0)),
                   pl.BlockSpec((1, tm, GROUP_W), lambda i, b: (b, i, 0)),
                   pl.BlockSpec((1, tm, GATE_W), lambda i, b: (b, i, 0))],
        out_shape=[jax.ShapeDtypeStruct((bsz, t, GROUP_W), BF16),
                   jax.ShapeDtypeStruct((bsz, t, GROUP_W), BF16),
                   jax.ShapeDtypeStruct((bsz, t, GATE_W), F32)],
        compiler_params=_params(("arbitrary", "arbitrary")),
        name="inproj",
    )(x, pe, scale, shift, g, w_main, w_gate)


def _scan_rows(x, op, ident, reverse, rowi):
    n = x.shape[0]
    k = 1
    while k < n:
        if reverse:
            shifted = jnp.where(rowi < n - k, pltpu.roll(x, n - k, axis=0), ident)
        else:
            shifted = jnp.where(rowi >= k, pltpu.roll(x, k, axis=0), ident)
        x = op(x, shifted)
        k *= 2
    return x


HEAD_DIRS = 2 * N_HEADS


def _mlstm_body(zl_ref, zc_ref, gl_ref, gc_ref, cw_ref, cb_ref, gb_ref, ng_ref, o_ref,
                q_s, kt_s, acc_s, cc_s, bc_s, st_s, mp_s, m_s, en_s, cr_s, wk_s, wc_s, c_s, *, t_lat, t_ctx):
    L = ML_CHUNK
    nc_ctx = t_ctx // L
    nc_lat = t_lat // L
    nc = nc_ctx + nc_lat
    scale = DK ** -0.5

    cw = cw_ref[...]
    cb = cb_ref[...]

    def conv_pass(z_ref, n, dst0):
        nb = n // CONV_ROWS
        row = lax.broadcasted_iota(jnp.int32, (CONV_ROWS, QK_W), 0)

        def body(r, carry):
            r0 = pl.multiple_of(r * CONV_ROWS, CONV_ROWS)
            zc = z_ref[0, pl.ds(r0, CONV_ROWS), 0:QK_W].astype(F32)
            p0 = pl.multiple_of(jnp.maximum(r0 - BF16_ROWS, 0), BF16_ROWS)
            prev = z_ref[0, pl.ds(p0, BF16_ROWS), 0:QK_W].astype(F32)[BF16_ROWS - 1:BF16_ROWS]
            prev = jnp.where(r > 0, prev, 0.0)
            n0 = pl.multiple_of(jnp.minimum(r0 + CONV_ROWS, n - BF16_ROWS), BF16_ROWS)
            nxt = z_ref[0, pl.ds(n0, BF16_ROWS), 0:QK_W].astype(F32)[0:1]
            nxt = jnp.where(r < nb - 1, nxt, 0.0)
            up = jnp.where(row == 0, prev, pltpu.roll(zc, 1, axis=0))
            dn = jnp.where(row == CONV_ROWS - 1, nxt, pltpu.roll(zc, CONV_ROWS - 1, axis=0))
            y = cw[0:1] * up + cw[1:2] * zc + cw[2:3] * dn + cb
            y = y * _sigmoid(y)
            d0 = pl.multiple_of(dst0 + r0, CONV_ROWS)
            q_s[pl.ds(d0, CONV_ROWS), :] = (y * scale).astype(BF16)
            for h in range(N_HEADS):
                kt_s[h, :, pl.ds(d0, CONV_ROWS)] = y[:, h * LANES:(h + 1) * LANES].T[DK:2 * DK].astype(BF16)
            return carry

        lax.fori_loop(0, nb, body, 0)

    conv_pass(zc_ref, t_ctx, 0)
    conv_pass(zl_ref, t_lat, t_ctx)

    rowi = lax.broadcasted_iota(jnp.int32, (L, LANES), 0)
    lane_g = lax.broadcasted_iota(jnp.int32, (L, GATE_W), 1)
    fwd_lane = lane_g < N_HEADS
    gb = gb_ref[...]

    def gate_pass(g_ref, nchunks, g0):
        def body(i, carry):
            r0 = pl.multiple_of(i * L, L)
            gates = g_ref[0, pl.ds(r0, L), :] + gb
            lf = pltpu.roll(_log_sigmoid(gates), GATE_W - HEAD_DIRS, axis=1)
            b = jnp.where(fwd_lane, _scan_rows(lf, jnp.add, 0.0, False, rowi),
                          _scan_rows(lf, jnp.add, 0.0, True, rowi))
            c = gates - b
            cc_s[g0 + i] = c
            bc_s[g0 + i] = b
            st_s[g0 + i, 0:1, :] = jnp.max(c, axis=0, keepdims=True)
            st_s[g0 + i, 1:2, :] = jnp.sum(lf, axis=0, keepdims=True)
            return carry

        lax.fori_loop(0, nchunks, body, 0)

    gate_pass(gc_ref, nc_ctx, 0)
    gate_pass(gl_ref, nc_lat, nc_ctx)

    lane1 = lax.broadcasted_iota(jnp.int32, (1, GATE_W), 1)
    fwd_order = list(range(nc))
    bwd_order = list(range(nc_ctx - 1, -1, -1)) + list(range(nc - 1, nc_ctx - 1, -1))
    mp = {}
    for d, order in enumerate((fwd_order, bwd_order)):
        m = jnp.zeros((1, GATE_W), F32)
        for g in order:
            mp[(d, g)] = m
            m = st_s[g, 1:2, :] + jnp.maximum(m, st_s[g, 0:1, :])
    for g in range(nc):
        mp_s[g, 0:1, :] = jnp.where(lane1 < N_HEADS, mp[(0, g)], mp[(1, g)])

    def ab_body(g, carry):
        c = cc_s[g]
        b = bc_s[g]
        mprev = mp_s[g, 0:1, :]
        mloc = jnp.where(fwd_lane, _scan_rows(c, jnp.maximum, -jnp.inf, False, rowi),
                         _scan_rows(c, jnp.maximum, -jnp.inf, True, rowi))
        m = jnp.maximum(mloc, mprev)
        mlast = jnp.maximum(mprev, st_s[g, 0:1, :])
        m_s[g] = m
        en_s[g] = jnp.exp(-(b + m))
        cr_s[g] = c.T[0:HEAD_DIRS]
        wk_s[g] = jnp.exp((c - mlast).T[0:HEAD_DIRS])
        wc_s[g] = jnp.exp(jnp.broadcast_to(mprev - mlast, (L, GATE_W)).T[0:HEAD_DIRS])
        return carry

    lax.fori_loop(0, nc, ab_body, 0)

    colL = lax.broadcasted_iota(jnp.int32, (L, L), 1)
    rowL = lax.broadcasted_iota(jnp.int32, (L, L), 0)
    masks = (colL <= rowL, colL >= rowL)
    ones_blk = jnp.ones((L, DV), BF16)
    c_s[...] = jnp.zeros(c_s.shape, F32)

    def twice(x):
        return jnp.concatenate([x, x], axis=1)

    def step(g, z_ref, r_src, r_q, d, with_out, r_out):
        for h in range(N_HEADS):
            j = d * N_HEADS + h
            hs = slice(h * DV, (h + 1) * DV)
            kt = kt_s[h, :, pl.ds(r_q, L)]
            v = z_ref[0, pl.ds(r_src, L), QK_W + h * DV:QK_W + (h + 1) * DV]
            vext = jnp.concatenate([v, ones_blk], axis=1)
            state = c_s[j]
            if with_out:
                q = q_s[pl.ds(r_q, L), hs][:, 0:DK]
                mb = jnp.broadcast_to(m_s[g][:, j:j + 1], (L, L))
                w = jnp.where(masks[d], jnp.exp(cr_s[g, j:j + 1, :] - mb), 0.0)
                w_int = jnp.exp(mp_s[g][0:1, j:j + 1] - mb[:, 0:DK])
                p = jnp.concatenate([(_dot(q, kt) * w).astype(BF16), (q.astype(F32) * w_int).astype(BF16)], axis=1)
                nd = _dot(p, jnp.concatenate([vext, state.astype(BF16)], axis=0))
                hh = nd[:, :DV] / jnp.maximum(jnp.abs(nd[:, DV:]), en_s[g][:, j:j + 1])
                acc_s[d, pl.ds(r_out, L), hs] = hh
            ktw = (kt.astype(F32) * wk_s[g, j:j + 1, :]).astype(BF16)
            c_s[j] = twice(wc_s[g, j:j + 1, :]) * state + _dot(ktw, vext)

    def ctx_body(i, carry):
        for d in (0, 1):
            ii = i if d == 0 else nc_ctx - 1 - i
            r0 = pl.multiple_of(ii * L, L)
            step(ii, zc_ref, r0, r0, d, False, None)
        return carry

    def lat_body(i, carry):
        for d in (0, 1):
            ii = i if d == 0 else nc_lat - 1 - i
            r0 = pl.multiple_of(ii * L, L)
            step(nc_ctx + ii, zl_ref, r0, pl.multiple_of(t_ctx + ii * L, L), d, True, r0)
        return carry

    lax.fori_loop(0, nc_ctx, ctx_body, 0)
    lax.fori_loop(0, nc_lat, lat_body, 0)

    def out_body(i, carry):
        r0 = pl.multiple_of(i * L, L)
        for h in range(N_HEADS):
            hs = slice(h * DV, (h + 1) * DV)
            tot = acc_s[0, pl.ds(r0, L), hs] + acc_s[1, pl.ds(r0, L), hs]
            tot = tot * lax.rsqrt(jnp.mean(tot * tot, axis=-1, keepdims=True) + EPS)
            og = zl_ref[0, pl.ds(r0, L), QK_W + V_W + h * DV:QK_W + V_W + (h + 1) * DV].astype(F32)
            o_ref[0, pl.ds(r0, L), hs] = (tot * ng_ref[:, hs] * _sigmoid(og)).astype(BF16)
        return carry

    lax.fori_loop(0, nc_lat, out_body, 0)


def _mlstm(zm_lat, zm_ctx, gt_lat, gt_ctx, conv_w, conv_b, gate_b, norm_g):
    bsz, t_lat, _ = zm_lat.shape
    t_ctx = zm_ctx.shape[1]
    nc = (t_lat + t_ctx) // ML_CHUNK
    body = functools.partial(_mlstm_body, t_lat=t_lat, t_ctx=t_ctx)
    full = lambda b: (0, 0)
    return pl.pallas_call(
        body,
        grid=(bsz,),
        in_specs=[pl.BlockSpec((1, t_lat, GROUP_W), lambda b: (b, 0, 0)),
                  pl.BlockSpec((1, t_ctx, GROUP_W), lambda b: (b, 0, 0)),
                  pl.BlockSpec((1, t_lat, GATE_W), lambda b: (b, 0, 0)),
                  pl.BlockSpec((1, t_ctx, GATE_W), lambda b: (b, 0, 0)),
                  pl.BlockSpec((ML_CONV, QK_W), full),
                  pl.BlockSpec((1, QK_W), full),
                  pl.BlockSpec((1, GATE_W), full),
                  pl.BlockSpec((1, V_W), full)],
        out_specs=pl.BlockSpec((1, t_lat, V_W), lambda b: (b, 0, 0)),
        out_shape=jax.ShapeDtypeStruct((bsz, t_lat, V_W), BF16),
        scratch_shapes=[pltpu.VMEM((t_ctx + t_lat, QK_W), BF16),
                        pltpu.VMEM((N_HEADS, DK, t_ctx + t_lat), BF16),
                        pltpu.VMEM((2, t_lat, V_W), F32),
                        pltpu.VMEM((nc, ML_CHUNK, GATE_W), F32),
                        pltpu.VMEM((nc, ML_CHUNK, GATE_W), F32),
                        pltpu.VMEM((nc, SUBLANES, GATE_W), F32),
                        pltpu.VMEM((nc, SUBLANES, GATE_W), F32),
                        pltpu.VMEM((nc, ML_CHUNK, GATE_W), F32),
                        pltpu.VMEM((nc, ML_CHUNK, GATE_W), F32),
                        pltpu.VMEM((nc, HEAD_DIRS, ML_CHUNK), F32),
                        pltpu.VMEM((nc, HEAD_DIRS, ML_CHUNK), F32),
                        pltpu.VMEM((nc, HEAD_DIRS, GATE_W), F32),
                        pltpu.VMEM((HEAD_DIRS, DK, 2 * DV), F32)],
        compiler_params=_params(("arbitrary",)),
        name="mlstm",
    )(zm_lat, zm_ctx, gt_lat, gt_ctx, conv_w, conv_b, gate_b, norm_g)


def _gla_body(zl_ref, zc_ref, gl_ref, gc_ref, w2_ref, b2_ref, ng_ref, o_ref,
              acc_s, s_s, qt_s, kt_s, ee_s, *, t_lat, t_ctx):
    L = GLA_CHUNK
    nc_ctx = t_ctx // L
    nc_lat = t_lat // L
    scale = DK ** -0.5
    half = N_HEADS * DK

    rowi = lax.broadcasted_iota(jnp.int32, (L, L), 0)
    coli = lax.broadcasted_iota(jnp.int32, (L, L), 1)
    masks = (coli <= rowi, coli >= rowi)
    tri_b = (masks[0].astype(BF16), masks[1].astype(BF16))
    low = lax.broadcasted_iota(jnp.int32, (L, LANES), 1) < DK
    w2_hi, w2_lo = _split2(w2_ref[...])
    bias2 = b2_ref[...]

    def prep(z_ref, g_ref, n_rows, c0, row0):
        grp = math.gcd(n_rows, GLA_PREP_ROWS)

        def body(gi, carry):
            rg = pl.multiple_of(gi * grp, grp)
            g_hi, g_lo = _split2(g_ref[0, pl.ds(rg, grp), :])
            pre = _dot(g_hi, w2_hi) + _dot(g_lo, w2_hi) + _dot(g_hi, w2_lo) + bias2
            la_hi, la_lo = _split2(_log_sigmoid(pre) * (1.0 / GLA_NORMALIZER))
            for cc in range(grp // L):
                rs = slice(cc * L, (cc + 1) * L)
                r0 = pl.multiple_of(rg + cc * L, L)
                rq = pl.multiple_of(row0 + rg + cc * L, L)
                c = c0 + gi * (grp // L) + cc
                for d in (0, 1):
                    ds_ = slice(d * half, (d + 1) * half)
                    b_all = _dot(tri_b[d], la_hi[rs, ds_]) + _dot(tri_b[d], la_lo[rs, ds_])
                    for hp in range(N_HEADS // 2):
                        pair = b_all[:, hp * LANES:(hp + 1) * LANES]
                        swapped = pltpu.roll(pair, DK, axis=1)
                        for hh in range(2):
                            h = 2 * hp + hh
                            hs = slice(h * LANES, (h + 1) * LANES)
                            b2 = jnp.where(low, pair, swapped) if hh == 0 else jnp.where(low, swapped, pair)
                            b_end = b2[L - 1:L] if d == 0 else b2[0:1]
                            qk = z_ref[0, pl.ds(r0, L), hs].astype(F32)
                            qkt = qk * jnp.exp(jnp.where(low, b2, -b2))
                            qt_s[d, pl.ds(rq, L), hs] = jnp.where(low, qkt * scale, 0.0).astype(BF16)
                            kt_s[d, pl.ds(rq, L), hs] = jnp.where(low, pltpu.roll(qkt, DK, axis=1), 0.0).astype(BF16)
                            ee_s[d, c, 0:1, hs] = jnp.exp(b_end)
            return carry

        lax.fori_loop(0, n_rows // grp, body, 0)

    prep(zc_ref, gc_ref, t_ctx, 0, 0)
    prep(zl_ref, gl_ref, t_lat, nc_ctx, t_ctx)

    s_s[...] = jnp.zeros(s_s.shape, F32)

    def step(z_ref, r0, rq, c, d, with_out):
        for h in range(N_HEADS):
            j = d * N_HEADS + h
            hs = slice(h * LANES, (h + 1) * LANES)
            qt = qt_s[d, pl.ds(rq, L), hs]
            kt = kt_s[d, pl.ds(rq, L), hs]
            v = z_ref[0, pl.ds(r0, L), QK_W + h * DV:QK_W + (h + 1) * DV]
            state = s_s[j]
            if with_out:
                a = jnp.where(masks[d], _dot_nt(qt, kt), 0.0).astype(BF16)
                acc_s[d, pl.ds(r0, L), hs] = _dot(a, v) + _dot_nt(qt, state.astype(BF16))
            e_end = ee_s[d, c, 0:1, hs]
            s_s[j] = state * e_end + _dot_tn(v, (kt.astype(F32) * e_end).astype(BF16))

    def loop(z_ref, n, c0, row0, with_out):
        def body(i, carry):
            for d in (0, 1):
                ii = i if d == 0 else n - 1 - i
                r0 = pl.multiple_of(ii * L, L)
                step(z_ref, r0, pl.multiple_of(row0 + ii * L, L), c0 + ii, d, with_out)
            return carry
        lax.fori_loop(0, n, body, 0, unroll=4)

    loop(zc_ref, nc_ctx, 0, 0, False)
    loop(zl_ref, nc_lat, nc_ctx, t_ctx, True)

    def out_body(i, carry):
        r0 = pl.multiple_of(i * L, L)
        for h in range(N_HEADS):
            hs = slice(h * LANES, (h + 1) * LANES)
            tot = acc_s[0, pl.ds(r0, L), hs] + acc_s[1, pl.ds(r0, L), hs]
            tot = tot * lax.rsqrt(jnp.mean(tot * tot, axis=-1, keepdims=True) + EPS)
            gg = zl_ref[0, pl.ds(r0, L), QK_W + V_W + h * DV:QK_W + V_W + (h + 1) * DV].astype(F32)
            o_ref[0, pl.ds(r0, L), hs] = (tot * ng_ref[:, hs] * (gg * _sigmoid(gg))).astype(BF16)
        return carry

    lax.fori_loop(0, nc_lat, out_body, 0)


def _gla(zg_lat, zg_ctx, gt_lat, gt_ctx, w2_ext, b2_ext, norm_g):
    bsz, t_lat, _ = zg_lat.shape
    t_ctx = zg_ctx.shape[1]
    body = functools.partial(_gla_body, t_lat=t_lat, t_ctx=t_ctx)
    full = lambda b: (0, 0)
    return pl.pallas_call(
        body,
        grid=(bsz,),
        in_specs=[pl.BlockSpec((1, t_lat, GROUP_W), lambda b: (b, 0, 0)),
                  pl.BlockSpec((1, t_ctx, GROUP_W), lambda b: (b, 0, 0)),
                  pl.BlockSpec((1, t_lat, GATE_W), lambda b: (b, 0, 0)),
                  pl.BlockSpec((1, t_ctx, GATE_W), lambda b: (b, 0, 0)),
                  pl.BlockSpec((GATE_W, 2 * N_HEADS * DK), full),
                  pl.BlockSpec((1, 2 * N_HEADS * DK), full),
                  pl.BlockSpec((1, V_W), full)],
        out_specs=pl.BlockSpec((1, t_lat, V_W), lambda b: (b, 0, 0)),
        out_shape=jax.ShapeDtypeStruct((bsz, t_lat, V_W), BF16),
        scratch_shapes=[pltpu.VMEM((2, t_lat, V_W), F32),
                        pltpu.VMEM((HEAD_DIRS, DV, LANES), F32),
                        pltpu.VMEM((2, t_ctx + t_lat, QK_W), BF16),
                        pltpu.VMEM((2, t_ctx + t_lat, QK_W), BF16),
                        pltpu.VMEM((2, (t_ctx + t_lat) // GLA_CHUNK, SUBLANES, QK_W), F32)],
        compiler_params=_params(("arbitrary",)),
        name="gla",
    )(zg_lat, zg_ctx, gt_lat, gt_ctx, w2_ext, b2_ext, norm_g)


def _outproj_body(x_ref, pe_ref, a_ref, g_ref, wo_ref, g1_ref, sc_ref, sh_ref, n2_ref, rw_ref, rb_ref,
                  x1_ref, u2_ref, route_ref, cnt_ref):
    x = x_ref[0] + pe_ref[...]
    y = _dot(a_ref[0], wo_ref[0:V_W, :]) + _dot(g_ref[0], wo_ref[V_W:2 * V_W, :])
    x1 = x + g1_ref[0] * y
    x1_ref[0] = x1
    u2 = _rms_mod(x1, n2_ref[...], sc_ref[0], sh_ref[0])
    u2_ref[0] = u2
    u_hi, u_lo = _split2(u2)
    w_hi, w_lo = _split2(rw_ref[...])
    hh = _dot(u_hi, jnp.concatenate([w_hi, w_lo], axis=1))
    logits = hh[:, :LANES] + _dot(u_lo, w_hi) + hh[:, LANES:] + rb_ref[...]
    lane = lax.broadcasted_iota(jnp.int32, logits.shape, 1).astype(F32)
    ids = []
    vals = []
    for _ in range(TOP_K):
        m = jnp.max(logits, axis=1, keepdims=True)
        idx = jnp.min(jnp.where(logits == m, lane, float(LANES)), axis=1, keepdims=True)
        ids.append(idx)
        vals.append(m)
        logits = jnp.where(lane == idx, -jnp.inf, logits)
    exps = [jnp.exp(v - vals[0]) for v in vals]
    denom = exps[0]
    for e in exps[1:]:
        denom = denom + e
    route = jnp.zeros(logits.shape, F32)
    for k in range(TOP_K):
        route = jnp.where(lane == float(k), ids[k], route)
        route = jnp.where(lane == float(TOP_K + k), exps[k] / denom, route)
    route_ref[0] = route

    @pl.when((pl.program_id(0) == 0) & (pl.program_id(1) == 0))
    def _():
        cnt_ref[...] = jnp.zeros(cnt_ref.shape, F32)

    cnt_ref[...] += _block_picks(route)[3]


def _outproj(x, pe, a, g, w_out, gate1, scale2, shift2, norm2_g, router_w, router_b, tm):
    bsz, t, d = x.shape
    tok = lambda i, b: (b, i, 0)
    mod = lambda i, b: (b, 0, 0)
    full = lambda i, b: (0, 0)
    return pl.pallas_call(
        _outproj_body,
        grid=(t // tm, bsz),
        in_specs=[pl.BlockSpec((1, tm, d), tok),
                  pl.BlockSpec((tm, d), lambda i, b: (i, 0)),
                  pl.BlockSpec((1, tm, V_W), tok),
                  pl.BlockSpec((1, tm, V_W), tok),
                  pl.BlockSpec((2 * V_W, d), full),
                  pl.BlockSpec((1, 1, d), mod),
                  pl.BlockSpec((1, 1, d), mod),
                  pl.BlockSpec((1, 1, d), mod),
                  pl.BlockSpec((1, d), full),
                  pl.BlockSpec((d, LANES), full),
                  pl.BlockSpec((1, LANES), full)],
        out_specs=[pl.BlockSpec((1, tm, d), tok),
                   pl.BlockSpec((1, tm, d), tok),
                   pl.BlockSpec((1, tm, LANES), tok),
                   pl.BlockSpec((1, LANES), full)],
        out_shape=[jax.ShapeDtypeStruct((bsz, t, d), F32),
                   jax.ShapeDtypeStruct((bsz, t, d), F32),
                   jax.ShapeDtypeStruct((bsz, t, LANES), F32),
                   jax.ShapeDtypeStruct((1, LANES), F32)],
        compiler_params=_params(("arbitrary", "arbitrary")),
        name="outproj_router",
    )(x, pe, a, g, w_out, gate1, scale2, shift2, norm2_g, router_w, router_b)


def _lane_excl_cumsum(x):
    x8 = jnp.broadcast_to(x, (SUBLANES, LANES))
    lane8 = lax.broadcasted_iota(jnp.int32, (SUBLANES, LANES), 1)
    inc = x8
    s = 1
    while s < LANES:
        inc = inc + jnp.where(lane8 >= s, pltpu.roll(inc, s, axis=1), 0.0)
        s *= 2
    return (inc - x8)[0:1]


def _block_picks(route):
    lane = lax.broadcasted_iota(jnp.int32, route.shape, 1).astype(F32)
    onehots = [lane == route[:, k:k + 1] for k in range(TOP_K)]
    oh = onehots[0].astype(F32)
    for o in onehots[1:]:
        oh = oh + o.astype(F32)
    run_len = jnp.ceil(jnp.sum(oh, axis=0, keepdims=True) * (1.0 / RUN_ALIGN)) * float(RUN_ALIGN)
    return lane, onehots, oh, run_len


def _route_pos_body(route_ref, cnt_ref, lic_ref, lit_ref, tab_ref, off_s, *, tile):
    tb = route_ref.shape[0]
    lane, onehots, oh, run_len = _block_picks(route_ref[...])

    @pl.when(pl.program_id(0) == 0)
    def _():
        cpad = jnp.floor((cnt_ref[...] + float(tile - 1)) * (1.0 / tile)) * float(tile)
        off_s[...] = _lane_excl_cumsum(cpad)

    r = lax.broadcasted_iota(jnp.int32, (tb, tb), 0)
    c = lax.broadcasted_iota(jnp.int32, (tb, tb), 1)
    before = _dot((c < r).astype(BF16), oh.astype(BF16))
    lstart = _lane_excl_cumsum(run_len)
    local = lstart + before
    li = jnp.zeros((tb, LANES), F32)
    for k in range(TOP_K):
        lk = jnp.sum(jnp.where(onehots[k], local, 0.0), axis=1, keepdims=True)
        li = jnp.where(lane == float(k), lk, li)
    lic_ref[...] = li
    lit_ref[0] = li.T[0:SUBLANES]
    row8 = lax.broadcasted_iota(jnp.int32, (SUBLANES, LANES), 0)
    total = jnp.sum(run_len, axis=1, keepdims=True)
    tab = jnp.where(row8 == 0, off_s[...],
                    jnp.where(row8 == 1, run_len, jnp.where(row8 == 2, lstart, jnp.where(row8 == 3, total, 0.0))))
    tab_ref[0] = tab.astype(jnp.int32)
    off_s[...] += run_len


def _route_pos(route, counts, tile):
    n = route.shape[0]
    tb = ROUTE_BLOCK
    nb = n // tb
    return pl.pallas_call(
        functools.partial(_route_pos_body, tile=tile),
        grid=(nb,),
        in_specs=[pl.BlockSpec((tb, LANES), lambda j: (j, 0)),
                  pl.BlockSpec((1, LANES), lambda j: (0, 0))],
        out_specs=[pl.BlockSpec((tb, LANES), lambda j: (j, 0)),
                   pl.BlockSpec((1, SUBLANES, tb), lambda j: (j, 0, 0)),
                   pl.BlockSpec((1, SUBLANES, LANES), lambda j: (j, 0, 0))],
        out_shape=[jax.ShapeDtypeStruct((n, LANES), F32),
                   jax.ShapeDtypeStruct((nb, SUBLANES, tb), F32),
                   jax.ShapeDtypeStruct((nb, SUBLANES, LANES), jnp.int32)],
        scratch_shapes=[pltpu.VMEM((1, LANES), F32)],
        compiler_params=_params(("arbitrary",)),
        name="route_pos",
    )(route, counts)


def _issue_runs(tab_ref, n_experts, make_copy):
    def per_expert(e, carry):
        g = tab_ref[0, 0, e]
        cnt = tab_ref[0, 1, e]
        l = tab_ref[0, 2, e]
        def pieces(l, g, sizes):
            for size in sizes:
                part = cnt & size

                @pl.when(part != 0)
                def _():
                    make_copy(pl.multiple_of(l, RUN_ALIGN), pl.multiple_of(g, RUN_ALIGN), size).start()

                l = l + part
                g = g + part

        big = cnt - (cnt & (RUN_BIG - 1))

        @pl.when(big != 0)
        def _():
            pieces(l, g, [s for s in RUN_SIZES if s >= RUN_BIG])

        pieces(l + big, g + big, [s for s in RUN_SIZES if s < RUN_BIG])
        return carry

    lax.fori_loop(0, n_experts, per_expert, 0)


def _dispatch_body(zrow_ref, zflag_ref, nu_ref, tab_ref, lit_ref, u_ref, xs_ref, buf, zero_s, tot_s, sem, zsem,
                   *, n_experts, tile):
    i = pl.program_id(0)
    nb = pl.num_programs(0)
    tb = u_ref.shape[0]
    rows = buf.shape[1]
    n_tiles = xs_ref.shape[0] // tile
    slot = i % 2

    @pl.when(i == 0)
    def _():
        zero_s[...] = jnp.zeros(zero_s.shape, F32)

        def zero_tile(r0):
            cp = pltpu.make_async_copy(zero_s, xs_ref.at[pl.ds(pl.multiple_of(r0, tile), tile)], zsem)
            cp.start()
            cp.wait()

        for e in range(n_experts):
            @pl.when(zflag_ref[e] == 1)
            def _():
                zero_tile(zrow_ref[e])

        def tail(t, carry):
            zero_tile(t * tile)
            return carry

        lax.fori_loop(nu_ref[0], n_tiles, tail, 0)

    def wait_slot(s):
        nrows = pl.multiple_of(tot_s[s], RUN_ALIGN)
        pltpu.make_async_copy(buf.at[s, pl.ds(0, nrows)], xs_ref.at[pl.ds(0, nrows)], sem.at[s]).wait()

    @pl.when(i >= 2)
    def _():
        wait_slot(slot)

    tot_s[slot] = tab_ref[0, 3, 0]

    lit = lit_ref[0]
    ub = u_ref[...].astype(BF16)
    ri0 = lax.broadcasted_iota(jnp.int32, (PERM_ROWS, tb), 0).astype(F32)
    for c in range(rows // PERM_ROWS):
        ri = ri0 + float(c * PERM_ROWS)
        hit = jnp.zeros((PERM_ROWS, tb), F32)
        for k in range(TOP_K):
            hit = jnp.where(ri == lit[k:k + 1], 1.0, hit)
        buf[slot, c * PERM_ROWS:(c + 1) * PERM_ROWS] = _dot(hit.astype(BF16), ub)
    _issue_runs(tab_ref, n_experts,
                lambda l, g, size: pltpu.make_async_copy(buf.at[slot, pl.ds(l, size)], xs_ref.at[pl.ds(g, size)],
                                                         sem.at[slot]))

    @pl.when(i == nb - 1)
    def _():
        wait_slot(slot)

    @pl.when((i == nb - 1) & (nb > 1))
    def _():
        wait_slot(1 - slot)


def _dispatch(u2, tab, lit, zrow, zflag, n_used, n_rows, tile):
    n, d = u2.shape
    tb = ROUTE_BLOCK
    nb = n // tb
    n_experts = zrow.shape[0]
    grid_spec = pltpu.PrefetchScalarGridSpec(
        num_scalar_prefetch=3,
        grid=(nb,),
        in_specs=[pl.BlockSpec((1, SUBLANES, LANES), lambda i, zr, zf, nu: (i, 0, 0), memory_space=pltpu.SMEM),
                  pl.BlockSpec((1, SUBLANES, tb), lambda i, zr, zf, nu: (i, 0, 0)),
                  pl.BlockSpec((tb, d), lambda i, zr, zf, nu: (i, 0))],
        out_specs=pl.BlockSpec(memory_space=pl.ANY),
        scratch_shapes=[pltpu.VMEM((2, tb * TOP_K + n_experts * RUN_ALIGN, d), F32), pltpu.VMEM((tile, d), F32),
                        pltpu.SMEM((2,), jnp.int32), pltpu.SemaphoreType.DMA((2,)), pltpu.SemaphoreType.DMA(())],
    )
    return pl.pallas_call(
        functools.partial(_dispatch_body, n_experts=n_experts, tile=tile),
        grid_spec=grid_spec,
        out_shape=jax.ShapeDtypeStruct((n_rows, d), F32),
        compiler_params=_params(("arbitrary",)),
        name="dispatch",
    )(zrow, zflag, n_used, tab, lit, u2)


def _experts_body(te_ref, nu_ref, x_ref, wgu_ref, bgu_ref, wd_ref, bd_ref, y_ref, wgu_s, wd_s, *, d_expert):
    t = pl.program_id(0)

    @pl.when((t == 0) | (te_ref[t] != te_ref[jnp.maximum(t - 1, 0)]))
    def _():
        wgu_s[...] = wgu_ref[0].astype(BF16)
        wd_s[...] = wd_ref[0].astype(BF16)

    @pl.when(t < nu_ref[0])
    def _():
        gu = _dot(x_ref[...].astype(BF16), wgu_s[...]) + bgu_ref[0]
        gate = jnp.minimum(gu[:, :d_expert], SWIGLU_LIMIT)
        up = jnp.clip(gu[:, d_expert:], -SWIGLU_LIMIT, SWIGLU_LIMIT)
        act = ((up + 1.0) * (gate * _sigmoid(SWIGLU_ALPHA * gate))).astype(BF16)
        y_ref[...] = _dot(act, wd_s[...]) + bd_ref[0]

    @pl.when(t >= nu_ref[0])
    def _():
        y_ref[...] = jnp.zeros(y_ref.shape, F32)


def _experts(xs, tile_expert, n_used, w_gu, b_gu, w_down, b_down, tile):
    n_rows, d = xs.shape
    n_experts, _, two_de = w_gu.shape
    d_expert = two_de // 2
    grid_spec = pltpu.PrefetchScalarGridSpec(
        num_scalar_prefetch=2,
        grid=(n_rows // tile,),
        in_specs=[pl.BlockSpec((tile, d), lambda t, te, nu: (jnp.minimum(t, nu[0] - 1), 0)),
                  pl.BlockSpec((1, d, two_de), lambda t, te, nu: (te[t], 0, 0)),
                  pl.BlockSpec((1, 1, two_de), lambda t, te, nu: (te[t], 0, 0)),
                  pl.BlockSpec((1, d_expert, d), lambda t, te, nu: (te[t], 0, 0)),
                  pl.BlockSpec((1, 1, d), lambda t, te, nu: (te[t], 0, 0))],
        out_specs=pl.BlockSpec((tile, d), lambda t, te, nu: (t, 0)),
        scratch_shapes=[pltpu.VMEM((d, two_de), BF16), pltpu.VMEM((d_expert, d), BF16)],
    )
    return pl.pallas_call(
        functools.partial(_experts_body, d_expert=d_expert),
        grid_spec=grid_spec,
        out_shape=jax.ShapeDtypeStruct((n_rows, d), F32),
        compiler_params=_params(("arbitrary",)),
        name="experts",
    )(tile_expert, n_used, xs, w_gu, b_gu, w_down, b_down)


def _combine_body(tab_ref, tabn_ref, lic_ref, route_ref, x1_ref, g5_ref, fg_ref, ys_ref, o_ref, buf, sem,
                  *, n_experts):
    i = pl.program_id(0)
    n = pl.num_programs(0)
    tb = x1_ref.shape[0]
    rows = buf.shape[1]
    slot = i % 2

    def issue(t_ref, s):
        _issue_runs(t_ref, n_experts,
                    lambda l, g, size: pltpu.make_async_copy(ys_ref.at[pl.ds(g, size)], buf.at[s, pl.ds(l, size)],
                                                             sem.at[s]))

    @pl.when(i == 0)
    def _():
        buf[...] = jnp.zeros(buf.shape, F32)
        issue(tab_ref, 0)

    @pl.when(i + 1 < n)
    def _():
        issue(tabn_ref, 1 - slot)

    nrows = pl.multiple_of(tab_ref[0, 3, 0], RUN_ALIGN)
    pltpu.make_async_copy(ys_ref.at[pl.ds(0, nrows)], buf.at[slot, pl.ds(0, nrows)], sem.at[slot]).wait()

    lane = lax.broadcasted_iota(jnp.int32, (tb, LANES), 1)
    li_hi, li_lo = _split2(lic_ref[...])
    a = jnp.where(lane < TOP_K, li_hi.astype(F32),
                  jnp.where(lane < 2 * TOP_K, pltpu.roll(li_lo.astype(F32), TOP_K, axis=1),
                            jnp.where(lane < 3 * TOP_K, pltpu.roll(route_ref[...], TOP_K, axis=1), 0.0))).astype(BF16)
    rsel = lax.broadcasted_iota(jnp.int32, (LANES, LANES), 0)
    lane_f = lane.astype(F32)
    idx, gate = [], []
    for k in range(TOP_K):
        sel = jnp.concatenate([((rsel == k) | (rsel == TOP_K + k)).astype(BF16),
                               (rsel == 2 * TOP_K + k).astype(BF16)], axis=1)
        bk = _dot(a, sel)
        idx.append(bk[:, :LANES])
        gate.append(bk[:, LANES:])
    moe = jnp.zeros(x1_ref.shape, F32)
    for c0 in range(0, rows, PERM_ROWS):
        blocks = []
        for c in range(c0, c0 + PERM_ROWS, LANES):
            col = lane_f + float(c)
            w = jnp.zeros((tb, LANES), F32)
            for k in range(TOP_K):
                w = jnp.where(idx[k] == col, gate[k], w)
            blocks.append(w.astype(BF16))
        moe = moe + _dot(jnp.concatenate(blocks, axis=1), buf[slot, c0:c0 + PERM_ROWS].astype(BF16))
    xo = x1_ref[...] + g5_ref[0] * moe
    o_ref[...] = xo * lax.rsqrt(jnp.mean(xo * xo, axis=-1, keepdims=True) + EPS) * fg_ref[...]


def _combine(ys, tab, lic, route, x1, gate5, final_g, t_lat, n_experts):
    n, d = x1.shape
    tb = ROUTE_BLOCK
    nb = n // tb
    per_b = t_lat // tb
    return pl.pallas_call(
        functools.partial(_combine_body, n_experts=n_experts),
        grid=(nb,),
        in_specs=[pl.BlockSpec((1, SUBLANES, LANES), lambda i: (i, 0, 0), memory_space=pltpu.SMEM),
                  pl.BlockSpec((1, SUBLANES, LANES), lambda i: (jnp.minimum(i + 1, nb - 1), 0, 0),
                               memory_space=pltpu.SMEM),
                  pl.BlockSpec((tb, LANES), lambda i: (i, 0)),
                  pl.BlockSpec((tb, LANES), lambda i: (i, 0)),
                  pl.BlockSpec((tb, d), lambda i: (i, 0)),
                  pl.BlockSpec((1, 1, d), lambda i: (i // per_b, 0, 0)),
                  pl.BlockSpec((1, d), lambda i: (0, 0)),
                  pl.BlockSpec(memory_space=pl.ANY)],
        out_specs=pl.BlockSpec((tb, d), lambda i: (i, 0)),
        out_shape=jax.ShapeDtypeStruct((n, d), F32),
        scratch_shapes=[pltpu.VMEM((2, tb * TOP_K + n_experts * RUN_ALIGN, d), F32), pltpu.SemaphoreType.DMA((2,))],
        compiler_params=_params(("arbitrary",)),
        name="combine",
    )(tab, tab, lic, route, x1, gate5, final_g, ys)


def _moe(u2, route, counts, x1, w_gu, b_gu, w_down, b_down, gate5, final_g):
    bsz, t_lat, d = x1.shape
    n = bsz * t_lat
    n_experts = w_gu.shape[0]
    tile = EXPERT_TILE
    route2 = route.reshape(n, LANES)
    lic, lit, tab = _route_pos(route2, counts, tile)
    cnt = counts[0, :n_experts].astype(jnp.int32)
    cpad = ((cnt + tile - 1) // tile) * tile
    ends = jnp.cumsum(cpad)
    n_tiles = -(-(n * TOP_K + (n // ROUTE_BLOCK) * n_experts * RUN_ALIGN) // tile) + n_experts
    n_rows = n_tiles * tile
    tile_expert = jnp.minimum(
        jnp.sum(jnp.arange(n_tiles, dtype=jnp.int32)[:, None] * tile >= ends[None, :], axis=1), n_experts - 1
    ).astype(jnp.int32)
    n_used = (ends[-1:] // tile).astype(jnp.int32)
    zrow = jnp.maximum(ends - tile, 0).astype(jnp.int32)
    zflag = (cnt % tile != 0).astype(jnp.int32)
    xs = _dispatch(u2.reshape(n, d), tab, lit, zrow, zflag, n_used, n_rows, tile)
    ys = _experts(xs, tile_expert, n_used, w_gu, b_gu, w_down, b_down, tile)
    out = _combine(ys, tab, lic, route2, x1.reshape(n, d), gate5, final_g, t_lat, n_experts)
    return out.reshape(bsz, t_lat, d)


def _grid_sincos(rows, d):
    nf = d // 4
    omega = 1.0 / (POS_BASE ** (jnp.arange(nf, dtype=F32) / nf))
    r = jnp.broadcast_to(jnp.arange(rows, dtype=F32)[:, None, None] * omega, (rows, GRID_W, nf))
    cl = jnp.broadcast_to(jnp.arange(GRID_W, dtype=F32)[None, :, None] * omega, (rows, GRID_W, nf))
    pe = jnp.concatenate([jnp.sin(r), jnp.cos(r), jnp.sin(cl), jnp.cos(cl)], axis=-1)
    return pe.reshape(rows * GRID_W, d)


def _pack_qk(wq, wk):
    lead = wq.shape[:-1]
    q = wq.reshape(lead + (N_HEADS, DK))
    k = wk.reshape(lead + (N_HEADS, DK))
    return jnp.concatenate([q, k], axis=-1).reshape(lead + (QK_W,))


def _row_tile(t):
    for tm in (512, 256, 128):
        if t % tm == 0:
            return tm
    raise ValueError(f"sequence length {t} must be a multiple of 128")


def kernel(x, c, ctx, c_ctx, ada_w, ada_b, norm1_g, w_in, ml_conv_w, ml_conv_b, ml_gate_b, ml_norm_g,
           gla_gate_w2, gla_gate_b, gla_norm_g, w_out, norm2_g, router_w, router_b, moe_w_gu, moe_b_gu,
           moe_w_down, moe_b_down, final_norm_g):
    bsz, t_lat, d = x.shape
    t_ctx = ctx.shape[1]
    assert ada_w.shape[0] == 1, "single-layer block"
    assert t_lat % ML_CHUNK == 0 and t_ctx % ML_CHUNK == 0 and t_lat % GRID_W == 0 and t_lat % ROUTE_BLOCK == 0
    n_experts = router_w.shape[-1]
    assert n_experts <= LANES

    ml_qk, ml_v = N_HEADS * DK, N_HEADS * DV
    sizes = (ml_qk, ml_qk, ml_v, ml_v, 4 * N_HEADS, ml_qk, ml_qk, ml_v, ml_v, 2 * GLA_RANK)
    offs = [int(o) for o in np.cumsum(sizes)[:-1]]
    w_mq, w_mk, w_mv, w_mo, w_mg, w_gq, w_gk, w_gv, w_gg, w_glr = jnp.split(w_in[0], offs, axis=-1)
    w_main = jnp.concatenate([_pack_qk(w_mq, w_mk), w_mv, w_mo, _pack_qk(w_gq, w_gk), w_gv, w_gg],
                             axis=-1).astype(BF16)
    gate_perm = np.concatenate([np.arange(N_HEADS) + g * N_HEADS for g in (0, 2, 1, 3)])
    w_gate = jnp.concatenate([w_mg[:, gate_perm], w_glr,
                              jnp.zeros((d, GATE_W - 4 * N_HEADS - 2 * GLA_RANK), F32)], axis=-1).astype(BF16)
    gate_b = jnp.concatenate([ml_gate_b[0].reshape(-1)[gate_perm],
                              jnp.zeros((GATE_W - 4 * N_HEADS,), F32)]).reshape(1, GATE_W)
    conv_w = _pack_qk(ml_conv_w[0][:, :ml_qk], ml_conv_w[0][:, ml_qk:])
    conv_b = _pack_qk(ml_conv_b[0][:ml_qk], ml_conv_b[0][ml_qk:]).reshape(1, QK_W)
    gla_qk = N_HEADS * DK
    w2_ext = jnp.zeros((GATE_W, 2 * gla_qk), F32)
    for dd in range(2):
        r0 = 4 * N_HEADS + dd * GLA_RANK
        w2_ext = w2_ext.at[r0:r0 + GLA_RANK, dd * gla_qk:(dd + 1) * gla_qk].set(gla_gate_w2[0, dd])
    b2_ext = gla_gate_b[0].reshape(1, 2 * gla_qk)
    router_w_p = jnp.concatenate([router_w[0], jnp.zeros((d, LANES - n_experts), F32)], axis=-1)
    router_b_p = jnp.concatenate([router_b[0], jnp.full((LANES - n_experts,), -jnp.inf, F32)]).reshape(1, LANES)

    cond = jnp.concatenate([c, c_ctx[None, :], jnp.zeros(((-bsz - 1) % SUBLANES, d), F32)], axis=0)
    mod = _adaln(cond, ada_w[0], ada_b[0]).reshape(cond.shape[0], N_MOD, 1, d)
    m_lat = mod[:bsz]
    m_ctx = mod[bsz:bsz + 1]

    pe = _grid_sincos(t_lat // GRID_W, d)
    g1 = norm1_g[0].reshape(1, d)
    tm = _row_tile(t_lat)
    zm_lat, zg_lat, gt_lat = _inproj(x, pe, m_lat[:, 1], m_lat[:, 0], g1, w_main, w_gate, tm)
    zm_ctx, zg_ctx, gt_ctx = _inproj(ctx, jnp.zeros((t_ctx, d), F32), m_ctx[:, 1], m_ctx[:, 0], g1,
                                     w_main, w_gate, _row_tile(t_ctx))

    a = _mlstm(zm_lat, zm_ctx, gt_lat, gt_ctx, conv_w, conv_b, gate_b, ml_norm_g[0].reshape(1, V_W))
    g = _gla(zg_lat, zg_ctx, gt_lat, gt_ctx, w2_ext, b2_ext, gla_norm_g[0].reshape(1, V_W))

    assert tm == ROUTE_BLOCK
    x1, u2, route, counts = _outproj(x, pe, a, g, w_out[0].astype(BF16), m_lat[:, 2], m_lat[:, 4], m_lat[:, 3],
                                     norm2_g[0].reshape(1, d), router_w_p, router_b_p, tm)

    return _moe(u2, route, counts, x1, moe_w_gu[0], moe_b_gu[0][:, None, :], moe_w_down[0],
                moe_b_down[0][:, None, :], m_lat[:, 5], final_norm_g.reshape(1, d))
```

```python
import functools
import math

import numpy as np
import jax
import jax.numpy as jnp
from jax import lax
from jax.experimental import pallas as pl
from jax.experimental.pallas import tpu as pltpu

F32 = jnp.float32
BF16 = jnp.bfloat16

EPS = 1e-6
GRID_W = 64
POS_BASE = 10000.0
N_MOD = 6
N_HEADS = 4
DK = 64
DV = 128
QK_W = N_HEADS * 2 * DK
V_W = N_HEADS * DV
GROUP_W = QK_W + 2 * V_W
GATE_W = 128
ML_CONV = 3
GLA_RANK = 16
GLA_NORMALIZER = 16.0
TOP_K = 4
SWIGLU_LIMIT = 7.0
SWIGLU_ALPHA = 1.702
LANES = 128
SUBLANES = 8
BF16_ROWS = 16
ML_CHUNK = 128
GLA_CHUNK = 64
GLA_PREP_ROWS = 256
CONV_ROWS = 128
EXPERT_TILE = 512
ROUTE_BLOCK = 512
RUN_ALIGN = SUBLANES
PERM_ROWS = 256
RUN_SIZES = tuple(s for s in (ROUTE_BLOCK >> i for i in range(ROUTE_BLOCK.bit_length())) if s >= RUN_ALIGN)
VMEM_LIMIT = 56 * 1024 * 1024


def _params(sem):
    return pltpu.CompilerParams(dimension_semantics=sem, vmem_limit_bytes=VMEM_LIMIT)


def _sigmoid(x):
    return 1.0 / (1.0 + jnp.exp(-x))


def _log_sigmoid(x):
    return jnp.minimum(x, 0.0) - jnp.log(1.0 + jnp.exp(-jnp.abs(x)))


def _dot(a, b):
    return jnp.dot(a, b, preferred_element_type=F32)


def _split2(x):
    hi = x.astype(BF16)
    lo = (x - hi.astype(F32)).astype(BF16)
    return hi, lo


def _dot3(a, b):
    a_hi, a_lo = _split2(a)
    b_hi, b_lo = _split2(b)
    return _dot(a_hi, b_hi) + _dot(a_lo, b_hi) + _dot(a_hi, b_lo)


def _dot_nt(a, b):
    return lax.dot_general(a, b, (((1,), (1,)), ((), ())), preferred_element_type=F32)


def _dot_tn(a, b):
    return lax.dot_general(a, b, (((0,), (0,)), ((), ())), preferred_element_type=F32)


def _adaln_body(c_ref, w_ref, b_ref, o_ref):
    c = c_ref[...]
    s = c * _sigmoid(c)
    o_ref[...] = _dot3(s, w_ref[...]) + b_ref[...]


def _adaln(cond, w, b):
    rows, d = cond.shape
    n = w.shape[1]
    tn = 512
    return pl.pallas_call(
        _adaln_body,
        grid=(n // tn,),
        in_specs=[pl.BlockSpec((rows, d), lambda j: (0, 0)),
                  pl.BlockSpec((d, tn), lambda j: (0, j)),
                  pl.BlockSpec((1, tn), lambda j: (0, j))],
        out_specs=pl.BlockSpec((rows, tn), lambda j: (0, j)),
        out_shape=jax.ShapeDtypeStruct((rows, n), F32),
        compiler_params=_params(("arbitrary",)),
        name="adaln",
    )(cond, w, b.reshape(1, n))


def _rms_mod(x, g, scale, shift):
    y = x * lax.rsqrt(jnp.mean(x * x, axis=-1, keepdims=True) + EPS)
    return (y * g) * (1.0 + scale) + shift


def _inproj_body(x_ref, pe_ref, sc_ref, sh_ref, g_ref, wm_ref, wg_ref, zm_ref, zg_ref, gt_ref):
    x = x_ref[0] + pe_ref[...]
    u = _rms_mod(x, g_ref[...], sc_ref[0], sh_ref[0]).astype(BF16)
    z = _dot(u, wm_ref[...])
    zm_ref[0] = z[:, :GROUP_W].astype(BF16)
    zg_ref[0] = z[:, GROUP_W:].astype(BF16)
    gt_ref[0] = _dot(u, wg_ref[...])


def _inproj(x, pe, scale, shift, g, w_main, w_gate, tm):
    bsz, t, d = x.shape
    per_batch = scale.shape[0] == bsz and bsz > 1
    mod_map = (lambda i, b: (b, 0, 0)) if per_batch else (lambda i, b: (0, 0, 0))
    return pl.pallas_call(
        _inproj_body,
        grid=(t // tm, bsz),
        in_specs=[pl.BlockSpec((1, tm, d), lambda i, b: (b, i, 0)),
                  pl.BlockSpec((tm, d), lambda i, b: (i, 0)),
                  pl.BlockSpec((1, 1, d), mod_map),
                  pl.BlockSpec((1, 1, d), mod_map),
                  pl.BlockSpec((1, d), lambda i, b: (0, 0)),
                  pl.BlockSpec((d, 2 * GROUP_W), lambda i, b: (0, 0)),
                  pl.BlockSpec((d, GATE_W), lambda i, b: (0, 0))],
        out_specs=[pl.BlockSpec((1, tm, GROUP_W), lambda i, b: (b, i, 0)),
                   pl.BlockSpec((1, tm, GROUP_W), lambda i, b: (b, i, 0)),
                   pl.BlockSpec((1, tm, GATE_W), lambda i, b: (b, i, 0))],
        out_shape=[jax.ShapeDtypeStruct((bsz, t, GROUP_W), BF16),
                   jax.ShapeDtypeStruct((bsz, t, GROUP_W), BF16),
                   jax.ShapeDtypeStruct((bsz, t, GATE_W), F32)],
        compiler_params=_params(("arbitrary", "arbitrary")),
        name="inproj",
    )(x, pe, scale, shift, g, w_main, w_gate)


def _scan_rows(x, op, ident, reverse, rowi):
    n = x.shape[0]
    k = 1
    while k < n:
        if reverse:
            shifted = jnp.where(rowi < n - k, pltpu.roll(x, n - k, axis=0), ident)
        else:
            shifted = jnp.where(rowi >= k, pltpu.roll(x, k, axis=0), ident)
        x = op(x, shifted)
        k *= 2
    return x


HEAD_DIRS = 2 * N_HEADS


def _mlstm_body(zl_ref, zc_ref, gl_ref, gc_ref, cw_ref, cb_ref, gb_ref, ng_ref, o_ref,
                q_s, kt_s, acc_s, cc_s, bc_s, st_s, mp_s, m_s, en_s, cr_s, wk_s, wc_s, c_s, *, t_lat, t_ctx):
    L = ML_CHUNK
    nc_ctx = t_ctx // L
    nc_lat = t_lat // L
    nc = nc_ctx + nc_lat
    scale = DK ** -0.5

    cw = cw_ref[...]
    cb = cb_ref[...]

    def conv_pass(z_ref, n, dst0):
        nb = n // CONV_ROWS
        row = lax.broadcasted_iota(jnp.int32, (CONV_ROWS, QK_W), 0)

        def body(r, carry):
            r0 = pl.multiple_of(r * CONV_ROWS, CONV_ROWS)
            zc = z_ref[0, pl.ds(r0, CONV_ROWS), 0:QK_W].astype(F32)
            p0 = pl.multiple_of(jnp.maximum(r0 - BF16_ROWS, 0), BF16_ROWS)
            prev = z_ref[0, pl.ds(p0, BF16_ROWS), 0:QK_W].astype(F32)[BF16_ROWS - 1:BF16_ROWS]
            prev = jnp.where(r > 0, prev, 0.0)
            n0 = pl.multiple_of(jnp.minimum(r0 + CONV_ROWS, n - BF16_ROWS), BF16_ROWS)
            nxt = z_ref[0, pl.ds(n0, BF16_ROWS), 0:QK_W].astype(F32)[0:1]
            nxt = jnp.where(r < nb - 1, nxt, 0.0)
            up = jnp.where(row == 0, prev, pltpu.roll(zc, 1, axis=0))
            dn = jnp.where(row == CONV_ROWS - 1, nxt, pltpu.roll(zc, CONV_ROWS - 1, axis=0))
            y = cw[0:1] * up + cw[1:2] * zc + cw[2:3] * dn + cb
            y = y * _sigmoid(y)
            d0 = pl.multiple_of(dst0 + r0, CONV_ROWS)
            q_s[pl.ds(d0, CONV_ROWS), :] = (y * scale).astype(BF16)
            for h in range(N_HEADS):
                kt_s[h, :, pl.ds(d0, CONV_ROWS)] = y[:, h * LANES:(h + 1) * LANES].T[DK:2 * DK].astype(BF16)
            return carry

        lax.fori_loop(0, nb, body, 0)

    conv_pass(zc_ref, t_ctx, 0)
    conv_pass(zl_ref, t_lat, t_ctx)

    rowi = lax.broadcasted_iota(jnp.int32, (L, LANES), 0)
    lane_g = lax.broadcasted_iota(jnp.int32, (L, GATE_W), 1)
    fwd_lane = lane_g < N_HEADS
    gb = gb_ref[...]

    def gate_pass(g_ref, nchunks, g0):
        def body(i, carry):
            r0 = pl.multiple_of(i * L, L)
            gates = g_ref[0, pl.ds(r0, L), :] + gb
            lf = pltpu.roll(_log_sigmoid(gates), GATE_W - HEAD_DIRS, axis=1)
            b = jnp.where(fwd_lane, _scan_rows(lf, jnp.add, 0.0, False, rowi),
                          _scan_rows(lf, jnp.add, 0.0, True, rowi))
            c = gates - b
            cc_s[g0 + i] = c
            bc_s[g0 + i] = b
            st_s[g0 + i, 0:1, :] = jnp.max(c, axis=0, keepdims=True)
            st_s[g0 + i, 1:2, :] = jnp.sum(lf, axis=0, keepdims=True)
            return carry

        lax.fori_loop(0, nchunks, body, 0)

    gate_pass(gc_ref, nc_ctx, 0)
    gate_pass(gl_ref, nc_lat, nc_ctx)

    lane1 = lax.broadcasted_iota(jnp.int32, (1, GATE_W), 1)
    fwd_order = list(range(nc))
    bwd_order = list(range(nc_ctx - 1, -1, -1)) + list(range(nc - 1, nc_ctx - 1, -1))
    mp = {}
    for d, order in enumerate((fwd_order, bwd_order)):
        m = jnp.zeros((1, GATE_W), F32)
        for g in order:
            mp[(d, g)] = m
            m = st_s[g, 1:2, :] + jnp.maximum(m, st_s[g, 0:1, :])
    for g in range(nc):
        mp_s[g, 0:1, :] = jnp.where(lane1 < N_HEADS, mp[(0, g)], mp[(1, g)])

    def ab_body(g, carry):
        c = cc_s[g]
        b = bc_s[g]
        mprev = mp_s[g, 0:1, :]
        mloc = jnp.where(fwd_lane, _scan_rows(c, jnp.maximum, -jnp.inf, False, rowi),
                         _scan_rows(c, jnp.maximum, -jnp.inf, True, rowi))
        m = jnp.maximum(mloc, mprev)
        mlast = jnp.maximum(mprev, st_s[g, 0:1, :])
        m_s[g] = m
        en_s[g] = jnp.exp(-(b + m))
        cr_s[g] = c.T[0:HEAD_DIRS]
        wk_s[g] = jnp.exp((c - mlast).T[0:HEAD_DIRS])
        wc_s[g] = jnp.exp(jnp.broadcast_to(mprev - mlast, (L, GATE_W)).T[0:HEAD_DIRS])
        return carry

    lax.fori_loop(0, nc, ab_body, 0)

    colL = lax.broadcasted_iota(jnp.int32, (L, L), 1)
    rowL = lax.broadcasted_iota(jnp.int32, (L, L), 0)
    masks = (colL <= rowL, colL >= rowL)
    ones_blk = jnp.ones((L, DV), BF16)
    c_s[...] = jnp.zeros(c_s.shape, F32)

    def twice(x):
        return jnp.concatenate([x, x], axis=1)

    def step(g, z_ref, r_src, r_q, d, with_out, r_out):
        for h in range(N_HEADS):
            j = d * N_HEADS + h
            hs = slice(h * DV, (h + 1) * DV)
            kt = kt_s[h, :, pl.ds(r_q, L)]
            v = z_ref[0, pl.ds(r_src, L), QK_W + h * DV:QK_W + (h + 1) * DV]
            vext = jnp.concatenate([v, ones_blk], axis=1)
            state = c_s[j]
            if with_out:
                q = q_s[pl.ds(r_q, L), hs][:, 0:DK]
                mb = jnp.broadcast_to(m_s[g][:, j:j + 1], (L, L))
                w = jnp.where(masks[d], jnp.exp(cr_s[g, j:j + 1, :] - mb), 0.0)
                w_int = jnp.exp(mp_s[g][0:1, j:j + 1] - mb[:, 0:DK])
                p = jnp.concatenate([(_dot(q, kt) * w).astype(BF16), (q.astype(F32) * w_int).astype(BF16)], axis=1)
                nd = _dot(p, jnp.concatenate([vext, state.astype(BF16)], axis=0))
                hh = nd[:, :DV] / jnp.maximum(jnp.abs(nd[:, DV:]), en_s[g][:, j:j + 1])
                acc_s[d, pl.ds(r_out, L), hs] = hh
            ktw = (kt.astype(F32) * wk_s[g, j:j + 1, :]).astype(BF16)
            c_s[j] = twice(wc_s[g, j:j + 1, :]) * state + _dot(ktw, vext)

    def ctx_body(i, carry):
        for d in (0, 1):
            ii = i if d == 0 else nc_ctx - 1 - i
            r0 = pl.multiple_of(ii * L, L)
            step(ii, zc_ref, r0, r0, d, False, None)
        return carry

    def lat_body(i, carry):
        for d in (0, 1):
            ii = i if d == 0 else nc_lat - 1 - i
            r0 = pl.multiple_of(ii * L, L)
            step(nc_ctx + ii, zl_ref, r0, pl.multiple_of(t_ctx + ii * L, L), d, True, r0)
        return carry

    lax.fori_loop(0, nc_ctx, ctx_body, 0)
    lax.fori_loop(0, nc_lat, lat_body, 0)

    def out_body(i, carry):
        r0 = pl.multiple_of(i * L, L)
        for h in range(N_HEADS):
            hs = slice(h * DV, (h + 1) * DV)
            tot = acc_s[0, pl.ds(r0, L), hs] + acc_s[1, pl.ds(r0, L), hs]
            tot = tot * lax.rsqrt(jnp.mean(tot * tot, axis=-1, keepdims=True) + EPS)
            og = zl_ref[0, pl.ds(r0, L), QK_W + V_W + h * DV:QK_W + V_W + (h + 1) * DV].astype(F32)
            o_ref[0, pl.ds(r0, L), hs] = (tot * ng_ref[:, hs] * _sigmoid(og)).astype(BF16)
        return carry

    lax.fori_loop(0, nc_lat, out_body, 0)


def _mlstm(zm_lat, zm_ctx, gt_lat, gt_ctx, conv_w, conv_b, gate_b, norm_g):
    bsz, t_lat, _ = zm_lat.shape
    t_ctx = zm_ctx.shape[1]
    nc = (t_lat + t_ctx) // ML_CHUNK
    body = functools.partial(_mlstm_body, t_lat=t_lat, t_ctx=t_ctx)
    full = lambda b: (0, 0)
    return pl.pallas_call(
        body,
        grid=(bsz,),
        in_specs=[pl.BlockSpec((1, t_lat, GROUP_W), lambda b: (b, 0, 0)),
                  pl.BlockSpec((1, t_ctx, GROUP_W), lambda b: (b, 0, 0)),
                  pl.BlockSpec((1, t_lat, GATE_W), lambda b: (b, 0, 0)),
                  pl.BlockSpec((1, t_ctx, GATE_W), lambda b: (b, 0, 0)),
                  pl.BlockSpec((ML_CONV, QK_W), full),
                  pl.BlockSpec((1, QK_W), full),
                  pl.BlockSpec((1, GATE_W), full),
                  pl.BlockSpec((1, V_W), full)],
        out_specs=pl.BlockSpec((1, t_lat, V_W), lambda b: (b, 0, 0)),
        out_shape=jax.ShapeDtypeStruct((bsz, t_lat, V_W), BF16),
        scratch_shapes=[pltpu.VMEM((t_ctx + t_lat, QK_W), BF16),
                        pltpu.VMEM((N_HEADS, DK, t_ctx + t_lat), BF16),
                        pltpu.VMEM((2, t_lat, V_W), F32),
                        pltpu.VMEM((nc, ML_CHUNK, GATE_W), F32),
                        pltpu.VMEM((nc, ML_CHUNK, GATE_W), F32),
                        pltpu.VMEM((nc, SUBLANES, GATE_W), F32),
                        pltpu.VMEM((nc, SUBLANES, GATE_W), F32),
                        pltpu.VMEM((nc, ML_CHUNK, GATE_W), F32),
                        pltpu.VMEM((nc, ML_CHUNK, GATE_W), F32),
                        pltpu.VMEM((nc, HEAD_DIRS, ML_CHUNK), F32),
                        pltpu.VMEM((nc, HEAD_DIRS, ML_CHUNK), F32),
                        pltpu.VMEM((nc, HEAD_DIRS, GATE_W), F32),
                        pltpu.VMEM((HEAD_DIRS, DK, 2 * DV), F32)],
        compiler_params=_params(("arbitrary",)),
        name="mlstm",
    )(zm_lat, zm_ctx, gt_lat, gt_ctx, conv_w, conv_b, gate_b, norm_g)


def _gla_body(zl_ref, zc_ref, gl_ref, gc_ref, w2_ref, b2_ref, ng_ref, o_ref,
              acc_s, s_s, qt_s, kt_s, ee_s, *, t_lat, t_ctx):
    L = GLA_CHUNK
    nc_ctx = t_ctx // L
    nc_lat = t_lat // L
    scale = DK ** -0.5
    half = N_HEADS * DK

    rowi = lax.broadcasted_iota(jnp.int32, (L, L), 0)
    coli = lax.broadcasted_iota(jnp.int32, (L, L), 1)
    masks = (coli <= rowi, coli >= rowi)
    tri_b = (masks[0].astype(BF16), masks[1].astype(BF16))
    low = lax.broadcasted_iota(jnp.int32, (L, LANES), 1) < DK
    w2_hi, w2_lo = _split2(w2_ref[...])
    bias2 = b2_ref[...]

    def prep(z_ref, g_ref, n_rows, c0, row0):
        grp = math.gcd(n_rows, GLA_PREP_ROWS)

        def body(gi, carry):
            rg = pl.multiple_of(gi * grp, grp)
            g_hi, g_lo = _split2(g_ref[0, pl.ds(rg, grp), :])
            pre = _dot(g_hi, w2_hi) + _dot(g_lo, w2_hi) + _dot(g_hi, w2_lo) + bias2
            la_hi, la_lo = _split2(_log_sigmoid(pre) * (1.0 / GLA_NORMALIZER))
            for cc in range(grp // L):
                rs = slice(cc * L, (cc + 1) * L)
                r0 = pl.multiple_of(rg + cc * L, L)
                rq = pl.multiple_of(row0 + rg + cc * L, L)
                c = c0 + gi * (grp // L) + cc
                for d in (0, 1):
                    ds_ = slice(d * half, (d + 1) * half)
                    b_all = _dot(tri_b[d], la_hi[rs, ds_]) + _dot(tri_b[d], la_lo[rs, ds_])
                    for hp in range(N_HEADS // 2):
                        pair = b_all[:, hp * LANES:(hp + 1) * LANES]
                        swapped = pltpu.roll(pair, DK, axis=1)
                        for hh in range(2):
                            h = 2 * hp + hh
                            hs = slice(h * LANES, (h + 1) * LANES)
                            b2 = jnp.where(low, pair, swapped) if hh == 0 else jnp.where(low, swapped, pair)
                            b_end = b2[L - 1:L] if d == 0 else b2[0:1]
                            qk = z_ref[0, pl.ds(r0, L), hs].astype(F32)
                            qkt = qk * jnp.exp(jnp.where(low, b2, -b2))
                            qt_s[d, pl.ds(rq, L), hs] = (qkt * scale).astype(BF16)
                            kt_s[d, pl.ds(rq, L), hs] = jnp.where(low, pltpu.roll(qkt, DK, axis=1), 0.0).astype(BF16)
                            ee_s[d, c, 0:1, hs] = jnp.exp(b_end)
            return carry

        lax.fori_loop(0, n_rows // grp, body, 0)

    prep(zc_ref, gc_ref, t_ctx, 0, 0)
    prep(zl_ref, gl_ref, t_lat, nc_ctx, t_ctx)

    s_s[...] = jnp.zeros(s_s.shape, F32)

    def step(z_ref, r0, rq, c, d, with_out):
        for h in range(N_HEADS):
            j = d * N_HEADS + h
            hs = slice(h * LANES, (h + 1) * LANES)
            qt = qt_s[d, pl.ds(rq, L), hs]
            kt = kt_s[d, pl.ds(rq, L), hs]
            v = z_ref[0, pl.ds(r0, L), QK_W + h * DV:QK_W + (h + 1) * DV]
            state = s_s[j]
            if with_out:
                a = jnp.where(masks[d], _dot_nt(qt, kt), 0.0).astype(BF16)
                acc_s[d, pl.ds(r0, L), hs] = _dot(a, v) + _dot_nt(qt, state.astype(BF16))
            e_end = ee_s[d, c, 0:1, hs]
            s_s[j] = state * e_end + _dot_tn(v, (kt.astype(F32) * e_end).astype(BF16))

    def loop(z_ref, n, c0, row0, with_out):
        def body(i, carry):
            for d in (0, 1):
                ii = i if d == 0 else n - 1 - i
                r0 = pl.multiple_of(ii * L, L)
                step(z_ref, r0, pl.multiple_of(row0 + ii * L, L), c0 + ii, d, with_out)
            return carry
        lax.fori_loop(0, n, body, 0, unroll=4)

    loop(zc_ref, nc_ctx, 0, 0, False)
    loop(zl_ref, nc_lat, nc_ctx, t_ctx, True)

    def out_body(i, carry):
        r0 = pl.multiple_of(i * L, L)
        for h in range(N_HEADS):
            hs = slice(h * LANES, (h + 1) * LANES)
            tot = acc_s[0, pl.ds(r0, L), hs] + acc_s[1, pl.ds(r0, L), hs]
            tot = tot * lax.rsqrt(jnp.mean(tot * tot, axis=-1, keepdims=True) + EPS)
            gg = zl_ref[0, pl.ds(r0, L), QK_W + V_W + h * DV:QK_W + V_W + (h + 1) * DV].astype(F32)
            o_ref[0, pl.ds(r0, L), hs] = (tot * ng_ref[:, hs] * (gg * _sigmoid(gg))).astype(BF16)
        return carry

    lax.fori_loop(0, nc_lat, out_body, 0)


def _gla(zg_lat, zg_ctx, gt_lat, gt_ctx, w2_ext, b2_ext, norm_g):
    bsz, t_lat, _ = zg_lat.shape
    t_ctx = zg_ctx.shape[1]
    body = functools.partial(_gla_body, t_lat=t_lat, t_ctx=t_ctx)
    full = lambda b: (0, 0)
    return pl.pallas_call(
        body,
        grid=(bsz,),
        in_specs=[pl.BlockSpec((1, t_lat, GROUP_W), lambda b: (b, 0, 0)),
                  pl.BlockSpec((1, t_ctx, GROUP_W), lambda b: (b, 0, 0)),
                  pl.BlockSpec((1, t_lat, GATE_W), lambda b: (b, 0, 0)),
                  pl.BlockSpec((1, t_ctx, GATE_W), lambda b: (b, 0, 0)),
                  pl.BlockSpec((GATE_W, 2 * N_HEADS * DK), full),
                  pl.BlockSpec((1, 2 * N_HEADS * DK), full),
                  pl.BlockSpec((1, V_W), full)],
        out_specs=pl.BlockSpec((1, t_lat, V_W), lambda b: (b, 0, 0)),
        out_shape=jax.ShapeDtypeStruct((bsz, t_lat, V_W), BF16),
        scratch_shapes=[pltpu.VMEM((2, t_lat, V_W), F32),
                        pltpu.VMEM((HEAD_DIRS, DV, LANES), F32),
                        pltpu.VMEM((2, t_ctx + t_lat, QK_W), BF16),
                        pltpu.VMEM((2, t_ctx + t_lat, QK_W), BF16),
                        pltpu.VMEM((2, (t_ctx + t_lat) // GLA_CHUNK, SUBLANES, QK_W), F32)],
        compiler_params=_params(("arbitrary",)),
        name="gla",
    )(zg_lat, zg_ctx, gt_lat, gt_ctx, w2_ext, b2_ext, norm_g)


def _outproj_body(x_ref, pe_ref, a_ref, g_ref, wo_ref, g1_ref, sc_ref, sh_ref, n2_ref, rw_ref, rb_ref,
                  x1_ref, u2_ref, route_ref, cnt_ref):
    x = x_ref[0] + pe_ref[...]
    y = _dot(a_ref[0], wo_ref[0:V_W, :]) + _dot(g_ref[0], wo_ref[V_W:2 * V_W, :])
    x1 = x + g1_ref[0] * y
    x1_ref[0] = x1
    u2 = _rms_mod(x1, n2_ref[...], sc_ref[0], sh_ref[0])
    u_hi, u_lo = _split2(u2)
    u2_ref[0] = u_hi
    w_hi, w_lo = _split2(rw_ref[...])
    hh = _dot(u_hi, jnp.concatenate([w_hi, w_lo], axis=1))
    logits = hh[:, :LANES] + _dot(u_lo, w_hi) + hh[:, LANES:] + rb_ref[...]
    lane = lax.broadcasted_iota(jnp.int32, logits.shape, 1).astype(F32)
    ids = []
    vals = []
    for _ in range(TOP_K):
        m = jnp.max(logits, axis=1, keepdims=True)
        idx = jnp.min(jnp.where(logits == m, lane, float(LANES)), axis=1, keepdims=True)
        ids.append(idx)
        vals.append(m)
        logits = jnp.where(lane == idx, -jnp.inf, logits)
    exps = [jnp.exp(v - vals[0]) for v in vals]
    denom = exps[0]
    for e in exps[1:]:
        denom = denom + e
    route = jnp.zeros(logits.shape, F32)
    for k in range(TOP_K):
        route = jnp.where(lane == float(k), ids[k], route)
        route = jnp.where(lane == float(TOP_K + k), exps[k] / denom, route)
    route_ref[0] = route

    @pl.when((pl.program_id(0) == 0) & (pl.program_id(1) == 0))
    def _():
        cnt_ref[...] = jnp.zeros(cnt_ref.shape, F32)

    cnt_ref[...] += _block_picks(route)[3]


def _outproj(x, pe, a, g, w_out, gate1, scale2, shift2, norm2_g, router_w, router_b, tm):
    bsz, t, d = x.shape
    tok = lambda i, b: (b, i, 0)
    mod = lambda i, b: (b, 0, 0)
    full = lambda i, b: (0, 0)
    return pl.pallas_call(
        _outproj_body,
        grid=(t // tm, bsz),
        in_specs=[pl.BlockSpec((1, tm, d), tok),
                  pl.BlockSpec((tm, d), lambda i, b: (i, 0)),
                  pl.BlockSpec((1, tm, V_W), tok),
                  pl.BlockSpec((1, tm, V_W), tok),
                  pl.BlockSpec((2 * V_W, d), full),
                  pl.BlockSpec((1, 1, d), mod),
                  pl.BlockSpec((1, 1, d), mod),
                  pl.BlockSpec((1, 1, d), mod),
                  pl.BlockSpec((1, d), full),
                  pl.BlockSpec((d, LANES), full),
                  pl.BlockSpec((1, LANES), full)],
        out_specs=[pl.BlockSpec((1, tm, d), tok),
                   pl.BlockSpec((1, tm, d), tok),
                   pl.BlockSpec((1, tm, LANES), tok),
                   pl.BlockSpec((1, LANES), full)],
        out_shape=[jax.ShapeDtypeStruct((bsz, t, d), F32),
                   jax.ShapeDtypeStruct((bsz, t, d), BF16),
                   jax.ShapeDtypeStruct((bsz, t, LANES), F32),
                   jax.ShapeDtypeStruct((1, LANES), F32)],
        compiler_params=_params(("arbitrary", "arbitrary")),
        name="outproj_router",
    )(x, pe, a, g, w_out, gate1, scale2, shift2, norm2_g, router_w, router_b)


def _lane_excl_cumsum(x):
    x8 = jnp.broadcast_to(x, (SUBLANES, LANES))
    lane8 = lax.broadcasted_iota(jnp.int32, (SUBLANES, LANES), 1)
    inc = x8
    s = 1
    while s < LANES:
        inc = inc + jnp.where(lane8 >= s, pltpu.roll(inc, s, axis=1), 0.0)
        s *= 2
    return (inc - x8)[0:1]


def _block_picks(route):
    lane = lax.broadcasted_iota(jnp.int32, route.shape, 1).astype(F32)
    onehots = [lane == route[:, k:k + 1] for k in range(TOP_K)]
    oh = onehots[0].astype(F32)
    for o in onehots[1:]:
        oh = oh + o.astype(F32)
    run_len = jnp.ceil(jnp.sum(oh, axis=0, keepdims=True) * (1.0 / RUN_ALIGN)) * float(RUN_ALIGN)
    return lane, onehots, oh, run_len


def _route_pos_body(route_ref, cnt_ref, lic_ref, lit_ref, tab_ref, off_s, *, tile):
    tb = route_ref.shape[0]
    lane, onehots, oh, run_len = _block_picks(route_ref[...])

    @pl.when(pl.program_id(0) == 0)
    def _():
        cpad = jnp.floor((cnt_ref[...] + float(tile - 1)) * (1.0 / tile)) * float(tile)
        off_s[...] = _lane_excl_cumsum(cpad)

    r = lax.broadcasted_iota(jnp.int32, (tb, tb), 0)
    c = lax.broadcasted_iota(jnp.int32, (tb, tb), 1)
    before = _dot((c < r).astype(BF16), oh.astype(BF16))
    lstart = _lane_excl_cumsum(run_len)
    local = lstart + before
    li = jnp.zeros((tb, LANES), F32)
    for k in range(TOP_K):
        lk = jnp.sum(jnp.where(onehots[k], local, 0.0), axis=1, keepdims=True)
        li = jnp.where(lane == float(k), lk, li)
    lic_ref[...] = li
    lit_ref[0] = li.T[0:SUBLANES]
    row8 = lax.broadcasted_iota(jnp.int32, (SUBLANES, LANES), 0)
    total = jnp.sum(run_len, axis=1, keepdims=True)
    tab = jnp.where(row8 == 0, off_s[...],
                    jnp.where(row8 == 1, run_len, jnp.where(row8 == 2, lstart, jnp.where(row8 == 3, total, 0.0))))
    tab_ref[0] = tab.astype(jnp.int32)
    off_s[...] += run_len


def _route_pos(route, counts, tile):
    n = route.shape[0]
    tb = ROUTE_BLOCK
    nb = n // tb
    return pl.pallas_call(
        functools.partial(_route_pos_body, tile=tile),
        grid=(nb,),
        in_specs=[pl.BlockSpec((tb, LANES), lambda j: (j, 0)),
                  pl.BlockSpec((1, LANES), lambda j: (0, 0))],
        out_specs=[pl.BlockSpec((tb, LANES), lambda j: (j, 0)),
                   pl.BlockSpec((1, SUBLANES, tb), lambda j: (j, 0, 0)),
                   pl.BlockSpec((1, SUBLANES, LANES), lambda j: (j, 0, 0))],
        out_shape=[jax.ShapeDtypeStruct((n, LANES), F32),
                   jax.ShapeDtypeStruct((nb, SUBLANES, tb), F32),
                   jax.ShapeDtypeStruct((nb, SUBLANES, LANES), jnp.int32)],
        scratch_shapes=[pltpu.VMEM((1, LANES), F32)],
        compiler_params=_params(("arbitrary",)),
        name="route_pos",
    )(route, counts)


def _issue_runs(tab_ref, n_experts, make_copy):
    def per_expert(e, carry):
        g = tab_ref[0, 0, e]
        cnt = tab_ref[0, 1, e]
        l = tab_ref[0, 2, e]
        for size in RUN_SIZES:
            part = cnt & size

            @pl.when(part != 0)
            def _():
                make_copy(pl.multiple_of(l, RUN_ALIGN), pl.multiple_of(g, RUN_ALIGN), size).start()

            l = l + part
            g = g + part
        return carry

    lax.fori_loop(0, n_experts, per_expert, 0)


def _dispatch_body(zrow_ref, zflag_ref, nu_ref, tab_ref, lit_ref, u_ref, xs_ref, buf, zero_s, tot_s, sem, zsem,
                   *, n_experts, tile):
    i = pl.program_id(0)
    nb = pl.num_programs(0)
    tb = u_ref.shape[0]
    rows = buf.shape[1]
    n_tiles = xs_ref.shape[0] // tile
    slot = i % 2

    @pl.when(i == 0)
    def _():
        zero_s[...] = jnp.zeros(zero_s.shape, F32)

        def zero_tile(r0):
            cp = pltpu.make_async_copy(zero_s, xs_ref.at[pl.ds(pl.multiple_of(r0, tile), tile)], zsem)
            cp.start()
            cp.wait()

        for e in range(n_experts):
            @pl.when(zflag_ref[e] == 1)
            def _():
                zero_tile(zrow_ref[e])

        def tail(t, carry):
            zero_tile(t * tile)
            return carry

        lax.fori_loop(nu_ref[0], n_tiles, tail, 0)

    def wait_slot(s):
        nrows = pl.multiple_of(tot_s[s], RUN_ALIGN)
        pltpu.make_async_copy(buf.at[s, pl.ds(0, nrows)], xs_ref.at[pl.ds(0, nrows)], sem.at[s]).wait()

    @pl.when(i >= 2)
    def _():
        wait_slot(slot)

    tot_s[slot] = tab_ref[0, 3, 0]

    lit = lit_ref[0]
    ub = u_ref[...]
    ri0 = lax.broadcasted_iota(jnp.int32, (PERM_ROWS, tb), 0).astype(F32)
    for c in range(rows // PERM_ROWS):
        ri = ri0 + float(c * PERM_ROWS)
        hit = jnp.zeros((PERM_ROWS, tb), F32)
        for k in range(TOP_K):
            hit = jnp.where(ri == lit[k:k + 1], 1.0, hit)
        buf[slot, c * PERM_ROWS:(c + 1) * PERM_ROWS] = _dot(hit.astype(BF16), ub)
    _issue_runs(tab_ref, n_experts,
                lambda l, g, size: pltpu.make_async_copy(buf.at[slot, pl.ds(l, size)], xs_ref.at[pl.ds(g, size)],
                                                         sem.at[slot]))

    @pl.when(i == nb - 1)
    def _():
        wait_slot(slot)

    @pl.when((i == nb - 1) & (nb > 1))
    def _():
        wait_slot(1 - slot)


def _dispatch(u2, tab, lit, zrow, zflag, n_used, n_rows, tile):
    n, d = u2.shape
    tb = ROUTE_BLOCK
    nb = n // tb
    n_experts = zrow.shape[0]
    grid_spec = pltpu.PrefetchScalarGridSpec(
        num_scalar_prefetch=3,
        grid=(nb,),
        in_specs=[pl.BlockSpec((1, SUBLANES, LANES), lambda i, zr, zf, nu: (i, 0, 0), memory_space=pltpu.SMEM),
                  pl.BlockSpec((1, SUBLANES, tb), lambda i, zr, zf, nu: (i, 0, 0)),
                  pl.BlockSpec((tb, d), lambda i, zr, zf, nu: (i, 0))],
        out_specs=pl.BlockSpec(memory_space=pl.ANY),
        scratch_shapes=[pltpu.VMEM((2, tb * TOP_K + n_experts * RUN_ALIGN, d), F32), pltpu.VMEM((tile, d), F32),
                        pltpu.SMEM((2,), jnp.int32), pltpu.SemaphoreType.DMA((2,)), pltpu.SemaphoreType.DMA(())],
    )
    return pl.pallas_call(
        functools.partial(_dispatch_body, n_experts=n_experts, tile=tile),
        grid_spec=grid_spec,
        out_shape=jax.ShapeDtypeStruct((n_rows, d), F32),
        compiler_params=_params(("arbitrary",)),
        name="dispatch",
    )(zrow, zflag, n_used, tab, lit, u2)


def _experts_body(te_ref, nu_ref, x_ref, wgu_ref, bgu_ref, wd_ref, bd_ref, y_ref, wgu_s, wd_s, *, d_expert):
    t = pl.program_id(0)

    @pl.when((t == 0) | (te_ref[t] != te_ref[jnp.maximum(t - 1, 0)]))
    def _():
        wgu_s[...] = wgu_ref[0].astype(BF16)
        wd_s[...] = wd_ref[0].astype(BF16)

    @pl.when(t < nu_ref[0])
    def _():
        gu = _dot(x_ref[...].astype(BF16), wgu_s[...]) + bgu_ref[0]
        gate = jnp.minimum(gu[:, :d_expert], SWIGLU_LIMIT)
        up = jnp.clip(gu[:, d_expert:], -SWIGLU_LIMIT, SWIGLU_LIMIT)
        act = ((up + 1.0) * (gate * _sigmoid(SWIGLU_ALPHA * gate))).astype(BF16)
        y_ref[...] = _dot(act, wd_s[...]) + bd_ref[0]

    @pl.when(t >= nu_ref[0])
    def _():
        y_ref[...] = jnp.zeros(y_ref.shape, F32)


def _experts(xs, tile_expert, n_used, w_gu, b_gu, w_down, b_down, tile):
    n_rows, d = xs.shape
    n_experts, _, two_de = w_gu.shape
    d_expert = two_de // 2
    grid_spec = pltpu.PrefetchScalarGridSpec(
        num_scalar_prefetch=2,
        grid=(n_rows // tile,),
        in_specs=[pl.BlockSpec((tile, d), lambda t, te, nu: (jnp.minimum(t, nu[0] - 1), 0)),
                  pl.BlockSpec((1, d, two_de), lambda t, te, nu: (te[t], 0, 0)),
                  pl.BlockSpec((1, 1, two_de), lambda t, te, nu: (te[t], 0, 0)),
                  pl.BlockSpec((1, d_expert, d), lambda t, te, nu: (te[t], 0, 0)),
                  pl.BlockSpec((1, 1, d), lambda t, te, nu: (te[t], 0, 0))],
        out_specs=pl.BlockSpec((tile, d), lambda t, te, nu: (t, 0)),
        scratch_shapes=[pltpu.VMEM((d, two_de), BF16), pltpu.VMEM((d_expert, d), BF16)],
    )
    return pl.pallas_call(
        functools.partial(_experts_body, d_expert=d_expert),
        grid_spec=grid_spec,
        out_shape=jax.ShapeDtypeStruct((n_rows, d), F32),
        compiler_params=_params(("arbitrary",)),
        name="experts",
    )(tile_expert, n_used, xs, w_gu, b_gu, w_down, b_down)


def _combine_body(tab_ref, tabn_ref, lic_ref, route_ref, x1_ref, g5_ref, fg_ref, ys_ref, o_ref, buf, sem,
                  *, n_experts):
    i = pl.program_id(0)
    n = pl.num_programs(0)
    tb = x1_ref.shape[0]
    rows = buf.shape[1]
    slot = i % 2

    def issue(t_ref, s):
        _issue_runs(t_ref, n_experts,
                    lambda l, g, size: pltpu.make_async_copy(ys_ref.at[pl.ds(g, size)], buf.at[s, pl.ds(l, size)],
                                                             sem.at[s]))

    @pl.when(i == 0)
    def _():
        buf[...] = jnp.zeros(buf.shape, F32)
        issue(tab_ref, 0)

    @pl.when(i + 1 < n)
    def _():
        issue(tabn_ref, 1 - slot)

    nrows = pl.multiple_of(tab_ref[0, 3, 0], RUN_ALIGN)
    pltpu.make_async_copy(ys_ref.at[pl.ds(0, nrows)], buf.at[slot, pl.ds(0, nrows)], sem.at[slot]).wait()

    lane = lax.broadcasted_iota(jnp.int32, (tb, LANES), 1)
    li_hi, li_lo = _split2(lic_ref[...])
    a = jnp.where(lane < TOP_K, li_hi.astype(F32),
                  jnp.where(lane < 2 * TOP_K, pltpu.roll(li_lo.astype(F32), TOP_K, axis=1),
                            jnp.where(lane < 3 * TOP_K, pltpu.roll(route_ref[...], TOP_K, axis=1), 0.0))).astype(BF16)
    rsel = lax.broadcasted_iota(jnp.int32, (LANES, LANES), 0)
    lane_f = lane.astype(F32)
    idx, gate = [], []
    for k in range(TOP_K):
        sel = jnp.concatenate([((rsel == k) | (rsel == TOP_K + k)).astype(BF16),
                               (rsel == 2 * TOP_K + k).astype(BF16)], axis=1)
        bk = _dot(a, sel)
        idx.append(bk[:, :LANES])
        gate.append(bk[:, LANES:])
    moe = jnp.zeros(x1_ref.shape, F32)
    for c0 in range(0, rows, PERM_ROWS):
        blocks = []
        for c in range(c0, c0 + PERM_ROWS, LANES):
            col = lane_f + float(c)
            w = jnp.zeros((tb, LANES), F32)
            for k in range(TOP_K):
                w = jnp.where(idx[k] == col, gate[k], w)
            blocks.append(w.astype(BF16))
        moe = moe + _dot(jnp.concatenate(blocks, axis=1), buf[slot, c0:c0 + PERM_ROWS].astype(BF16))
    xo = x1_ref[...] + g5_ref[0] * moe
    o_ref[...] = xo * lax.rsqrt(jnp.mean(xo * xo, axis=-1, keepdims=True) + EPS) * fg_ref[...]


def _combine(ys, tab, lic, route, x1, gate5, final_g, t_lat, n_experts):
    n, d = x1.shape
    tb = ROUTE_BLOCK
    nb = n // tb
    per_b = t_lat // tb
    return pl.pallas_call(
        functools.partial(_combine_body, n_experts=n_experts),
        grid=(nb,),
        in_specs=[pl.BlockSpec((1, SUBLANES, LANES), lambda i: (i, 0, 0), memory_space=pltpu.SMEM),
                  pl.BlockSpec((1, SUBLANES, LANES), lambda i: (jnp.minimum(i + 1, nb - 1), 0, 0),
                               memory_space=pltpu.SMEM),
                  pl.BlockSpec((tb, LANES), lambda i: (i, 0)),
                  pl.BlockSpec((tb, LANES), lambda i: (i, 0)),
                  pl.BlockSpec((tb, d), lambda i: (i, 0)),
                  pl.BlockSpec((1, 1, d), lambda i: (i // per_b, 0, 0)),
                  pl.BlockSpec((1, d), lambda i: (0, 0)),
                  pl.BlockSpec(memory_space=pl.ANY)],
        out_specs=pl.BlockSpec((tb, d), lambda i: (i, 0)),
        out_shape=jax.ShapeDtypeStruct((n, d), F32),
        scratch_shapes=[pltpu.VMEM((2, tb * TOP_K + n_experts * RUN_ALIGN, d), F32), pltpu.SemaphoreType.DMA((2,))],
        compiler_params=_params(("arbitrary",)),
        name="combine",
    )(tab, tab, lic, route, x1, gate5, final_g, ys)


def _moe(u2, route, counts, x1, w_gu, b_gu, w_down, b_down, gate5, final_g):
    bsz, t_lat, d = x1.shape
    n = bsz * t_lat
    n_experts = w_gu.shape[0]
    tile = EXPERT_TILE
    route2 = route.reshape(n, LANES)
    lic, lit, tab = _route_pos(route2, counts, tile)
    cnt = counts[0, :n_experts].astype(jnp.int32)
    cpad = ((cnt + tile - 1) // tile) * tile
    ends = jnp.cumsum(cpad)
    n_tiles = -(-(n * TOP_K + (n // ROUTE_BLOCK) * n_experts * RUN_ALIGN) // tile) + n_experts
    n_rows = n_tiles * tile
    tile_expert = jnp.minimum(
        jnp.sum(jnp.arange(n_tiles, dtype=jnp.int32)[:, None] * tile >= ends[None, :], axis=1), n_experts - 1
    ).astype(jnp.int32)
    n_used = (ends[-1:] // tile).astype(jnp.int32)
    zrow = jnp.maximum(ends - tile, 0).astype(jnp.int32)
    zflag = (cnt % tile != 0).astype(jnp.int32)
    xs = _dispatch(u2.reshape(n, d), tab, lit, zrow, zflag, n_used, n_rows, tile)
    ys = _experts(xs, tile_expert, n_used, w_gu, b_gu, w_down, b_down, tile)
    out = _combine(ys, tab, lic, route2, x1.reshape(n, d), gate5, final_g, t_lat, n_experts)
    return out.reshape(bsz, t_lat, d)


def _grid_sincos(rows, d):
    nf = d // 4
    omega = 1.0 / (POS_BASE ** (jnp.arange(nf, dtype=F32) / nf))
    r = jnp.broadcast_to(jnp.arange(rows, dtype=F32)[:, None, None] * omega, (rows, GRID_W, nf))
    cl = jnp.broadcast_to(jnp.arange(GRID_W, dtype=F32)[None, :, None] * omega, (rows, GRID_W, nf))
    pe = jnp.concatenate([jnp.sin(r), jnp.cos(r), jnp.sin(cl), jnp.cos(cl)], axis=-1)
    return pe.reshape(rows * GRID_W, d)


def _pack_qk(wq, wk):
    lead = wq.shape[:-1]
    q = wq.reshape(lead + (N_HEADS, DK))
    k = wk.reshape(lead + (N_HEADS, DK))
    return jnp.concatenate([q, k], axis=-1).reshape(lead + (QK_W,))


def _row_tile(t):
    for tm in (512, 256, 128):
        if t % tm == 0:
            return tm
    raise ValueError(f"sequence length {t} must be a multiple of 128")


def kernel(x, c, ctx, c_ctx, ada_w, ada_b, norm1_g, w_in, ml_conv_w, ml_conv_b, ml_gate_b, ml_norm_g,
           gla_gate_w2, gla_gate_b, gla_norm_g, w_out, norm2_g, router_w, router_b, moe_w_gu, moe_b_gu,
           moe_w_down, moe_b_down, final_norm_g):
    bsz, t_lat, d = x.shape
    t_ctx = ctx.shape[1]
    assert ada_w.shape[0] == 1, "single-layer block"
    assert t_lat % ML_CHUNK == 0 and t_ctx % ML_CHUNK == 0 and t_lat % GRID_W == 0 and t_lat % ROUTE_BLOCK == 0
    n_experts = router_w.shape[-1]
    assert n_experts <= LANES

    ml_qk, ml_v = N_HEADS * DK, N_HEADS * DV
    sizes = (ml_qk, ml_qk, ml_v, ml_v, 4 * N_HEADS, ml_qk, ml_qk, ml_v, ml_v, 2 * GLA_RANK)
    offs = [int(o) for o in np.cumsum(sizes)[:-1]]
    w_mq, w_mk, w_mv, w_mo, w_mg, w_gq, w_gk, w_gv, w_gg, w_glr = jnp.split(w_in[0], offs, axis=-1)
    w_main = jnp.concatenate([_pack_qk(w_mq, w_mk), w_mv, w_mo, _pack_qk(w_gq, w_gk), w_gv, w_gg],
                             axis=-1).astype(BF16)
    gate_perm = np.concatenate([np.arange(N_HEADS) + g * N_HEADS for g in (0, 2, 1, 3)])
    w_gate = jnp.concatenate([w_mg[:, gate_perm], w_glr,
                              jnp.zeros((d, GATE_W - 4 * N_HEADS - 2 * GLA_RANK), F32)], axis=-1).astype(BF16)
    gate_b = jnp.concatenate([ml_gate_b[0].reshape(-1)[gate_perm],
                              jnp.zeros((GATE_W - 4 * N_HEADS,), F32)]).reshape(1, GATE_W)
    conv_w = _pack_qk(ml_conv_w[0][:, :ml_qk], ml_conv_w[0][:, ml_qk:])
    conv_b = _pack_qk(ml_conv_b[0][:ml_qk], ml_conv_b[0][ml_qk:]).reshape(1, QK_W)
    gla_qk = N_HEADS * DK
    w2_ext = jnp.zeros((GATE_W, 2 * gla_qk), F32)
    for dd in range(2):
        r0 = 4 * N_HEADS + dd * GLA_RANK
        w2_ext = w2_ext.at[r0:r0 + GLA_RANK, dd * gla_qk:(dd + 1) * gla_qk].set(gla_gate_w2[0, dd])
    b2_ext = gla_gate_b[0].reshape(1, 2 * gla_qk)
    router_w_p = jnp.concatenate([router_w[0], jnp.zeros((d, LANES - n_experts), F32)], axis=-1)
    router_b_p = jnp.concatenate([router_b[0], jnp.full((LANES - n_experts,), -jnp.inf, F32)]).reshape(1, LANES)

    cond = jnp.concatenate([c, c_ctx[None, :], jnp.zeros(((-bsz - 1) % SUBLANES, d), F32)], axis=0)
    mod = _adaln(cond, ada_w[0], ada_b[0]).reshape(cond.shape[0], N_MOD, 1, d)
    m_lat = mod[:bsz]
    m_ctx = mod[bsz:bsz + 1]

    pe = _grid_sincos(t_lat // GRID_W, d)
    g1 = norm1_g[0].reshape(1, d)
    tm = _row_tile(t_lat)
    zm_lat, zg_lat, gt_lat = _inproj(x, pe, m_lat[:, 1], m_lat[:, 0], g1, w_main, w_gate, tm)
    zm_ctx, zg_ctx, gt_ctx = _inproj(ctx, jnp.zeros((t_ctx, d), F32), m_ctx[:, 1], m_ctx[:, 0], g1,
                                     w_main, w_gate, _row_tile(t_ctx))

    a = _mlstm(zm_lat, zm_ctx, gt_lat, gt_ctx, conv_w, conv_b, gate_b, ml_norm_g[0].reshape(1, V_W))
    g = _gla(zg_lat, zg_ctx, gt_lat, gt_ctx, w2_ext, b2_ext, gla_norm_g[0].reshape(1, V_W))

    assert tm == ROUTE_BLOCK
    x1, u2, route, counts = _outproj(x, pe, a, g, w_out[0].astype(BF16), m_lat[:, 2], m_lat[:, 4], m_lat[:, 3],
                                     norm2_g[0].reshape(1, d), router_w_p, router_b_p, tm)

    return _moe(u2, route, counts, x1, moe_w_gu[0], moe_b_gu[0][:, None, :], moe_w_down[0],
                moe_b_down[0][:, None, :], m_lat[:, 5], final_norm_g.reshape(1, d))
```

```python
import functools
import math

import numpy as np
import jax
import jax.numpy as jnp
from jax import lax
from jax.experimental import pallas as pl
from jax.experimental.pallas import tpu as pltpu

F32 = jnp.float32
BF16 = jnp.bfloat16

EPS = 1e-6
GRID_W = 64
POS_BASE = 10000.0
N_MOD = 6
N_HEADS = 4
DK = 64
DV = 128
QK_W = N_HEADS * 2 * DK
V_W = N_HEADS * DV
GROUP_W = QK_W + 2 * V_W
GATE_W = 128
ML_CONV = 3
GLA_RANK = 16
GLA_NORMALIZER = 16.0
TOP_K = 4
SWIGLU_LIMIT = 7.0
SWIGLU_ALPHA = 1.702
LANES = 128
SUBLANES = 8
BF16_ROWS = 16
ML_CHUNK = 128
GLA_CHUNK = 64
GLA_PREP_ROWS = 256
CONV_ROWS = 128
EXPERT_TILE = 512
ROUTE_BLOCK = 512
RUN_ALIGN = SUBLANES
PERM_ROWS = 256
RUN_SIZES = tuple(s for s in (ROUTE_BLOCK >> i for i in range(ROUTE_BLOCK.bit_length())) if s >= RUN_ALIGN)
VMEM_LIMIT = 56 * 1024 * 1024


def _params(sem):
    return pltpu.CompilerParams(dimension_semantics=sem, vmem_limit_bytes=VMEM_LIMIT)


def _sigmoid(x):
    return 1.0 / (1.0 + jnp.exp(-x))


def _log_sigmoid(x):
    return jnp.minimum(x, 0.0) - jnp.log(1.0 + jnp.exp(-jnp.abs(x)))


def _dot(a, b):
    return jnp.dot(a, b, preferred_element_type=F32)


def _split2(x):
    hi = x.astype(BF16)
    lo = (x - hi.astype(F32)).astype(BF16)
    return hi, lo


def _dot3(a, b):
    a_hi, a_lo = _split2(a)
    b_hi, b_lo = _split2(b)
    return _dot(a_hi, b_hi) + _dot(a_lo, b_hi) + _dot(a_hi, b_lo)


def _dot_nt(a, b):
    return lax.dot_general(a, b, (((1,), (1,)), ((), ())), preferred_element_type=F32)


def _dot_tn(a, b):
    return lax.dot_general(a, b, (((0,), (0,)), ((), ())), preferred_element_type=F32)


def _adaln_body(c_ref, w_ref, b_ref, o_ref):
    c = c_ref[...]
    s = c * _sigmoid(c)
    o_ref[...] = _dot3(s, w_ref[...]) + b_ref[...]


def _adaln(cond, w, b):
    rows, d = cond.shape
    n = w.shape[1]
    tn = 512
    return pl.pallas_call(
        _adaln_body,
        grid=(n // tn,),
        in_specs=[pl.BlockSpec((rows, d), lambda j: (0, 0)),
                  pl.BlockSpec((d, tn), lambda j: (0, j)),
                  pl.BlockSpec((1, tn), lambda j: (0, j))],
        out_specs=pl.BlockSpec((rows, tn), lambda j: (0, j)),
        out_shape=jax.ShapeDtypeStruct((rows, n), F32),
        compiler_params=_params(("arbitrary",)),
        name="adaln",
    )(cond, w, b.reshape(1, n))


def _rms_mod(x, g, scale, shift):
    y = x * lax.rsqrt(jnp.mean(x * x, axis=-1, keepdims=True) + EPS)
    return (y * g) * (1.0 + scale) + shift


def _inproj_body(x_ref, pe_ref, sc_ref, sh_ref, g_ref, wm_ref, wg_ref, zm_ref, zg_ref, gt_ref):
    x = x_ref[0] + pe_ref[...]
    u = _rms_mod(x, g_ref[...], sc_ref[0], sh_ref[0]).astype(BF16)
    z = _dot(u, wm_ref[...])
    zm_ref[0] = z[:, :GROUP_W].astype(BF16)
    zg_ref[0] = z[:, GROUP_W:].astype(BF16)
    gt_ref[0] = _dot(u, wg_ref[...])


def _inproj(x, pe, scale, shift, g, w_main, w_gate, tm):
    bsz, t, d = x.shape
    per_batch = scale.shape[0] == bsz and bsz > 1
    mod_map = (lambda i, b: (b, 0, 0)) if per_batch else (lambda i, b: (0, 0, 0))
    return pl.pallas_call(
        _inproj_body,
        grid=(t // tm, bsz),
        in_specs=[pl.BlockSpec((1, tm, d), lambda i, b: (b, i, 0)),
                  pl.BlockSpec((tm, d), lambda i, b: (i, 0)),
                  pl.BlockSpec((1, 1, d), mod_map),
                  pl.BlockSpec((1, 1, d), mod_map),
                  pl.BlockSpec((1, d), lambda i, b: (0, 0)),
                  pl.BlockSpec((d, 2 * GROUP_W), lambda i, b: (0, 0)),
                  pl.BlockSpec((d, GATE_W), lambda i, b: (0, 0))],
        out_specs=[pl.BlockSpec((1, tm, GROUP_W), lambda i, b: (b, i, 0)),
                   pl.BlockSpec((1, tm, GROUP_W), lambda i, b: (b, i, 0)),
                   pl.BlockSpec((1, tm, GATE_W), lambda i, b: (b, i, 0))],
        out_shape=[jax.ShapeDtypeStruct((bsz, t, GROUP_W), BF16),
                   jax.ShapeDtypeStruct((bsz, t, GROUP_W), BF16),
                   jax.ShapeDtypeStruct((bsz, t, GATE_W), F32)],
        compiler_params=_params(("arbitrary", "arbitrary")),
        name="inproj",
    )(x, pe, scale, shift, g, w_main, w_gate)


def _scan_rows(x, op, ident, reverse, rowi):
    n = x.shape[0]
    k = 1
    while k < n:
        if reverse:
            shifted = jnp.where(rowi < n - k, pltpu.roll(x, n - k, axis=0), ident)
        else:
            shifted = jnp.where(rowi >= k, pltpu.roll(x, k, axis=0), ident)
        x = op(x, shifted)
        k *= 2
    return x


HEAD_DIRS = 2 * N_HEADS


def _mlstm_body(zl_ref, zc_ref, gl_ref, gc_ref, cw_ref, cb_ref, gb_ref, ng_ref, o_ref,
                q_s, kt_s, acc_s, cc_s, bc_s, st_s, mp_s, m_s, en_s, cr_s, wk_s, wc_s, c_s, *, t_lat, t_ctx):
    L = ML_CHUNK
    nc_ctx = t_ctx // L
    nc_lat = t_lat // L
    nc = nc_ctx + nc_lat
    scale = DK ** -0.5

    cw = cw_ref[...]
    cb = cb_ref[...]

    def conv_pass(z_ref, n, dst0):
        nb = n // CONV_ROWS
        row = lax.broadcasted_iota(jnp.int32, (CONV_ROWS, QK_W), 0)

        def body(r, carry):
            r0 = pl.multiple_of(r * CONV_ROWS, CONV_ROWS)
            zc = z_ref[0, pl.ds(r0, CONV_ROWS), 0:QK_W].astype(F32)
            p0 = pl.multiple_of(jnp.maximum(r0 - BF16_ROWS, 0), BF16_ROWS)
            prev = z_ref[0, pl.ds(p0, BF16_ROWS), 0:QK_W].astype(F32)[BF16_ROWS - 1:BF16_ROWS]
            prev = jnp.where(r > 0, prev, 0.0)
            n0 = pl.multiple_of(jnp.minimum(r0 + CONV_ROWS, n - BF16_ROWS), BF16_ROWS)
            nxt = z_ref[0, pl.ds(n0, BF16_ROWS), 0:QK_W].astype(F32)[0:1]
            nxt = jnp.where(r < nb - 1, nxt, 0.0)
            up = jnp.where(row == 0, prev, pltpu.roll(zc, 1, axis=0))
            dn = jnp.where(row == CONV_ROWS - 1, nxt, pltpu.roll(zc, CONV_ROWS - 1, axis=0))
            y = cw[0:1] * up + cw[1:2] * zc + cw[2:3] * dn + cb
            y = y * _sigmoid(y)
            d0 = pl.multiple_of(dst0 + r0, CONV_ROWS)
            q_s[pl.ds(d0, CONV_ROWS), :] = (y * scale).astype(BF16)
            for h in range(N_HEADS):
                kt_s[h, :, pl.ds(d0, CONV_ROWS)] = y[:, h * LANES:(h + 1) * LANES].T[DK:2 * DK].astype(BF16)
            return carry

        lax.fori_loop(0, nb, body, 0, unroll=2)

    conv_pass(zc_ref, t_ctx, 0)
    conv_pass(zl_ref, t_lat, t_ctx)

    rowi = lax.broadcasted_iota(jnp.int32, (L, LANES), 0)
    lane_g = lax.broadcasted_iota(jnp.int32, (L, GATE_W), 1)
    fwd_lane = lane_g < N_HEADS
    gb = gb_ref[...]

    def gate_pass(g_ref, nchunks, g0):
        def body(i, carry):
            r0 = pl.multiple_of(i * L, L)
            gates = g_ref[0, pl.ds(r0, L), :] + gb
            lf = pltpu.roll(_log_sigmoid(gates), GATE_W - HEAD_DIRS, axis=1)
            b = jnp.where(fwd_lane, _scan_rows(lf, jnp.add, 0.0, False, rowi),
                          _scan_rows(lf, jnp.add, 0.0, True, rowi))
            c = gates - b
            cc_s[g0 + i] = c
            bc_s[g0 + i] = b
            st_s[g0 + i, 0:1, :] = jnp.max(c, axis=0, keepdims=True)
            st_s[g0 + i, 1:2, :] = jnp.sum(lf, axis=0, keepdims=True)
            return carry

        lax.fori_loop(0, nchunks, body, 0, unroll=2)

    gate_pass(gc_ref, nc_ctx, 0)
    gate_pass(gl_ref, nc_lat, nc_ctx)

    lane1 = lax.broadcasted_iota(jnp.int32, (1, GATE_W), 1)
    fwd_order = list(range(nc))
    bwd_order = list(range(nc_ctx - 1, -1, -1)) + list(range(nc - 1, nc_ctx - 1, -1))
    mp = {}
    for d, order in enumerate((fwd_order, bwd_order)):
        m = jnp.zeros((1, GATE_W), F32)
        for g in order:
            mp[(d, g)] = m
            m = st_s[g, 1:2, :] + jnp.maximum(m, st_s[g, 0:1, :])
    for g in range(nc):
        mp_s[g, 0:1, :] = jnp.where(lane1 < N_HEADS, mp[(0, g)], mp[(1, g)])

    def ab_body(g, carry):
        c = cc_s[g]
        b = bc_s[g]
        mprev = mp_s[g, 0:1, :]
        mloc = jnp.where(fwd_lane, _scan_rows(c, jnp.maximum, -jnp.inf, False, rowi),
                         _scan_rows(c, jnp.maximum, -jnp.inf, True, rowi))
        m = jnp.maximum(mloc, mprev)
        mlast = jnp.maximum(mprev, st_s[g, 0:1, :])
        m_s[g] = m
        en_s[g] = jnp.exp(-(b + m))
        cr_s[g] = c.T[0:HEAD_DIRS]
        wk_s[g] = jnp.exp((c - mlast).T[0:HEAD_DIRS])
        wc_s[g] = jnp.exp(jnp.broadcast_to(mprev - mlast, (L, GATE_W)).T[0:HEAD_DIRS])
        return carry

    lax.fori_loop(0, nc, ab_body, 0, unroll=2)

    colL = lax.broadcasted_iota(jnp.int32, (L, L), 1)
    rowL = lax.broadcasted_iota(jnp.int32, (L, L), 0)
    masks = (colL <= rowL, colL >= rowL)
    ones_blk = jnp.ones((L, DV), BF16)
    c_s[...] = jnp.zeros(c_s.shape, F32)

    def twice(x):
        return jnp.concatenate([x, x], axis=1)

    def step(g, z_ref, r_src, r_q, d, with_out, r_out):
        for h in range(N_HEADS):
            j = d * N_HEADS + h
            hs = slice(h * DV, (h + 1) * DV)
            kt = kt_s[h, :, pl.ds(r_q, L)]
            v = z_ref[0, pl.ds(r_src, L), QK_W + h * DV:QK_W + (h + 1) * DV]
            vext = jnp.concatenate([v, ones_blk], axis=1)
            state = c_s[j]
            if with_out:
                q = q_s[pl.ds(r_q, L), hs][:, 0:DK]
                mb = jnp.broadcast_to(m_s[g][:, j:j + 1], (L, L))
                w = jnp.where(masks[d], jnp.exp(cr_s[g, j:j + 1, :] - mb), 0.0)
                w_int = jnp.exp(mp_s[g][0:1, j:j + 1] - mb[:, 0:DK])
                p = jnp.concatenate([(_dot(q, kt) * w).astype(BF16), (q.astype(F32) * w_int).astype(BF16)], axis=1)
                nd = _dot(p, jnp.concatenate([vext, state.astype(BF16)], axis=0))
                hh = nd[:, :DV] / jnp.maximum(jnp.abs(nd[:, DV:]), en_s[g][:, j:j + 1])
                acc_s[d, pl.ds(r_out, L), hs] = hh
            ktw = (kt.astype(F32) * wk_s[g, j:j + 1, :]).astype(BF16)
            c_s[j] = twice(wc_s[g, j:j + 1, :]) * state + _dot(ktw, vext)

    def ctx_body(i, carry):
        for d in (0, 1):
            ii = i if d == 0 else nc_ctx - 1 - i
            r0 = pl.multiple_of(ii * L, L)
            step(ii, zc_ref, r0, r0, d, False, None)
        return carry

    def lat_body(i, carry):
        for d in (0, 1):
            ii = i if d == 0 else nc_lat - 1 - i
            r0 = pl.multiple_of(ii * L, L)
            step(nc_ctx + ii, zl_ref, r0, pl.multiple_of(t_ctx + ii * L, L), d, True, r0)
        return carry

    lax.fori_loop(0, nc_ctx, ctx_body, 0)
    lax.fori_loop(0, nc_lat, lat_body, 0)

    def out_body(i, carry):
        r0 = pl.multiple_of(i * L, L)
        for h in range(N_HEADS):
            hs = slice(h * DV, (h + 1) * DV)
            tot = acc_s[0, pl.ds(r0, L), hs] + acc_s[1, pl.ds(r0, L), hs]
            tot = tot * lax.rsqrt(jnp.mean(tot * tot, axis=-1, keepdims=True) + EPS)
            og = zl_ref[0, pl.ds(r0, L), QK_W + V_W + h * DV:QK_W + V_W + (h + 1) * DV].astype(F32)
            o_ref[0, pl.ds(r0, L), hs] = (tot * ng_ref[:, hs] * _sigmoid(og)).astype(BF16)
        return carry

    lax.fori_loop(0, nc_lat, out_body, 0, unroll=2)


def _mlstm(zm_lat, zm_ctx, gt_lat, gt_ctx, conv_w, conv_b, gate_b, norm_g):
    bsz, t_lat, _ = zm_lat.shape
    t_ctx = zm_ctx.shape[1]
    nc = (t_lat + t_ctx) // ML_CHUNK
    body = functools.partial(_mlstm_body, t_lat=t_lat, t_ctx=t_ctx)
    full = lambda b: (0, 0)
    return pl.pallas_call(
        body,
        grid=(bsz,),
        in_specs=[pl.BlockSpec((1, t_lat, GROUP_W), lambda b: (b, 0, 0)),
                  pl.BlockSpec((1, t_ctx, GROUP_W), lambda b: (b, 0, 0)),
                  pl.BlockSpec((1, t_lat, GATE_W), lambda b: (b, 0, 0)),
                  pl.BlockSpec((1, t_ctx, GATE_W), lambda b: (b, 0, 0)),
                  pl.BlockSpec((ML_CONV, QK_W), full),
                  pl.BlockSpec((1, QK_W), full),
                  pl.BlockSpec((1, GATE_W), full),
                  pl.BlockSpec((1, V_W), full)],
        out_specs=pl.BlockSpec((1, t_lat, V_W), lambda b: (b, 0, 0)),
        out_shape=jax.ShapeDtypeStruct((bsz, t_lat, V_W), BF16),
        scratch_shapes=[pltpu.VMEM((t_ctx + t_lat, QK_W), BF16),
                        pltpu.VMEM((N_HEADS, DK, t_ctx + t_lat), BF16),
                        pltpu.VMEM((2, t_lat, V_W), F32),
                        pltpu.VMEM((nc, ML_CHUNK, GATE_W), F32),
                        pltpu.VMEM((nc, ML_CHUNK, GATE_W), F32),
                        pltpu.VMEM((nc, SUBLANES, GATE_W), F32),
                        pltpu.VMEM((nc, SUBLANES, GATE_W), F32),
                        pltpu.VMEM((nc, ML_CHUNK, GATE_W), F32),
                        pltpu.VMEM((nc, ML_CHUNK, GATE_W), F32),
                        pltpu.VMEM((nc, HEAD_DIRS, ML_CHUNK), F32),
                        pltpu.VMEM((nc, HEAD_DIRS, ML_CHUNK), F32),
                        pltpu.VMEM((nc, HEAD_DIRS, GATE_W), F32),
                        pltpu.VMEM((HEAD_DIRS, DK, 2 * DV), F32)],
        compiler_params=_params(("arbitrary",)),
        name="mlstm",
    )(zm_lat, zm_ctx, gt_lat, gt_ctx, conv_w, conv_b, gate_b, norm_g)


def _gla_body(zl_ref, zc_ref, gl_ref, gc_ref, w2_ref, b2_ref, ng_ref, o_ref,
              acc_s, s_s, qt_s, kt_s, ee_s, *, t_lat, t_ctx):
    L = GLA_CHUNK
    nc_ctx = t_ctx // L
    nc_lat = t_lat // L
    scale = DK ** -0.5
    half = N_HEADS * DK

    rowi = lax.broadcasted_iota(jnp.int32, (L, L), 0)
    coli = lax.broadcasted_iota(jnp.int32, (L, L), 1)
    masks = (coli <= rowi, coli >= rowi)
    tri_b = (masks[0].astype(BF16), masks[1].astype(BF16))
    low = lax.broadcasted_iota(jnp.int32, (L, LANES), 1) < DK
    w2_hi, w2_lo = _split2(w2_ref[...])
    bias2 = b2_ref[...]

    def prep(z_ref, g_ref, n_rows, c0, row0):
        grp = math.gcd(n_rows, GLA_PREP_ROWS)

        def body(gi, carry):
            rg = pl.multiple_of(gi * grp, grp)
            g_hi, g_lo = _split2(g_ref[0, pl.ds(rg, grp), :])
            pre = _dot(g_hi, w2_hi) + _dot(g_lo, w2_hi) + _dot(g_hi, w2_lo) + bias2
            la_hi, la_lo = _split2(_log_sigmoid(pre) * (1.0 / GLA_NORMALIZER))
            for cc in range(grp // L):
                rs = slice(cc * L, (cc + 1) * L)
                r0 = pl.multiple_of(rg + cc * L, L)
                rq = pl.multiple_of(row0 + rg + cc * L, L)
                c = c0 + gi * (grp // L) + cc
                for d in (0, 1):
                    ds_ = slice(d * half, (d + 1) * half)
                    b_all = _dot(tri_b[d], la_hi[rs, ds_]) + _dot(tri_b[d], la_lo[rs, ds_])
                    for hp in range(N_HEADS // 2):
                        pair = b_all[:, hp * LANES:(hp + 1) * LANES]
                        swapped = pltpu.roll(pair, DK, axis=1)
                        for hh in range(2):
                            h = 2 * hp + hh
                            hs = slice(h * LANES, (h + 1) * LANES)
                            b2 = jnp.where(low, pair, swapped) if hh == 0 else jnp.where(low, swapped, pair)
                            b_end = b2[L - 1:L] if d == 0 else b2[0:1]
                            qk = z_ref[0, pl.ds(r0, L), hs].astype(F32)
                            qkt = qk * jnp.exp(jnp.where(low, b2, -b2))
                            qt_s[d, pl.ds(rq, L), hs] = (qkt * scale).astype(BF16)
                            kt_s[d, pl.ds(rq, L), hs] = jnp.where(low, pltpu.roll(qkt, DK, axis=1), 0.0).astype(BF16)
                            ee_s[d, c, 0:1, hs] = jnp.exp(b_end)
            return carry

        lax.fori_loop(0, n_rows // grp, body, 0)

    prep(zc_ref, gc_ref, t_ctx, 0, 0)
    prep(zl_ref, gl_ref, t_lat, nc_ctx, t_ctx)

    s_s[...] = jnp.zeros(s_s.shape, F32)

    def step(z_ref, r0, rq, c, d, with_out):
        for h in range(N_HEADS):
            j = d * N_HEADS + h
            hs = slice(h * LANES, (h + 1) * LANES)
            qt = qt_s[d, pl.ds(rq, L), hs]
            kt = kt_s[d, pl.ds(rq, L), hs]
            v = z_ref[0, pl.ds(r0, L), QK_W + h * DV:QK_W + (h + 1) * DV]
            state = s_s[j]
            if with_out:
                a = jnp.where(masks[d], _dot_nt(qt, kt), 0.0).astype(BF16)
                acc_s[d, pl.ds(r0, L), hs] = _dot(a, v) + _dot_nt(qt, state.astype(BF16))
            e_end = ee_s[d, c, 0:1, hs]
            s_s[j] = state * e_end + _dot_tn(v, (kt.astype(F32) * e_end).astype(BF16))

    def loop(z_ref, n, c0, row0, with_out):
        def body(i, carry):
            for d in (0, 1):
                ii = i if d == 0 else n - 1 - i
                r0 = pl.multiple_of(ii * L, L)
                step(z_ref, r0, pl.multiple_of(row0 + ii * L, L), c0 + ii, d, with_out)
            return carry
        lax.fori_loop(0, n, body, 0, unroll=4)

    loop(zc_ref, nc_ctx, 0, 0, False)
    loop(zl_ref, nc_lat, nc_ctx, t_ctx, True)

    def out_body(i, carry):
        r0 = pl.multiple_of(i * L, L)
        for h in range(N_HEADS):
            hs = slice(h * LANES, (h + 1) * LANES)
            tot = acc_s[0, pl.ds(r0, L), hs] + acc_s[1, pl.ds(r0, L), hs]
            tot = tot * lax.rsqrt(jnp.mean(tot * tot, axis=-1, keepdims=True) + EPS)
            gg = zl_ref[0, pl.ds(r0, L), QK_W + V_W + h * DV:QK_W + V_W + (h + 1) * DV].astype(F32)
            o_ref[0, pl.ds(r0, L), hs] = (tot * ng_ref[:, hs] * (gg * _sigmoid(gg))).astype(BF16)
        return carry

    lax.fori_loop(0, nc_lat, out_body, 0, unroll=2)


def _gla(zg_lat, zg_ctx, gt_lat, gt_ctx, w2_ext, b2_ext, norm_g):
    bsz, t_lat, _ = zg_lat.shape
    t_ctx = zg_ctx.shape[1]
    body = functools.partial(_gla_body, t_lat=t_lat, t_ctx=t_ctx)
    full = lambda b: (0, 0)
    return pl.pallas_call(
        body,
        grid=(bsz,),
        in_specs=[pl.BlockSpec((1, t_lat, GROUP_W), lambda b: (b, 0, 0)),
                  pl.BlockSpec((1, t_ctx, GROUP_W), lambda b: (b, 0, 0)),
                  pl.BlockSpec((1, t_lat, GATE_W), lambda b: (b, 0, 0)),
                  pl.BlockSpec((1, t_ctx, GATE_W), lambda b: (b, 0, 0)),
                  pl.BlockSpec((GATE_W, 2 * N_HEADS * DK), full),
                  pl.BlockSpec((1, 2 * N_HEADS * DK), full),
                  pl.BlockSpec((1, V_W), full)],
        out_specs=pl.BlockSpec((1, t_lat, V_W), lambda b: (b, 0, 0)),
        out_shape=jax.ShapeDtypeStruct((bsz, t_lat, V_W), BF16),
        scratch_shapes=[pltpu.VMEM((2, t_lat, V_W), F32),
                        pltpu.VMEM((HEAD_DIRS, DV, LANES), F32),
                        pltpu.VMEM((2, t_ctx + t_lat, QK_W), BF16),
                        pltpu.VMEM((2, t_ctx + t_lat, QK_W), BF16),
                        pltpu.VMEM((2, (t_ctx + t_lat) // GLA_CHUNK, SUBLANES, QK_W), F32)],
        compiler_params=_params(("arbitrary",)),
        name="gla",
    )(zg_lat, zg_ctx, gt_lat, gt_ctx, w2_ext, b2_ext, norm_g)


def _outproj_body(x_ref, pe_ref, a_ref, g_ref, wo_ref, g1_ref, sc_ref, sh_ref, n2_ref, rw_ref, rb_ref,
                  x1_ref, u2_ref, route_ref, cnt_ref):
    x = x_ref[0] + pe_ref[...]
    y = _dot(a_ref[0], wo_ref[0:V_W, :]) + _dot(g_ref[0], wo_ref[V_W:2 * V_W, :])
    x1 = x + g1_ref[0] * y
    x1_ref[0] = x1
    u2 = _rms_mod(x1, n2_ref[...], sc_ref[0], sh_ref[0])
    u_hi, u_lo = _split2(u2)
    u2_ref[0] = u_hi
    w_hi, w_lo = _split2(rw_ref[...])
    hh = _dot(u_hi, jnp.concatenate([w_hi, w_lo], axis=1))
    logits = hh[:, :LANES] + _dot(u_lo, w_hi) + hh[:, LANES:] + rb_ref[...]
    lane = lax.broadcasted_iota(jnp.int32, logits.shape, 1).astype(F32)
    ids = []
    vals = []
    for _ in range(TOP_K):
        m = jnp.max(logits, axis=1, keepdims=True)
        idx = jnp.min(jnp.where(logits == m, lane, float(LANES)), axis=1, keepdims=True)
        ids.append(idx)
        vals.append(m)
        logits = jnp.where(lane == idx, -jnp.inf, logits)
    exps = [jnp.exp(v - vals[0]) for v in vals]
    denom = exps[0]
    for e in exps[1:]:
        denom = denom + e
    route = jnp.zeros(logits.shape, F32)
    for k in range(TOP_K):
        route = jnp.where(lane == float(k), ids[k], route)
        route = jnp.where(lane == float(TOP_K + k), exps[k] / denom, route)
    route_ref[0] = route

    @pl.when((pl.program_id(0) == 0) & (pl.program_id(1) == 0))
    def _():
        cnt_ref[...] = jnp.zeros(cnt_ref.shape, F32)

    cnt_ref[...] += _block_picks(route)[3]


def _outproj(x, pe, a, g, w_out, gate1, scale2, shift2, norm2_g, router_w, router_b, tm):
    bsz, t, d = x.shape
    tok = lambda i, b: (b, i, 0)
    mod = lambda i, b: (b, 0, 0)
    full = lambda i, b: (0, 0)
    return pl.pallas_call(
        _outproj_body,
        grid=(t // tm, bsz),
        in_specs=[pl.BlockSpec((1, tm, d), tok),
                  pl.BlockSpec((tm, d), lambda i, b: (i, 0)),
                  pl.BlockSpec((1, tm, V_W), tok),
                  pl.BlockSpec((1, tm, V_W), tok),
                  pl.BlockSpec((2 * V_W, d), full),
                  pl.BlockSpec((1, 1, d), mod),
                  pl.BlockSpec((1, 1, d), mod),
                  pl.BlockSpec((1, 1, d), mod),
                  pl.BlockSpec((1, d), full),
                  pl.BlockSpec((d, LANES), full),
                  pl.BlockSpec((1, LANES), full)],
        out_specs=[pl.BlockSpec((1, tm, d), tok),
                   pl.BlockSpec((1, tm, d), tok),
                   pl.BlockSpec((1, tm, LANES), tok),
                   pl.BlockSpec((1, LANES), full)],
        out_shape=[jax.ShapeDtypeStruct((bsz, t, d), F32),
                   jax.ShapeDtypeStruct((bsz, t, d), BF16),
                   jax.ShapeDtypeStruct((bsz, t, LANES), F32),
                   jax.ShapeDtypeStruct((1, LANES), F32)],
        compiler_params=_params(("arbitrary", "arbitrary")),
        name="outproj_router",
    )(x, pe, a, g, w_out, gate1, scale2, shift2, norm2_g, router_w, router_b)


def _lane_excl_cumsum(x):
    x8 = jnp.broadcast_to(x, (SUBLANES, LANES))
    lane8 = lax.broadcasted_iota(jnp.int32, (SUBLANES, LANES), 1)
    inc = x8
    s = 1
    while s < LANES:
        inc = inc + jnp.where(lane8 >= s, pltpu.roll(inc, s, axis=1), 0.0)
        s *= 2
    return (inc - x8)[0:1]


def _block_picks(route):
    lane = lax.broadcasted_iota(jnp.int32, route.shape, 1).astype(F32)
    onehots = [lane == route[:, k:k + 1] for k in range(TOP_K)]
    oh = onehots[0].astype(F32)
    for o in onehots[1:]:
        oh = oh + o.astype(F32)
    run_len = jnp.ceil(jnp.sum(oh, axis=0, keepdims=True) * (1.0 / RUN_ALIGN)) * float(RUN_ALIGN)
    return lane, onehots, oh, run_len


def _route_pos_body(route_ref, cnt_ref, lic_ref, lit_ref, tab_ref, off_s, *, tile):
    tb = route_ref.shape[0]
    lane, onehots, oh, run_len = _block_picks(route_ref[...])

    @pl.when(pl.program_id(0) == 0)
    def _():
        cpad = jnp.floor((cnt_ref[...] + float(tile - 1)) * (1.0 / tile)) * float(tile)
        off_s[...] = _lane_excl_cumsum(cpad)

    r = lax.broadcasted_iota(jnp.int32, (tb, tb), 0)
    c = lax.broadcasted_iota(jnp.int32, (tb, tb), 1)
    before = _dot((c < r).astype(BF16), oh.astype(BF16))
    lstart = _lane_excl_cumsum(run_len)
    local = lstart + before
    li = jnp.zeros((tb, LANES), F32)
    for k in range(TOP_K):
        lk = jnp.sum(jnp.where(onehots[k], local, 0.0), axis=1, keepdims=True)
        li = jnp.where(lane == float(k), lk, li)
    lic_ref[...] = li
    lit_ref[0] = li.T[0:SUBLANES]
    row8 = lax.broadcasted_iota(jnp.int32, (SUBLANES, LANES), 0)
    total = jnp.sum(run_len, axis=1, keepdims=True)
    tab = jnp.where(row8 == 0, off_s[...],
                    jnp.where(row8 == 1, run_len, jnp.where(row8 == 2, lstart, jnp.where(row8 == 3, total, 0.0))))
    tab_ref[0] = tab.astype(jnp.int32)
    off_s[...] += run_len


def _route_pos(route, counts, tile):
    n = route.shape[0]
    tb = ROUTE_BLOCK
    nb = n // tb
    return pl.pallas_call(
        functools.partial(_route_pos_body, tile=tile),
        grid=(nb,),
        in_specs=[pl.BlockSpec((tb, LANES), lambda j: (j, 0)),
                  pl.BlockSpec((1, LANES), lambda j: (0, 0))],
        out_specs=[pl.BlockSpec((tb, LANES), lambda j: (j, 0)),
                   pl.BlockSpec((1, SUBLANES, tb), lambda j: (j, 0, 0)),
                   pl.BlockSpec((1, SUBLANES, LANES), lambda j: (j, 0, 0))],
        out_shape=[jax.ShapeDtypeStruct((n, LANES), F32),
                   jax.ShapeDtypeStruct((nb, SUBLANES, tb), F32),
                   jax.ShapeDtypeStruct((nb, SUBLANES, LANES), jnp.int32)],
        scratch_shapes=[pltpu.VMEM((1, LANES), F32)],
        compiler_params=_params(("arbitrary",)),
        name="route_pos",
    )(route, counts)


def _issue_runs(tab_ref, n_experts, make_copy):
    def per_expert(e, carry):
        g = tab_ref[0, 0, e]
        cnt = tab_ref[0, 1, e]
        l = tab_ref[0, 2, e]
        for size in RUN_SIZES:
            part = cnt & size

            @pl.when(part != 0)
            def _():
                make_copy(pl.multiple_of(l, RUN_ALIGN), pl.multiple_of(g, RUN_ALIGN), size).start()

            l = l + part
            g = g + part
        return carry

    lax.fori_loop(0, n_experts, per_expert, 0)


def _dispatch_body(zrow_ref, zflag_ref, nu_ref, tab_ref, lit_ref, u_ref, xs_ref, buf, zero_s, tot_s, sem, zsem,
                   *, n_experts, tile):
    i = pl.program_id(0)
    nb = pl.num_programs(0)
    tb = u_ref.shape[0]
    rows = buf.shape[1]
    n_tiles = xs_ref.shape[0] // tile
    slot = i % 2

    @pl.when(i == 0)
    def _():
        zero_s[...] = jnp.zeros(zero_s.shape, F32)

        def zero_tile(r0):
            cp = pltpu.make_async_copy(zero_s, xs_ref.at[pl.ds(pl.multiple_of(r0, tile), tile)], zsem)
            cp.start()
            cp.wait()

        for e in range(n_experts):
            @pl.when(zflag_ref[e] == 1)
            def _():
                zero_tile(zrow_ref[e])

        def tail(t, carry):
            zero_tile(t * tile)
            return carry

        lax.fori_loop(nu_ref[0], n_tiles, tail, 0)

    def wait_slot(s):
        nrows = pl.multiple_of(tot_s[s], RUN_ALIGN)
        pltpu.make_async_copy(buf.at[s, pl.ds(0, nrows)], xs_ref.at[pl.ds(0, nrows)], sem.at[s]).wait()

    @pl.when(i >= 2)
    def _():
        wait_slot(slot)

    tot_s[slot] = tab_ref[0, 3, 0]

    lit = lit_ref[0]
    ub = u_ref[...]
    ri0 = lax.broadcasted_iota(jnp.int32, (PERM_ROWS, tb), 0).astype(F32)
    for c in range(rows // PERM_ROWS):
        ri = ri0 + float(c * PERM_ROWS)
        hit = jnp.zeros((PERM_ROWS, tb), F32)
        for k in range(TOP_K):
            hit = jnp.where(ri == lit[k:k + 1], 1.0, hit)
        buf[slot, c * PERM_ROWS:(c + 1) * PERM_ROWS] = _dot(hit.astype(BF16), ub)
    _issue_runs(tab_ref, n_experts,
                lambda l, g, size: pltpu.make_async_copy(buf.at[slot, pl.ds(l, size)], xs_ref.at[pl.ds(g, size)],
                                                         sem.at[slot]))

    @pl.when(i == nb - 1)
    def _():
        wait_slot(slot)

    @pl.when((i == nb - 1) & (nb > 1))
    def _():
        wait_slot(1 - slot)


def _dispatch(u2, tab, lit, zrow, zflag, n_used, n_rows, tile):
    n, d = u2.shape
    tb = ROUTE_BLOCK
    nb = n // tb
    n_experts = zrow.shape[0]
    grid_spec = pltpu.PrefetchScalarGridSpec(
        num_scalar_prefetch=3,
        grid=(nb,),
        in_specs=[pl.BlockSpec((1, SUBLANES, LANES), lambda i, zr, zf, nu: (i, 0, 0), memory_space=pltpu.SMEM),
                  pl.BlockSpec((1, SUBLANES, tb), lambda i, zr, zf, nu: (i, 0, 0)),
                  pl.BlockSpec((tb, d), lambda i, zr, zf, nu: (i, 0))],
        out_specs=pl.BlockSpec(memory_space=pl.ANY),
        scratch_shapes=[pltpu.VMEM((2, tb * TOP_K + n_experts * RUN_ALIGN, d), F32), pltpu.VMEM((tile, d), F32),
                        pltpu.SMEM((2,), jnp.int32), pltpu.SemaphoreType.DMA((2,)), pltpu.SemaphoreType.DMA(())],
    )
    return pl.pallas_call(
        functools.partial(_dispatch_body, n_experts=n_experts, tile=tile),
        grid_spec=grid_spec,
        out_shape=jax.ShapeDtypeStruct((n_rows, d), F32),
        compiler_params=_params(("arbitrary",)),
        name="dispatch",
    )(zrow, zflag, n_used, tab, lit, u2)


def _experts_body(te_ref, nu_ref, x_ref, wgu_ref, bgu_ref, wd_ref, bd_ref, y_ref, wgu_s, wd_s, *, d_expert):
    t = pl.program_id(0)

    @pl.when((t == 0) | (te_ref[t] != te_ref[jnp.maximum(t - 1, 0)]))
    def _():
        wgu_s[...] = wgu_ref[0].astype(BF16)
        wd_s[...] = wd_ref[0].astype(BF16)

    @pl.when(t < nu_ref[0])
    def _():
        gu = _dot(x_ref[...].astype(BF16), wgu_s[...]) + bgu_ref[0]
        gate = jnp.minimum(gu[:, :d_expert], SWIGLU_LIMIT)
        up = jnp.clip(gu[:, d_expert:], -SWIGLU_LIMIT, SWIGLU_LIMIT)
        act = ((up + 1.0) * (gate * _sigmoid(SWIGLU_ALPHA * gate))).astype(BF16)
        y_ref[...] = _dot(act, wd_s[...]) + bd_ref[0]

    @pl.when(t >= nu_ref[0])
    def _():
        y_ref[...] = jnp.zeros(y_ref.shape, F32)


def _experts(xs, tile_expert, n_used, w_gu, b_gu, w_down, b_down, tile):
    n_rows, d = xs.shape
    n_experts, _, two_de = w_gu.shape
    d_expert = two_de // 2
    grid_spec = pltpu.PrefetchScalarGridSpec(
        num_scalar_prefetch=2,
        grid=(n_rows // tile,),
        in_specs=[pl.BlockSpec((tile, d), lambda t, te, nu: (jnp.minimum(t, nu[0] - 1), 0)),
                  pl.BlockSpec((1, d, two_de), lambda t, te, nu: (te[t], 0, 0)),
                  pl.BlockSpec((1, 1, two_de), lambda t, te, nu: (te[t], 0, 0)),
                  pl.BlockSpec((1, d_expert, d), lambda t, te, nu: (te[t], 0, 0)),
                  pl.BlockSpec((1, 1, d), lambda t, te, nu: (te[t], 0, 0))],
        out_specs=pl.BlockSpec((tile, d), lambda t, te, nu: (t, 0)),
        scratch_shapes=[pltpu.VMEM((d, two_de), BF16), pltpu.VMEM((d_expert, d), BF16)],
    )
    return pl.pallas_call(
        functools.partial(_experts_body, d_expert=d_expert),
        grid_spec=grid_spec,
        out_shape=jax.ShapeDtypeStruct((n_rows, d), F32),
        compiler_params=_params(("arbitrary",)),
        name="experts",
    )(tile_expert, n_used, xs, w_gu, b_gu, w_down, b_down)


def _combine_body(tab_ref, tabn_ref, lic_ref, route_ref, x1_ref, g5_ref, fg_ref, ys_ref, o_ref, buf, sem,
                  *, n_experts):
    i = pl.program_id(0)
    n = pl.num_programs(0)
    tb = x1_ref.shape[0]
    rows = buf.shape[1]
    slot = i % 2

    def issue(t_ref, s):
        _issue_runs(t_ref, n_experts,
                    lambda l, g, size: pltpu.make_async_copy(ys_ref.at[pl.ds(g, size)], buf.at[s, pl.ds(l, size)],
                                                             sem.at[s]))

    @pl.when(i == 0)
    def _():
        buf[...] = jnp.zeros(buf.shape, F32)
        issue(tab_ref, 0)

    @pl.when(i + 1 < n)
    def _():
        issue(tabn_ref, 1 - slot)

    nrows = pl.multiple_of(tab_ref[0, 3, 0], RUN_ALIGN)
    pltpu.make_async_copy(ys_ref.at[pl.ds(0, nrows)], buf.at[slot, pl.ds(0, nrows)], sem.at[slot]).wait()

    lane = lax.broadcasted_iota(jnp.int32, (tb, LANES), 1)
    li_hi, li_lo = _split2(lic_ref[...])
    a = jnp.where(lane < TOP_K, li_hi.astype(F32),
                  jnp.where(lane < 2 * TOP_K, pltpu.roll(li_lo.astype(F32), TOP_K, axis=1),
                            jnp.where(lane < 3 * TOP_K, pltpu.roll(route_ref[...], TOP_K, axis=1), 0.0))).astype(BF16)
    rsel = lax.broadcasted_iota(jnp.int32, (LANES, LANES), 0)
    lane_f = lane.astype(F32)
    idx, gate = [], []
    for k in range(TOP_K):
        sel = jnp.concatenate([((rsel == k) | (rsel == TOP_K + k)).astype(BF16),
                               (rsel == 2 * TOP_K + k).astype(BF16)], axis=1)
        bk = _dot(a, sel)
        idx.append(bk[:, :LANES])
        gate.append(bk[:, LANES:])
    moe = jnp.zeros(x1_ref.shape, F32)
    for c0 in range(0, rows, PERM_ROWS):
        blocks = []
        for c in range(c0, c0 + PERM_ROWS, LANES):
            col = lane_f + float(c)
            w = jnp.zeros((tb, LANES), F32)
            for k in range(TOP_K):
                w = jnp.where(idx[k] == col, gate[k], w)
            blocks.append(w.astype(BF16))
        moe = moe + _dot(jnp.concatenate(blocks, axis=1), buf[slot, c0:c0 + PERM_ROWS].astype(BF16))
    xo = x1_ref[...] + g5_ref[0] * moe
    o_ref[...] = xo * lax.rsqrt(jnp.mean(xo * xo, axis=-1, keepdims=True) + EPS) * fg_ref[...]


def _combine(ys, tab, lic, route, x1, gate5, final_g, t_lat, n_experts):
    n, d = x1.shape
    tb = ROUTE_BLOCK
    nb = n // tb
    per_b = t_lat // tb
    return pl.pallas_call(
        functools.partial(_combine_body, n_experts=n_experts),
        grid=(nb,),
        in_specs=[pl.BlockSpec((1, SUBLANES, LANES), lambda i: (i, 0, 0), memory_space=pltpu.SMEM),
                  pl.BlockSpec((1, SUBLANES, LANES), lambda i: (jnp.minimum(i + 1, nb - 1), 0, 0),
                               memory_space=pltpu.SMEM),
                  pl.BlockSpec((tb, LANES), lambda i: (i, 0)),
                  pl.BlockSpec((tb, LANES), lambda i: (i, 0)),
                  pl.BlockSpec((tb, d), lambda i: (i, 0)),
                  pl.BlockSpec((1, 1, d), lambda i: (i // per_b, 0, 0)),
                  pl.BlockSpec((1, d), lambda i: (0, 0)),
                  pl.BlockSpec(memory_space=pl.ANY)],
        out_specs=pl.BlockSpec((tb, d), lambda i: (i, 0)),
        out_shape=jax.ShapeDtypeStruct((n, d), F32),
        scratch_shapes=[pltpu.VMEM((2, tb * TOP_K + n_experts * RUN_ALIGN, d), F32), pltpu.SemaphoreType.DMA((2,))],
        compiler_params=_params(("arbitrary",)),
        name="combine",
    )(tab, tab, lic, route, x1, gate5, final_g, ys)


def _moe(u2, route, counts, x1, w_gu, b_gu, w_down, b_down, gate5, final_g):
    bsz, t_lat, d = x1.shape
    n = bsz * t_lat
    n_experts = w_gu.shape[0]
    tile = EXPERT_TILE
    route2 = route.reshape(n, LANES)
    lic, lit, tab = _route_pos(route2, counts, tile)
    cnt = counts[0, :n_experts].astype(jnp.int32)
    cpad = ((cnt + tile - 1) // tile) * tile
    ends = jnp.cumsum(cpad)
    n_tiles = -(-(n * TOP_K + (n // ROUTE_BLOCK) * n_experts * RUN_ALIGN) // tile) + n_experts
    n_rows = n_tiles * tile
    tile_expert = jnp.minimum(
        jnp.sum(jnp.arange(n_tiles, dtype=jnp.int32)[:, None] * tile >= ends[None, :], axis=1), n_experts - 1
    ).astype(jnp.int32)
    n_used = (ends[-1:] // tile).astype(jnp.int32)
    zrow = jnp.maximum(ends - tile, 0).astype(jnp.int32)
    zflag = (cnt % tile != 0).astype(jnp.int32)
    xs = _dispatch(u2.reshape(n, d), tab, lit, zrow, zflag, n_used, n_rows, tile)
    ys = _experts(xs, tile_expert, n_used, w_gu, b_gu, w_down, b_down, tile)
    out = _combine(ys, tab, lic, route2, x1.reshape(n, d), gate5, final_g, t_lat, n_experts)
    return out.reshape(bsz, t_lat, d)


def _grid_sincos(rows, d):
    nf = d // 4
    omega = 1.0 / (POS_BASE ** (jnp.arange(nf, dtype=F32) / nf))
    r = jnp.broadcast_to(jnp.arange(rows, dtype=F32)[:, None, None] * omega, (rows, GRID_W, nf))
    cl = jnp.broadcast_to(jnp.arange(GRID_W, dtype=F32)[None, :, None] * omega, (rows, GRID_W, nf))
    pe = jnp.concatenate([jnp.sin(r), jnp.cos(r), jnp.sin(cl), jnp.cos(cl)], axis=-1)
    return pe.reshape(rows * GRID_W, d)


def _pack_qk(wq, wk):
    lead = wq.shape[:-1]
    q = wq.reshape(lead + (N_HEADS, DK))
    k = wk.reshape(lead + (N_HEADS, DK))
    return jnp.concatenate([q, k], axis=-1).reshape(lead + (QK_W,))


def _row_tile(t):
    for tm in (512, 256, 128):
        if t % tm == 0:
            return tm
    raise ValueError(f"sequence length {t} must be a multiple of 128")


def kernel(x, c, ctx, c_ctx, ada_w, ada_b, norm1_g, w_in, ml_conv_w, ml_conv_b, ml_gate_b, ml_norm_g,
           gla_gate_w2, gla_gate_b, gla_norm_g, w_out, norm2_g, router_w, router_b, moe_w_gu, moe_b_gu,
           moe_w_down, moe_b_down, final_norm_g):
    bsz, t_lat, d = x.shape
    t_ctx = ctx.shape[1]
    assert ada_w.shape[0] == 1, "single-layer block"
    assert t_lat % ML_CHUNK == 0 and t_ctx % ML_CHUNK == 0 and t_lat % GRID_W == 0 and t_lat % ROUTE_BLOCK == 0
    n_experts = router_w.shape[-1]
    assert n_experts <= LANES

    ml_qk, ml_v = N_HEADS * DK, N_HEADS * DV
    sizes = (ml_qk, ml_qk, ml_v, ml_v, 4 * N_HEADS, ml_qk, ml_qk, ml_v, ml_v, 2 * GLA_RANK)
    offs = [int(o) for o in np.cumsum(sizes)[:-1]]
    w_mq, w_mk, w_mv, w_mo, w_mg, w_gq, w_gk, w_gv, w_gg, w_glr = jnp.split(w_in[0], offs, axis=-1)
    w_main = jnp.concatenate([_pack_qk(w_mq, w_mk), w_mv, w_mo, _pack_qk(w_gq, w_gk), w_gv, w_gg],
                             axis=-1).astype(BF16)
    gate_perm = np.concatenate([np.arange(N_HEADS) + g * N_HEADS for g in (0, 2, 1, 3)])
    w_gate = jnp.concatenate([w_mg[:, gate_perm], w_glr,
                              jnp.zeros((d, GATE_W - 4 * N_HEADS - 2 * GLA_RANK), F32)], axis=-1).astype(BF16)
    gate_b = jnp.concatenate([ml_gate_b[0].reshape(-1)[gate_perm],
                              jnp.zeros((GATE_W - 4 * N_HEADS,), F32)]).reshape(1, GATE_W)
    conv_w = _pack_qk(ml_conv_w[0][:, :ml_qk], ml_conv_w[0][:, ml_qk:])
    conv_b = _pack_qk(ml_conv_b[0][:ml_qk], ml_conv_b[0][ml_qk:]).reshape(1, QK_W)
    gla_qk = N_HEADS * DK
    w2_ext = jnp.zeros((GATE_W, 2 * gla_qk), F32)
    for dd in range(2):
        r0 = 4 * N_HEADS + dd * GLA_RANK
        w2_ext = w2_ext.at[r0:r0 + GLA_RANK, dd * gla_qk:(dd + 1) * gla_qk].set(gla_gate_w2[0, dd])
    b2_ext = gla_gate_b[0].reshape(1, 2 * gla_qk)
    router_w_p = jnp.concatenate([router_w[0], jnp.zeros((d, LANES - n_experts), F32)], axis=-1)
    router_b_p = jnp.concatenate([router_b[0], jnp.full((LANES - n_experts,), -jnp.inf, F32)]).reshape(1, LANES)

    cond = jnp.concatenate([c, c_ctx[None, :], jnp.zeros(((-bsz - 1) % SUBLANES, d), F32)], axis=0)
    mod = _adaln(cond, ada_w[0], ada_b[0]).reshape(cond.shape[0], N_MOD, 1, d)
    m_lat = mod[:bsz]
    m_ctx = mod[bsz:bsz + 1]

    pe = _grid_sincos(t_lat // GRID_W, d)
    g1 = norm1_g[0].reshape(1, d)
    tm = _row_tile(t_lat)
    zm_lat, zg_lat, gt_lat = _inproj(x, pe, m_lat[:, 1], m_lat[:, 0], g1, w_main, w_gate, tm)
    zm_ctx, zg_ctx, gt_ctx = _inproj(ctx, jnp.zeros((t_ctx, d), F32), m_ctx[:, 1], m_ctx[:, 0], g1,
                                     w_main, w_gate, _row_tile(t_ctx))

    a = _mlstm(zm_lat, zm_ctx, gt_lat, gt_ctx, conv_w, conv_b, gate_b, ml_norm_g[0].reshape(1, V_W))
    g = _gla(zg_lat, zg_ctx, gt_lat, gt_ctx, w2_ext, b2_ext, gla_norm_g[0].reshape(1, V_W))

    assert tm == ROUTE_BLOCK
    x1, u2, route, counts = _outproj(x, pe, a, g, w_out[0].astype(BF16), m_lat[:, 2], m_lat[:, 4], m_lat[:, 3],
                                     norm2_g[0].reshape(1, d), router_w_p, router_b_p, tm)

    return _moe(u2, route, counts, x1, moe_w_gu[0], moe_b_gu[0][:, None, :], moe_w_down[0],
                moe_b_down[0][:, None, :], m_lat[:, 5], final_norm_g.reshape(1, d))
```
